```python
import math
import jax, jax.numpy as jnp
from jax import lax
import numpy as np


D_MODEL = 2048
BATCH = 4
SEQ = 8192
DEPTH = 4

GRID_W = 64
CTX_LEN = 256
EPS = 1e-6
FOURIER_GROUPS = 4
FOURIER_GROUP_DIM = D_MODEL // 16
FOURIER_DIM = FOURIER_GROUPS * FOURIER_GROUP_DIM
CHUNK = 128
SGU_GROUPS = 4
SGU_DIM = D_MODEL // 4
SGU_GROUP_DIM = SGU_DIM // SGU_GROUPS
DA_HEADS = 8
DA_HEAD_DIM = 64
DA_QK_DIM = 2 * DA_HEADS * DA_HEAD_DIM
DA_V_DIM = DA_HEADS * 2 * DA_HEAD_DIM
ATTN_SCALE = DA_HEAD_DIM ** -0.5
Q_BLOCK = 128
ROPE_BASE = 10000.0
N_BRANCH = 3
OFF_F = 0
OFF_U = OFF_F + FOURIER_DIM
OFF_V = OFF_U + SGU_DIM
OFF_Q = OFF_V + SGU_DIM
OFF_K = OFF_Q + DA_QK_DIM
OFF_VA = OFF_K + DA_QK_DIM
OFF_G = OFF_VA + DA_V_DIM
N_IN = OFF_G + N_BRANCH * D_MODEL
D_FF = ((8 * D_MODEL // 3 + 255) // 256) * 256
N_EXPERTS = 8
TOP_K = 2
N_DENSE = (DEPTH + 1) // 2
N_MOE = DEPTH // 2

kernel_name = 'hybrid_fourier_sgu_diffattn_moe_dit'


def rmsnorm(x, g):
    xf = x.astype(jnp.float32)
    y = xf * lax.rsqrt(jnp.mean(jnp.square(xf), axis=-1, keepdims=True) + EPS)
    return (y * g.astype(jnp.float32)).astype(x.dtype)


def layernorm(x, g):
    xf = x.astype(jnp.float32)
    xc = xf - jnp.mean(xf, axis=-1, keepdims=True)
    y = xc * lax.rsqrt(jnp.mean(jnp.square(xc), axis=-1, keepdims=True) + EPS)
    return (y * g.astype(jnp.float32)).astype(x.dtype)


def modulate(h, shift, scale):
    return h * (1.0 + scale) + shift


def axial_rope(rows):
    r = jnp.repeat(jnp.arange(rows, dtype=jnp.float32), GRID_W)
    col = jnp.tile(jnp.arange(GRID_W, dtype=jnp.float32), rows)
    n_freq = DA_HEAD_DIM // 4
    inv = ROPE_BASE ** (-jnp.arange(n_freq, dtype=jnp.float32) / n_freq)
    ang = jnp.concatenate([r[:, None] * inv, col[:, None] * inv], axis=-1)
    return jnp.cos(ang), jnp.sin(ang)


def apply_rope(t, cos, sin):
    tf = t.astype(jnp.float32)
    t1, t2 = jnp.split(tf, 2, axis=-1)
    return jnp.concatenate([t1 * cos - t2 * sin, t1 * sin + t2 * cos], axis=-1).astype(t.dtype)


def split_q(p_q):
    b, n, _ = p_q.shape
    return p_q.reshape(b, n, 2 * DA_HEADS, DA_HEAD_DIM).transpose(0, 2, 1, 3)


def split_kv(p_kv):
    b, n, _ = p_kv.shape
    k = p_kv[..., :DA_QK_DIM].reshape(b, n, 2 * DA_HEADS, DA_HEAD_DIM).transpose(0, 2, 1, 3)
    v = p_kv[..., DA_QK_DIM:].reshape(b, n, DA_HEADS, 2 * DA_HEAD_DIM).transpose(0, 2, 1, 3)
    return k, v


def fourier_mix(f):
    b, n, _ = f.shape
    fg = f.reshape(b, n, FOURIER_GROUPS, FOURIER_GROUP_DIM).astype(jnp.float32)
    out = jnp.fft.fft2(fg, axes=(1, 3), norm='ortho').real
    return out.reshape(b, n, FOURIER_DIM).astype(f.dtype)


def spatial_gating(u, v, w_s, b_s, g_v):
    b, n, _ = u.shape
    v = layernorm(v, g_v)
    vc = v.reshape(b, n // CHUNK, CHUNK, SGU_GROUPS, SGU_GROUP_DIM)
    sv = jnp.einsum('gpq,bnqgc->bnpgc', w_s, vc) + b_s.T[None, None, :, :, None]
    return u * sv.reshape(b, n, SGU_DIM)


def diff_softmax_attend(q, k, v, lam):
    s = jnp.einsum('bhqd,bhkd->bhqk', q, k).astype(jnp.float32) * ATTN_SCALE
    p = jax.nn.softmax(s, axis=-1)
    b, h2, lq, lk = p.shape
    p = p.reshape(b, h2 // 2, 2, lq, lk)
    a = (p[:, :, 0] - lam * p[:, :, 1]).astype(v.dtype)
    return jnp.einsum('bhqk,bhkd->bhqd', a, v)


def diff_attention_latent(q, k, v, kc, vc, lam):
    k_all = jnp.concatenate([kc, k], axis=2)
    v_all = jnp.concatenate([vc, v], axis=2)
    b, h2, n, d = q.shape
    nb = n // Q_BLOCK
    qb = q.reshape(b, h2, nb, Q_BLOCK, d).transpose(2, 0, 1, 3, 4)
    ob = lax.map(lambda qq: diff_softmax_attend(qq, k_all, v_all, lam), qb)
    return ob.transpose(1, 2, 0, 3, 4).reshape(b, h2 // 2, n, 2 * d)


def head_out(o, g_sub, lam_init):
    o = rmsnorm(o, g_sub) * (1.0 - lam_init)
    b, h, n, d2 = o.shape
    return o.transpose(0, 2, 1, 3).reshape(b, n, h * d2)


def merge_branches(p, attn_flat, b_gate, w_f, w_s_out, w_a_out, w_o, sgu_w, sgu_b, sgu_g):
    b, n, _ = p.shape
    y_f = fourier_mix(p[..., OFF_F:OFF_U]) @ w_f
    u = jax.nn.gelu(p[..., OFF_U:OFF_V])
    v = jax.nn.gelu(p[..., OFF_V:OFF_Q])
    y_s = spatial_gating(u, v, sgu_w, sgu_b, sgu_g) @ w_s_out
    y_a = attn_flat @ w_a_out
    g = jax.nn.sigmoid(p[..., OFF_G:] + b_gate).reshape(b, n, N_BRANCH, D_MODEL)
    merged = g[:, :, 0] * y_f + g[:, :, 1] * y_s + g[:, :, 2] * y_a
    return merged @ w_o


def swiglu(h, w1, w3, w2):
    return (jax.nn.silu(h @ w1) * (h @ w3)) @ w2


def moe_swiglu(h, router_w, w1, w3, w2):
    logits = jnp.einsum('bld,de->ble', h, router_w).astype(jnp.float32)
    top_v, top_i = lax.top_k(logits, TOP_K)
    top_w = jax.nn.softmax(top_v, axis=-1)
    combine = jnp.sum(jax.nn.one_hot(top_i, N_EXPERTS, dtype=jnp.float32) * top_w[..., None], axis=-2)
    combine = combine.astype(h.dtype)
    out = jnp.zeros_like(h)
    for e in range(N_EXPERTS):
        out = out + combine[..., e:e + 1] * swiglu(h, w1[e], w3[e], w2[e])
    return out


def setup_inputs(seed: int = 0) -> dict:
    key = jax.random.key(seed)
    ks = jax.random.split(key, 26)
    D = D_MODEL

    def nrm(k, shape, scale):
        return jax.random.normal(k, shape, jnp.float32) * scale

    return {
        'x': nrm(ks[0], (BATCH, SEQ, D), 1.0),
        'c': nrm(ks[1], (BATCH, D), 1.0),
        'ctx': nrm(ks[2], (BATCH, CTX_LEN, D), 1.0),
        'c_ctx': nrm(ks[3], (D,), 1.0),
        'w_mod': nrm(ks[4], (DEPTH, D, 6 * D), 0.5 * D ** -0.5),
        'b_mod': nrm(ks[5], (DEPTH, 6 * D), 0.01),
        'g_norm': 1.0 + nrm(ks[6], (DEPTH, 4, D), 0.01),
        'w_in': nrm(ks[7], (DEPTH, D, N_IN), D ** -0.5),
        'b_gate': nrm(ks[8], (DEPTH, N_BRANCH * D), 0.01),
        'w_fourier_out': nrm(ks[9], (DEPTH, FOURIER_DIM, D), FOURIER_DIM ** -0.5),
        'w_sgu_out': nrm(ks[10], (DEPTH, SGU_DIM, D), SGU_DIM ** -0.5),
        'w_attn_out': nrm(ks[11], (DEPTH, DA_V_DIM, D), DA_V_DIM ** -0.5),
        'w_o': nrm(ks[12], (DEPTH, D, D), D ** -0.5),
        'sgu_w': nrm(ks[13], (DEPTH, SGU_GROUPS, CHUNK, CHUNK), CHUNK ** -0.5),
        'sgu_b': 1.0 + nrm(ks[14], (DEPTH, SGU_GROUPS, CHUNK), 0.01),
        'sgu_g': 1.0 + nrm(ks[15], (DEPTH, SGU_DIM), 0.01),
        'diff_lambda': nrm(ks[16], (DEPTH, 4, DA_HEAD_DIM), 0.1),
        'diff_subln_g': 1.0 + nrm(ks[17], (DEPTH, 2 * DA_HEAD_DIM), 0.01),
        'ffn_w1': nrm(ks[18], (N_DENSE, D, D_FF), D ** -0.5),
        'ffn_w3': nrm(ks[19], (N_DENSE, D, D_FF), D ** -0.5),
        'ffn_w2': nrm(ks[20], (N_DENSE, D_FF, D), D_FF ** -0.5),
        'router_w': nrm(ks[21], (N_MOE, D, N_EXPERTS), D ** -0.5),
        'moe_w1': nrm(ks[22], (N_MOE, N_EXPERTS, D, D_FF), D ** -0.5),
        'moe_w3': nrm(ks[23], (N_MOE, N_EXPERTS, D, D_FF), D ** -0.5),
        'moe_w2': nrm(ks[24], (N_MOE, N_EXPERTS, D_FF, D), D_FF ** -0.5),
    }


def reference(x, c, ctx, c_ctx, w_mod, b_mod, g_norm, w_in, b_gate, w_fourier_out, w_sgu_out,
              w_attn_out, w_o, sgu_w, sgu_b, sgu_g, diff_lambda, diff_subln_g,
              ffn_w1, ffn_w3, ffn_w2, router_w, moe_w1, moe_w3, moe_w2):
    b, n, _ = x.shape
    rows = n // GRID_W
    cos, sin = axial_rope(rows)
    silu_c = jax.nn.silu(c)
    silu_cc = jax.nn.silu(c_ctx)
    xl, xc = x, ctx
    for l in range(DEPTH):
        last = l == DEPTH - 1
        mod_l = (silu_c @ w_mod[l] + b_mod[l]).reshape(b, 6, 1, D_MODEL)
        mod_c = (silu_cc @ w_mod[l] + b_mod[l]).reshape(6, D_MODEL)
        lam_init = 0.8 - 0.6 * math.exp(-0.3 * l)
        dl = diff_lambda[l].astype(jnp.float32)
        lam = jnp.exp(jnp.sum(dl[0] * dl[1])) - jnp.exp(jnp.sum(dl[2] * dl[3])) + lam_init
        w_in_l = w_in[l]
        branch_w = (b_gate[l], w_fourier_out[l], w_sgu_out[l], w_attn_out[l], w_o[l], sgu_w[l], sgu_b[l], sgu_g[l])

        hl = modulate(rmsnorm(xl, g_norm[l, 0]), mod_l[:, 0], mod_l[:, 1])
        hc = modulate(rmsnorm(xc, g_norm[l, 0]), mod_c[0], mod_c[1])
        pl = hl @ w_in_l
        ql = apply_rope(split_q(pl[..., OFF_Q:OFF_K]), cos, sin)
        kl, vl = split_kv(pl[..., OFF_K:OFF_G])
        kl = apply_rope(kl, cos, sin)
        if last:
            kc, vc = split_kv(hc @ w_in_l[:, OFF_K:OFF_G])
        else:
            pc = hc @ w_in_l
            qc = split_q(pc[..., OFF_Q:OFF_K])
            kc, vc = split_kv(pc[..., OFF_K:OFF_G])
        al = head_out(diff_attention_latent(ql, kl, vl, kc, vc, lam), diff_subln_g[l], lam_init)
        yl = merge_branches(pl, al, *branch_w)
        xl = xl + mod_l[:, 2] * rmsnorm(yl, g_norm[l, 1])
        if not last:
            ac = head_out(diff_softmax_attend(qc, kc, vc, lam), diff_subln_g[l], lam_init)
            yc = merge_branches(pc, ac, *branch_w)
            xc = xc + mod_c[2] * rmsnorm(yc, g_norm[l, 1])

        i = l // 2
        if l % 2 == 0:
            ffn = lambda h, i=i: swiglu(h, ffn_w1[i], ffn_w3[i], ffn_w2[i])
        else:
            ffn = lambda h, i=i: moe_swiglu(h, router_w[i], moe_w1[i], moe_w3[i], moe_w2[i])
        fl = modulate(rmsnorm(xl, g_norm[l, 2]), mod_l[:, 3], mod_l[:, 4])
        xl = xl + mod_l[:, 5] * rmsnorm(ffn(fl), g_norm[l, 3])
        if not last:
            fc = modulate(rmsnorm(xc, g_norm[l, 2]), mod_c[3], mod_c[4])
            xc = xc + mod_c[5] * rmsnorm(ffn(fc), g_norm[l, 3])
    return xl
```

```python
import functools
import math
from typing import NamedTuple

import jax
import jax.numpy as jnp
from jax import lax
from jax.experimental import pallas as pl
from jax.experimental.pallas import tpu as pltpu

F32 = jnp.float32
BF16 = jnp.bfloat16

EPS = 1e-6
GRID_W = 64
ROPE_BASE = 10000.0
HEAD_DIM = 64
HEADS = 8
PAIR = 2 * HEAD_DIM
QK_DIM = HEADS * PAIR
GROUP_DIM = 128
GROUPS = 4
BRANCH_DIM = GROUPS * GROUP_DIM
CHUNK = 128
FFT_N1 = 128
N_EXPERTS = 8
TOP_K = 2
EXPERT_ROWS = 8
ATTN_SCALE = HEAD_DIM ** -0.5
VMEM_LIMIT = 56 * 1024 * 1024


class Cfg(NamedTuple):
    d: int
    batch: int
    seq: int
    ctx: int
    dff: int
    depth: int
    tm_norm: int
    tm_mm: int
    tn_mm: int
    tm_rope: int
    tq: int
    tk: int
    tm_sgu: int
    tm_merge: int
    tm_up: int
    tn_up: int
    tm_down: int
    tk_down: int
    tn_mod: int

    @property
    def n_in(self):
        return 3 * self.d + 3 * QK_DIM + 3 * BRANCH_DIM

    @property
    def off_q(self):
        return 3 * self.d

    @property
    def off_k(self):
        return self.off_q + QK_DIM

    @property
    def off_va(self):
        return self.off_k + QK_DIM

    @property
    def off_f(self):
        return self.off_va + QK_DIM

    @property
    def off_u(self):
        return self.off_f + BRANCH_DIM

    @property
    def off_v(self):
        return self.off_u + BRANCH_DIM


PROD = Cfg(d=2048, batch=4, seq=8192, ctx=256, dff=5632, depth=4,
           tm_norm=512, tm_mm=1024, tn_mm=1536, tm_rope=512, tq=256, tk=1024,
           tm_sgu=1024, tm_merge=256, tm_up=1024, tn_up=512, tm_down=512, tk_down=1408,
           tn_mod=1024)


def _params(*sem):
    return pltpu.CompilerParams(dimension_semantics=sem, vmem_limit_bytes=VMEM_LIMIT)


def _rms(y, g):
    return y * lax.rsqrt(jnp.mean(y * y, axis=-1, keepdims=True) + EPS) * g


def _mod_kernel(c_ref, w_ref, b_ref, o_ref):
    sc = jax.nn.silu(c_ref[...])
    o_ref[0] = jnp.dot(sc.astype(BF16), w_ref[0].astype(BF16), preferred_element_type=F32) + b_ref[0]


def _modulation(cfg, c_rows, w_mod, b_mod):
    depth, d, n6 = w_mod.shape
    tn = cfg.tn_mod
    return pl.pallas_call(
        _mod_kernel,
        grid=(depth, n6 // tn),
        in_specs=[pl.BlockSpec((8, d), lambda l, j: (0, 0)),
                  pl.BlockSpec((1, d, tn), lambda l, j: (l, 0, j)),
                  pl.BlockSpec((1, 1, tn), lambda l, j: (l, 0, j))],
        out_specs=pl.BlockSpec((1, 8, tn), lambda l, j: (l, 0, j)),
        out_shape=jax.ShapeDtypeStruct((depth, 8, n6), F32),
        compiler_params=_params("parallel", "parallel"),
        name="modulation",
    )(c_rows, w_mod, b_mod.reshape(depth, 1, n6))


def _norm_mod_kernel(x_ref, g_ref, mod_ref, o_ref, *, shift_idx, scale_idx):
    y = _rms(x_ref[...], g_ref[...])
    m = mod_ref[0]
    o_ref[...] = (y * (1.0 + m[scale_idx:scale_idx + 1]) + m[shift_idx:shift_idx + 1]).astype(o_ref.dtype)


def _norm_mod_router_kernel(x_ref, g_ref, mod_ref, rw_ref, o_ref, idx_ref, wt_ref, *, shift_idx, scale_idx):
    y = _rms(x_ref[...], g_ref[...])
    m = mod_ref[0]
    h = y * (1.0 + m[scale_idx:scale_idx + 1]) + m[shift_idx:shift_idx + 1]
    o_ref[...] = h.astype(o_ref.dtype)
    logits = lax.dot_general(rw_ref[...], h, (((1,), (1,)), ((), ())),
                             precision=lax.Precision.HIGHEST, preferred_element_type=F32)
    row = lax.broadcasted_iota(jnp.int32, logits.shape, 0).astype(F32)
    m1 = jnp.max(logits, axis=0, keepdims=True)
    i1 = jnp.min(jnp.where(logits == m1, row, float(N_EXPERTS)), axis=0, keepdims=True)
    rest = jnp.where(row == i1, -jnp.inf, logits)
    m2 = jnp.max(rest, axis=0, keepdims=True)
    i2 = jnp.min(jnp.where(rest == m2, row, float(N_EXPERTS)), axis=0, keepdims=True)
    e2 = jnp.exp(m2 - m1)
    w1 = 1.0 / (1.0 + e2)
    w2 = e2 / (1.0 + e2)
    idx_ref[...] = jnp.where(row == 0.0, i1, jnp.where(row == 1.0, i2, 0.0)).astype(jnp.int32)
    wt_ref[...] = jnp.where(row == 0.0, w1, jnp.where(row == 1.0, w2, 0.0))


def _norm_mod(cfg, x, g, mod, rows_per_mod, shift_idx, scale_idx, router_wt=None):
    t, d = x.shape
    tm = min(cfg.tm_norm, t)
    in_specs = [pl.BlockSpec((tm, d), lambda i: (i, 0)),
                pl.BlockSpec((1, d), lambda i: (0, 0)),
                pl.BlockSpec((1, 6, d), lambda i: (i * tm // rows_per_mod, 0, 0))]
    h_spec = pl.BlockSpec((tm, d), lambda i: (i, 0))
    h_shape = jax.ShapeDtypeStruct((t, d), BF16)
    if router_wt is None:
        return pl.pallas_call(
            functools.partial(_norm_mod_kernel, shift_idx=shift_idx, scale_idx=scale_idx),
            grid=(t // tm,), in_specs=in_specs, out_specs=h_spec, out_shape=h_shape,
            compiler_params=_params("parallel"), name="norm_mod",
        )(x, g, mod)
    r_spec = pl.BlockSpec((EXPERT_ROWS, tm), lambda i: (0, i))
    return pl.pallas_call(
        functools.partial(_norm_mod_router_kernel, shift_idx=shift_idx, scale_idx=scale_idx),
        grid=(t // tm,),
        in_specs=in_specs + [pl.BlockSpec((N_EXPERTS, d), lambda i: (0, 0))],
        out_specs=[h_spec, r_spec, r_spec],
        out_shape=[h_shape, jax.ShapeDtypeStruct((EXPERT_ROWS, t), jnp.int32),
                   jax.ShapeDtypeStruct((EXPERT_ROWS, t), F32)],
        compiler_params=_params("parallel"), name="norm_mod_router",
    )(x, g, mod, router_wt)


def _matmul_kernel(a_ref, w_ref, o_ref):
    o_ref[...] = jnp.dot(a_ref[...], w_ref[...], preferred_element_type=F32).astype(o_ref.dtype)


def _matmul(cfg, a, w):
    t, k = a.shape
    n = w.shape[1]
    tm = min(cfg.tm_mm, t)
    tn = cfg.tn_mm if n % cfg.tn_mm == 0 else 512
    return pl.pallas_call(
        _matmul_kernel,
        grid=(t // tm, n // tn),
        in_specs=[pl.BlockSpec((tm, k), lambda i, j: (i, 0)),
                  pl.BlockSpec((k, tn), lambda i, j: (0, j))],
        out_specs=pl.BlockSpec((tm, tn), lambda i, j: (i, j)),
        out_shape=jax.ShapeDtypeStruct((t, n), BF16),
        compiler_params=_params("parallel", "parallel"), name="in_proj",
    )(a, w)


def _rope_tables(seq):
    rows = seq // GRID_W
    r = jnp.repeat(jnp.arange(rows, dtype=F32), GRID_W)
    col = jnp.tile(jnp.arange(GRID_W, dtype=F32), rows)
    n_freq = HEAD_DIM // 4
    inv = ROPE_BASE ** (-jnp.arange(n_freq, dtype=F32) / n_freq)
    ang = jnp.concatenate([r[:, None] * inv, col[:, None] * inv], axis=-1)
    cos, sin = jnp.cos(ang), jnp.sin(ang)
    return jnp.tile(cos, (1, 4)), jnp.tile(jnp.concatenate([-sin, sin], axis=-1), (1, 2))


def _rope_kernel(p_ref, cos_ref, sin_ref, o_ref):
    cos = cos_ref[...]
    sin = sin_ref[...]
    lane = lax.broadcasted_iota(jnp.int32, cos.shape, 1)
    first_half = (lane % HEAD_DIM) < (HEAD_DIM // 2)
    for c in range(QK_DIM // PAIR):
        t = p_ref[0, :, c * PAIR:(c + 1) * PAIR].astype(F32)
        partner = jnp.where(first_half, pltpu.roll(t, PAIR - HEAD_DIM // 2, 1), pltpu.roll(t, HEAD_DIM // 2, 1))
        o_ref[0, 0, :, c * PAIR:(c + 1) * PAIR] = (t * cos + partner * sin).astype(o_ref.dtype)


def _rope(cfg, p3, cos, sin):
    b, s, _ = p3.shape
    tm = cfg.tm_rope
    qblk = cfg.off_q // QK_DIM
    return pl.pallas_call(
        _rope_kernel,
        grid=(b, s // tm, 2),
        in_specs=[pl.BlockSpec((1, tm, QK_DIM), lambda bi, i, j: (bi, i, qblk + j)),
                  pl.BlockSpec((tm, PAIR), lambda bi, i, j: (i, 0)),
                  pl.BlockSpec((tm, PAIR), lambda bi, i, j: (i, 0))],
        out_specs=pl.BlockSpec((1, 1, tm, QK_DIM), lambda bi, i, j: (j, bi, i, 0)),
        out_shape=jax.ShapeDtypeStruct((2, b, s, QK_DIM), BF16),
        compiler_params=_params("parallel", "parallel", "parallel"), name="rope",
    )(p3, cos, sin)


def _attn_kernel(*refs, tq, tk, n_lat, lam_init):
    if n_lat:
        q_ref, kc_ref, vc_ref, k_ref, v_ref, dl_ref, g_ref, o_ref, q2_ref, m_ref, l_ref, acc_ref = refs
    else:
        q_ref, kc_ref, vc_ref, dl_ref, g_ref, o_ref, q2_ref, m_ref, l_ref, acc_ref = refs
    q = q_ref[0] * ATTN_SCALE
    lane = lax.broadcasted_iota(jnp.int32, q.shape, 1)
    zero = jnp.zeros_like(q)
    q2_ref[:tq] = jnp.where(lane < HEAD_DIM, q, zero)
    q2_ref[tq:] = jnp.where(lane >= HEAD_DIM, q, zero)
    m_ref[...] = jnp.full(m_ref.shape, -jnp.inf, F32)
    l_ref[...] = jnp.zeros(l_ref.shape, F32)
    acc_ref[...] = jnp.zeros(acc_ref.shape, F32)

    def update(k, v):
        s = lax.dot_general(q2_ref[...], k, (((1,), (1,)), ((), ())), preferred_element_type=F32)
        m_old = m_ref[...]
        m_new = jnp.maximum(m_old, jnp.max(s, axis=-1, keepdims=True))
        alpha = jnp.exp(m_old - m_new)
        p = jnp.exp(s - m_new)
        l_ref[...] = alpha * l_ref[...] + jnp.sum(p, axis=-1, keepdims=True)
        acc_ref[...] = alpha * acc_ref[...] + jnp.dot(p.astype(BF16), v, preferred_element_type=F32)
        m_ref[...] = m_new

    update(kc_ref[0], vc_ref[0])
    if n_lat:
        def body(i, carry):
            off = pl.multiple_of(i * tk, tk)
            update(k_ref[0, 0, pl.ds(off, tk), :], v_ref[0, pl.ds(off, tk), :])
            return carry
        lax.fori_loop(0, n_lat, body, 0)

    o12 = acc_ref[...] / l_ref[...]
    dl = dl_ref[...]
    lam = (jnp.exp(jnp.sum(dl[0:1] * dl[1:2], axis=-1, keepdims=True))
           - jnp.exp(jnp.sum(dl[2:3] * dl[3:4], axis=-1, keepdims=True)) + lam_init)
    o = o12[:tq] - lam * o12[tq:]
    o_ref[0] = (_rms(o, g_ref[...]) * (1.0 - lam_init)).astype(o_ref.dtype)


def _attention(cfg, q_arr, q_blk, kc_arr, kc_blk, vc_arr, vc_blk, dl, g_sub, lam_init, tq,
               k_lat=None, v_arr=None, v_blk=0):
    b, lq = q_arr.shape[0], q_arr.shape[1]
    lc = kc_arr.shape[1]
    in_specs = [pl.BlockSpec((1, tq, PAIR), lambda bi, h, i: (bi, i, q_blk + h)),
                pl.BlockSpec((1, lc, PAIR), lambda bi, h, i: (bi, 0, kc_blk + h)),
                pl.BlockSpec((1, lc, PAIR), lambda bi, h, i: (bi, 0, vc_blk + h))]
    args = [q_arr, kc_arr, vc_arr]
    n_lat = 0
    if k_lat is not None:
        s = k_lat.shape[2]
        n_lat = s // cfg.tk
        in_specs += [pl.BlockSpec((1, 1, s, PAIR), lambda bi, h, i: (1, bi, 0, h)),
                     pl.BlockSpec((1, s, PAIR), lambda bi, h, i: (bi, 0, v_blk + h))]
        args += [k_lat, v_arr]
    in_specs += [pl.BlockSpec((4, HEAD_DIM), lambda bi, h, i: (0, 0)),
                 pl.BlockSpec((1, PAIR), lambda bi, h, i: (0, 0))]
    args += [dl, g_sub]
    return pl.pallas_call(
        functools.partial(_attn_kernel, tq=tq, tk=cfg.tk, n_lat=n_lat, lam_init=lam_init),
        grid=(b, HEADS, lq // tq),
        in_specs=in_specs,
        out_specs=pl.BlockSpec((1, tq, PAIR), lambda bi, h, i: (bi, i, h)),
        out_shape=jax.ShapeDtypeStruct((b, lq, QK_DIM), BF16),
        scratch_shapes=[pltpu.VMEM((2 * tq, PAIR), BF16), pltpu.VMEM((2 * tq, 1), F32),
                        pltpu.VMEM((2 * tq, 1), F32), pltpu.VMEM((2 * tq, PAIR), F32)],
        compiler_params=_params("parallel", "parallel", "parallel"),
        name="diff_attention" if n_lat else "diff_attention_ctx",
    )(*args)


def _dft_cos_sin(n, scale=1.0):
    j = jnp.arange(n, dtype=jnp.int32)
    ang = ((j[:, None] * j[None, :]) % n).astype(F32) * (2.0 * math.pi / n)
    return jnp.cos(ang) * scale, jnp.sin(ang) * scale


def _fft_tables(seq):
    n1, n2 = FFT_N1, seq // FFT_N1
    c1, s1 = _dft_cos_sin(n1)
    stage1 = jnp.concatenate([c1, -s1], axis=0).astype(BF16)
    k1 = jnp.arange(n1, dtype=jnp.int32)
    m = jnp.arange(n2, dtype=jnp.int32)
    ang = (m[:, None] * k1[None, :]).astype(F32) * (2.0 * math.pi / seq)
    tw_cos, tw_sin = jnp.cos(ang)[:, :, None], jnp.sin(ang)[:, :, None]
    c2, s2 = _dft_cos_sin(n2)
    stage2 = jnp.concatenate([jnp.concatenate([c2, s2], axis=1),
                              jnp.concatenate([-s2, c2], axis=1)], axis=0).astype(BF16)
    cc, sc = _dft_cos_sin(GROUP_DIM, scale=(seq * GROUP_DIM) ** -0.5)
    return stage1, tw_cos, tw_sin, stage2, cc.astype(BF16), sc.astype(BF16)


def _fft_a_kernel(x_ref, w_ref, c_ref, s_ref, o_ref):
    a = jnp.dot(w_ref[...], x_ref[0], preferred_element_type=F32)
    re, im = a[:FFT_N1], a[FFT_N1:]
    c, s = c_ref[0], s_ref[0]
    o_ref[0, :, :BRANCH_DIM] = (re * c + im * s).astype(o_ref.dtype)
    o_ref[0, :, BRANCH_DIM:] = (im * c - re * s).astype(o_ref.dtype)


def _fft_b_kernel(x_ref, w_ref, cc_ref, sc_ref, o_ref, *, n2):
    x = x_ref[0, 0]
    z = jnp.concatenate([x[:, :BRANCH_DIM], x[:, BRANCH_DIM:]], axis=0)
    y = jnp.dot(w_ref[...], z, preferred_element_type=F32).astype(BF16)
    for g in range(GROUPS):
        sl = slice(g * GROUP_DIM, (g + 1) * GROUP_DIM)
        o_ref[0, :, sl] = (jnp.dot(y[:n2, sl], cc_ref[...], preferred_element_type=F32)
                           + jnp.dot(y[n2:, sl], sc_ref[...], preferred_element_type=F32)).astype(o_ref.dtype)


def _fourier_latent(cfg, p_lat, tables):
    b, s = cfg.batch, cfg.seq
    n1, n2 = FFT_N1, s // FFT_N1
    stage1, tw_cos, tw_sin, stage2, cc, sc = tables
    blocks_per_row = cfg.n_in // BRANCH_DIM
    fblk = cfg.off_f // BRANCH_DIM
    mid = pl.pallas_call(
        _fft_a_kernel,
        grid=(b, n2),
        in_specs=[pl.BlockSpec((1, n1, BRANCH_DIM), lambda bi, m: (bi, 0, m * blocks_per_row + fblk)),
                  pl.BlockSpec((2 * n1, n1), lambda bi, m: (0, 0)),
                  pl.BlockSpec((1, n1, 1), lambda bi, m: (m, 0, 0)),
                  pl.BlockSpec((1, n1, 1), lambda bi, m: (m, 0, 0))],
        out_specs=pl.BlockSpec((1, n1, 2 * BRANCH_DIM), lambda bi, m: (bi, 0, m)),
        out_shape=jax.ShapeDtypeStruct((b, n1, n2 * 2 * BRANCH_DIM), BF16),
        compiler_params=_params("parallel", "parallel"), name="fft_stage_a",
    )(p_lat.reshape(b, n1, n2 * cfg.n_in), stage1, tw_cos, tw_sin)
    out = pl.pallas_call(
        functools.partial(_fft_b_kernel, n2=n2),
        grid=(b, n1),
        in_specs=[pl.BlockSpec((1, 1, n2, 2 * BRANCH_DIM), lambda bi, k: (bi, k, 0, 0)),
                  pl.BlockSpec((2 * n2, 2 * n2), lambda bi, k: (0, 0)),
                  pl.BlockSpec((GROUP_DIM, GROUP_DIM), lambda bi, k: (0, 0)),
                  pl.BlockSpec((GROUP_DIM, GROUP_DIM), lambda bi, k: (0, 0))],
        out_specs=pl.BlockSpec((1, n2, BRANCH_DIM), lambda bi, k: (bi, 0, k)),
        out_shape=jax.ShapeDtypeStruct((b, n2, n1 * BRANCH_DIM), BF16),
        compiler_params=_params("parallel", "parallel"), name="fft_stage_b",
    )(mid.reshape(b, n1, n2, 2 * BRANCH_DIM), stage2, cc, sc)
    return out.reshape(b * s, BRANCH_DIM)


def _dft_dense_kernel(x_ref, c_ref, s_ref, cc_ref, sc_ref, o_ref):
    x = x_ref[0]
    yr = jnp.dot(c_ref[...], x, preferred_element_type=F32).astype(BF16)
    yi = (-jnp.dot(s_ref[...], x, preferred_element_type=F32)).astype(BF16)
    for g in range(GROUPS):
        sl = slice(g * GROUP_DIM, (g + 1) * GROUP_DIM)
        o_ref[0, :, sl] = (jnp.dot(yr[:, sl], cc_ref[...], preferred_element_type=F32)
                           + jnp.dot(yi[:, sl], sc_ref[...], preferred_element_type=F32)).astype(o_ref.dtype)


def _fourier_ctx(cfg, p_ctx):
    b, n = cfg.batch, cfg.ctx
    cn, sn = _dft_cos_sin(n)
    cc, sc = _dft_cos_sin(GROUP_DIM, scale=(n * GROUP_DIM) ** -0.5)
    fblk = cfg.off_f // BRANCH_DIM
    full = lambda bi: (0, 0)
    out = pl.pallas_call(
        _dft_dense_kernel,
        grid=(b,),
        in_specs=[pl.BlockSpec((1, n, BRANCH_DIM), lambda bi: (bi, 0, fblk)),
                  pl.BlockSpec((n, n), full), pl.BlockSpec((n, n), full),
                  pl.BlockSpec((GROUP_DIM, GROUP_DIM), full), pl.BlockSpec((GROUP_DIM, GROUP_DIM), full)],
        out_specs=pl.BlockSpec((1, n, BRANCH_DIM), lambda bi: (bi, 0, 0)),
        out_shape=jax.ShapeDtypeStruct((b, n, BRANCH_DIM), BF16),
        compiler_params=_params("parallel"), name="dft_ctx",
    )(p_ctx.reshape(b, n, cfg.n_in), cn.astype(BF16), sn.astype(BF16), cc.astype(BF16), sc.astype(BF16))
    return out.reshape(b * n, BRANCH_DIM)


def _sgu_kernel(u_ref, v_ref, g_ref, w_ref, b_ref, o_ref, *, n_chunks):
    gv = g_ref[...]
    for c in range(n_chunks):
        rows = slice(c * CHUNK, (c + 1) * CHUNK)
        u = jax.nn.gelu(u_ref[rows, :].astype(F32))
        v = jax.nn.gelu(v_ref[rows, :].astype(F32))
        vc = v - jnp.mean(v, axis=-1, keepdims=True)
        vn = (vc * lax.rsqrt(jnp.mean(vc * vc, axis=-1, keepdims=True) + EPS) * gv).astype(BF16)
        for g in range(GROUPS):
            sl = slice(g * GROUP_DIM, (g + 1) * GROUP_DIM)
            sv = jnp.dot(w_ref[g], vn[:, sl], preferred_element_type=F32) + b_ref[:, g:g + 1]
            o_ref[rows, sl] = (u[:, sl] * sv).astype(o_ref.dtype)


def _sgu(cfg, p2, sgu_w, sgu_bt, sgu_g):
    t = p2.shape[0]
    tm = min(cfg.tm_sgu, t)
    ublk, vblk = cfg.off_u // BRANCH_DIM, cfg.off_v // BRANCH_DIM
    return pl.pallas_call(
        functools.partial(_sgu_kernel, n_chunks=tm // CHUNK),
        grid=(t // tm,),
        in_specs=[pl.BlockSpec((tm, BRANCH_DIM), lambda i: (i, ublk)),
                  pl.BlockSpec((tm, BRANCH_DIM), lambda i: (i, vblk)),
                  pl.BlockSpec((1, BRANCH_DIM), lambda i: (0, 0)),
                  pl.BlockSpec((GROUPS, CHUNK, CHUNK), lambda i: (0, 0, 0)),
                  pl.BlockSpec((CHUNK, GROUPS), lambda i: (0, 0))],
        out_specs=pl.BlockSpec((tm, BRANCH_DIM), lambda i: (i, 0)),
        out_shape=jax.ShapeDtypeStruct((t, BRANCH_DIM), BF16),
        compiler_params=_params("parallel"), name="spatial_gating",
    )(p2, p2, sgu_g, sgu_w, sgu_bt)


def _merge_kernel(pg0_ref, pg1_ref, pg2_ref, f_ref, s_ref, a_ref, x_ref, bg_ref,
                  wf_ref, ws_ref, wa_ref, wo_ref, g_ref, mod_ref, o_ref, *, d):
    def gate(pg_ref, k):
        return jax.nn.sigmoid(pg_ref[...].astype(F32) + bg_ref[:, k * d:(k + 1) * d])

    merged = gate(pg0_ref, 0) * jnp.dot(f_ref[...], wf_ref[...], preferred_element_type=F32)
    merged += gate(pg1_ref, 1) * jnp.dot(s_ref[...], ws_ref[...], preferred_element_type=F32)
    merged += gate(pg2_ref, 2) * jnp.dot(a_ref[...], wa_ref[...], preferred_element_type=F32)
    y = jnp.dot(merged.astype(BF16), wo_ref[...], preferred_element_type=F32)
    o_ref[...] = x_ref[...] + mod_ref[0][2:3] * _rms(y, g_ref[...])


def _resident(shape):
    return pl.BlockSpec(shape, lambda *_: (0,) * len(shape), pipeline_mode=pl.Buffered(1))


def _merge(cfg, p2, four, sgu, attn, x, b_gate, wf, ws, wa, wo, g, mod, rows_per_mod):
    t, d = x.shape
    tm = cfg.tm_merge
    row = lambda i: (i, 0)
    return pl.pallas_call(
        functools.partial(_merge_kernel, d=d),
        grid=(t // tm,),
        in_specs=[pl.BlockSpec((tm, d), lambda i: (i, 0)),
                  pl.BlockSpec((tm, d), lambda i: (i, 1)),
                  pl.BlockSpec((tm, d), lambda i: (i, 2)),
                  pl.BlockSpec((tm, BRANCH_DIM), row), pl.BlockSpec((tm, BRANCH_DIM), row),
                  pl.BlockSpec((tm, QK_DIM), row), pl.BlockSpec((tm, d), row),
                  _resident((1, 3 * d)), _resident((BRANCH_DIM, d)), _resident((BRANCH_DIM, d)),
                  _resident((QK_DIM, d)), _resident((d, d)), _resident((1, d)),
                  pl.BlockSpec((1, 6, d), lambda i: (i * tm // rows_per_mod, 0, 0))],
        out_specs=pl.BlockSpec((tm, d), row),
        out_shape=jax.ShapeDtypeStruct((t, d), F32),
        compiler_params=_params("parallel"), name="merge",
    )(p2, p2, p2, four, sgu, attn, x, b_gate, wf, ws, wa, wo, g, mod)


def _ffn_up_kernel(eid_ref, nact_ref, a_ref, w1_ref, w3_ref, o_ref):
    active = pl.program_id(0) < nact_ref[0]

    @pl.when(active)
    def _():
        a = a_ref[...]
        h1 = jnp.dot(a, w1_ref[0], preferred_element_type=F32)
        h3 = jnp.dot(a, w3_ref[0], preferred_element_type=F32)
        o_ref[...] = (jax.nn.silu(h1) * h3).astype(o_ref.dtype)

    @pl.when(jnp.logical_not(active))
    def _():
        o_ref[...] = jnp.zeros(o_ref.shape, o_ref.dtype)


def _ffn_up(cfg, a, w1, w3, tile_eid, n_active, tm):
    r, d = a.shape
    dff = w1.shape[2]
    tn = cfg.tn_up
    grid_spec = pltpu.PrefetchScalarGridSpec(
        num_scalar_prefetch=2,
        grid=(r // tm, dff // tn),
        in_specs=[pl.BlockSpec((tm, d), lambda i, j, eid, na: (jnp.minimum(i, na[0] - 1), 0)),
                  pl.BlockSpec((1, d, tn), lambda i, j, eid, na: (eid[i], 0, j)),
                  pl.BlockSpec((1, d, tn), lambda i, j, eid, na: (eid[i], 0, j))],
        out_specs=pl.BlockSpec((tm, tn), lambda i, j, eid, na: (i, j)),
    )
    return pl.pallas_call(
        _ffn_up_kernel, grid_spec=grid_spec,
        out_shape=jax.ShapeDtypeStruct((r, dff), BF16),
        compiler_params=_params("parallel", "parallel"), name="swiglu_up",
    )(tile_eid, n_active, a, w1, w3)


def _ffn_down_kernel(eid_ref, nact_ref, *refs, residual):
    if residual:
        h_ref, w_ref, x_ref, g_ref, mod_ref, o_ref, acc_ref = refs
    else:
        h_ref, w_ref, o_ref, acc_ref = refs
    k = pl.program_id(1)

    @pl.when(k == 0)
    def _():
        acc_ref[...] = jnp.zeros(acc_ref.shape, F32)

    @pl.when(pl.program_id(0) < nact_ref[0])
    def _():
        acc_ref[...] += jnp.dot(h_ref[...], w_ref[0], preferred_element_type=F32)

    @pl.when(k == pl.num_programs(1) - 1)
    def _():
        if residual:
            o_ref[...] = x_ref[...] + mod_ref[0][5:6] * _rms(acc_ref[...], g_ref[...])
        else:
            o_ref[...] = acc_ref[...].astype(o_ref.dtype)


def _ffn_down(cfg, h, w2, tile_eid, n_active, tm, x=None, g=None, mod=None, rows_per_mod=None):
    r, dff = h.shape
    d = w2.shape[2]
    tk = cfg.tk_down
    residual = x is not None
    in_specs = [pl.BlockSpec((tm, tk), lambda i, k, eid, na: (jnp.minimum(i, na[0] - 1), k)),
                pl.BlockSpec((1, tk, d), lambda i, k, eid, na: (eid[i], k, 0))]
    args = [h, w2]
    if residual:
        in_specs += [pl.BlockSpec((tm, d), lambda i, k, eid, na: (i, 0)),
                     pl.BlockSpec((1, d), lambda i, k, eid, na: (0, 0)),
                     pl.BlockSpec((1, 6, d), lambda i, k, eid, na: (i * tm // rows_per_mod, 0, 0))]
        args += [x, g, mod]
    grid_spec = pltpu.PrefetchScalarGridSpec(
        num_scalar_prefetch=2,
        grid=(r // tm, dff // tk),
        in_specs=in_specs,
        out_specs=pl.BlockSpec((tm, d), lambda i, k, eid, na: (i, 0)),
        scratch_shapes=[pltpu.VMEM((tm, d), F32)],
    )
    return pl.pallas_call(
        functools.partial(_ffn_down_kernel, residual=residual), grid_spec=grid_spec,
        out_shape=jax.ShapeDtypeStruct((r, d), F32 if residual else BF16),
        compiler_params=_params("parallel", "arbitrary"),
        name="swiglu_down_residual" if residual else "swiglu_down",
    )(tile_eid, n_active, *args)


def _combine_kernel(y1_ref, y2_ref, wt_ref, x_ref, g_ref, mod_ref, o_ref):
    wt = wt_ref[...]
    y = wt[:, 0:1] * y1_ref[...].astype(F32) + wt[:, 1:2] * y2_ref[...].astype(F32)
    o_ref[...] = x_ref[...] + mod_ref[0][5:6] * _rms(y, g_ref[...])


def _combine(cfg, y1, y2, wt, x, g, mod, rows_per_mod):
    t, d = x.shape
    tm = min(cfg.tm_norm, t)
    row = lambda i: (i, 0)
    return pl.pallas_call(
        _combine_kernel,
        grid=(t // tm,),
        in_specs=[pl.BlockSpec((tm, d), row), pl.BlockSpec((tm, d), row),
                  pl.BlockSpec((tm, TOP_K), row), pl.BlockSpec((tm, d), row),
                  pl.BlockSpec((1, d), lambda i: (0, 0)),
                  pl.BlockSpec((1, 6, d), lambda i: (i * tm // rows_per_mod, 0, 0))],
        out_specs=pl.BlockSpec((tm, d), row),
        out_shape=jax.ShapeDtypeStruct((t, d), F32),
        compiler_params=_params("parallel"), name="moe_combine",
    )(y1, y2, wt, x, g, mod)


def _dispatch_plan(top_i, tm):
    t = top_i.shape[1]
    n_pairs = TOP_K * t
    n_rows = n_pairs + N_EXPERTS * tm
    e_flat = top_i.reshape(n_pairs)
    onehot = (e_flat[:, None] == jnp.arange(N_EXPERTS, dtype=jnp.int32)[None, :]).astype(jnp.int32)
    rank = jnp.sum(jnp.cumsum(onehot, axis=0) * onehot, axis=1) - 1
    counts = jnp.sum(onehot, axis=0)
    padded = (counts + tm - 1) // tm * tm
    ends = jnp.cumsum(padded)
    starts = ends - padded
    dest = starts[e_flat] + rank
    tok = jnp.tile(jnp.arange(t, dtype=jnp.int32), TOP_K)
    row_tok = jnp.zeros((n_rows,), jnp.int32).at[dest].set(tok)
    tile_start = jnp.arange(n_rows // tm, dtype=jnp.int32) * tm
    tile_eid = jnp.minimum(jnp.sum((tile_start[:, None] >= ends[None, :]).astype(jnp.int32), axis=1),
                           N_EXPERTS - 1)
    n_active = (ends[-1] // tm).astype(jnp.int32).reshape(1)
    return row_tok, dest.reshape(TOP_K, t), tile_eid, n_active


def _dense_plan(t, tm):
    return jnp.zeros((t // tm,), jnp.int32), jnp.full((1,), t // tm, jnp.int32)


def _permute_w_in(w):
    b3 = 3 * BRANCH_DIM
    q3 = 3 * QK_DIM
    return jnp.concatenate([w[:, b3 + q3:], w[:, b3:b3 + q3], w[:, :b3]], axis=1).astype(BF16)


def _forward(cfg, x, c, ctx, c_ctx, w_mod, b_mod, g_norm, w_in, b_gate, w_fourier_out, w_sgu_out,
             w_attn_out, w_o, sgu_w, sgu_b, sgu_g, diff_lambda, diff_subln_g,
             ffn_w1, ffn_w3, ffn_w2, router_w, moe_w1, moe_w3, moe_w2):
    b, s, d = x.shape
    n_ctx = ctx.shape[1]
    t_lat, t_ctx = b * s, b * n_ctx
    cos, sin = _rope_tables(s)
    fft_tables = _fft_tables(s)

    c_rows = jnp.concatenate([c, c_ctx[None, :], jnp.zeros((8 - b - 1, d), F32)], axis=0)
    mod_all = _modulation(cfg, c_rows, w_mod, b_mod)

    xl = x.reshape(t_lat, d)
    xc = ctx.reshape(t_ctx, d)
    for l in range(cfg.depth):
        last = l == cfg.depth - 1
        lam_init = 0.8 - 0.6 * math.exp(-0.3 * l)
        mod_l = mod_all[l, :b].reshape(b, 6, d)
        mod_c = mod_all[l, b:b + 1].reshape(1, 6, d)
        g = g_norm[l].reshape(4, 1, d)
        w_in_l = _permute_w_in(w_in[l])
        wf, ws = w_fourier_out[l].astype(BF16), w_sgu_out[l].astype(BF16)
        wa, wo = w_attn_out[l].astype(BF16), w_o[l].astype(BF16)
        bg = b_gate[l].reshape(1, 3 * d)
        sw = sgu_w[l].astype(BF16)
        sbt = sgu_b[l].T
        sg = sgu_g[l].reshape(1, BRANCH_DIM)
        dl = diff_lambda[l]
        gsub = diff_subln_g[l].reshape(1, PAIR)
        pair_blk = lambda off: off // PAIR

        hl = _norm_mod(cfg, xl, g[0], mod_l, s, 0, 1)
        hc = _norm_mod(cfg, xc, g[0], mod_c, t_ctx, 0, 1)
        p_lat = _matmul(cfg, hl, w_in_l)
        p_lat3 = p_lat.reshape(b, s, cfg.n_in)
        qk_rot = _rope(cfg, p_lat3, cos, sin)
        if last:
            p_ctx3 = _matmul(cfg, hc, w_in_l[:, cfg.off_k:cfg.off_f]).reshape(b, n_ctx, 2 * QK_DIM)
            kc_blk, vc_blk = 0, pair_blk(QK_DIM)
        else:
            p_ctx = _matmul(cfg, hc, w_in_l)
            p_ctx3 = p_ctx.reshape(b, n_ctx, cfg.n_in)
            kc_blk, vc_blk = pair_blk(cfg.off_k), pair_blk(cfg.off_va)
        al = _attention(cfg, qk_rot[0], 0, p_ctx3, kc_blk, p_ctx3, vc_blk, dl, gsub, lam_init, cfg.tq,
                        k_lat=qk_rot, v_arr=p_lat3, v_blk=pair_blk(cfg.off_va))
        four_l = _fourier_latent(cfg, p_lat, fft_tables)
        sgu_l = _sgu(cfg, p_lat, sw, sbt, sg)
        xl = _merge(cfg, p_lat, four_l, sgu_l, al.reshape(t_lat, QK_DIM), xl, bg, wf, ws, wa, wo, g[1], mod_l, s)
        if not last:
            ac = _attention(cfg, p_ctx3, pair_blk(cfg.off_q), p_ctx3, kc_blk, p_ctx3, vc_blk, dl, gsub,
                            lam_init, n_ctx)
            four_c = _fourier_ctx(cfg, p_ctx)
            sgu_c = _sgu(cfg, p_ctx, sw, sbt, sg)
            xc = _merge(cfg, p_ctx, four_c, sgu_c, ac.reshape(t_ctx, QK_DIM), xc, bg, wf, ws, wa, wo, g[1],
                        mod_c, t_ctx)

        i = l // 2
        streams = [(xl, mod_l, s)] + ([] if last else [(xc, mod_c, t_ctx)])
        outs = []
        if l % 2 == 0:
            w1, w3, w2 = (ffn_w1[i][None].astype(BF16), ffn_w3[i][None].astype(BF16),
                          ffn_w2[i][None].astype(BF16))
            for xs, mod, rpm in streams:
                t = xs.shape[0]
                fl = _norm_mod(cfg, xs, g[2], mod, rpm, 3, 4)
                tm_up = min(cfg.tm_up, t)
                h = _ffn_up(cfg, fl, w1, w3, *_dense_plan(t, tm_up), tm_up)
                tm_dn = min(cfg.tm_down, t)
                outs.append(_ffn_down(cfg, h, w2, *_dense_plan(t, tm_dn), tm_dn, x=xs, g=g[3], mod=mod,
                                      rows_per_mod=rpm))
        else:
            w1, w3, w2 = moe_w1[i].astype(BF16), moe_w3[i].astype(BF16), moe_w2[i].astype(BF16)
            rwt = router_w[i].T
            tm = cfg.tm_down
            for xs, mod, rpm in streams:
                fl, top_i, top_w = _norm_mod(cfg, xs, g[2], mod, rpm, 3, 4, router_wt=rwt)
                row_tok, pos, tile_eid, n_active = _dispatch_plan(top_i[:TOP_K], tm)
                a = jnp.take(fl, row_tok, axis=0)
                h = _ffn_up(cfg, a, w1, w3, tile_eid, n_active, tm)
                y = _ffn_down(cfg, h, w2, tile_eid, n_active, tm)
                outs.append(_combine(cfg, jnp.take(y, pos[0], axis=0), jnp.take(y, pos[1], axis=0),
                                     top_w[:TOP_K].T, xs, g[3], mod, rpm))
        xl = outs[0]
        if not last:
            xc = outs[1]
    return xl.reshape(b, s, d)


def kernel(x, c, ctx, c_ctx, w_mod, b_mod, g_norm, w_in, b_gate, w_fourier_out, w_sgu_out, w_attn_out, w_o,
           sgu_w, sgu_b, sgu_g, diff_lambda, diff_subln_g, ffn_w1, ffn_w3, ffn_w2, router_w,
           moe_w1, moe_w3, moe_w2):
    return _forward(PROD, x, c, ctx, c_ctx, w_mod, b_mod, g_norm, w_in, b_gate, w_fourier_out, w_sgu_out,
                    w_attn_out, w_o, sgu_w, sgu_b, sgu_g, diff_lambda, diff_subln_g,
                    ffn_w1, ffn_w3, ffn_w2, router_w, moe_w1, moe_w3, moe_w2)
```

```python
import functools
import math
from typing import NamedTuple

import jax
import jax.numpy as jnp
from jax import lax
from jax.experimental import pallas as pl
from jax.experimental.pallas import tpu as pltpu

F32 = jnp.float32
BF16 = jnp.bfloat16

EPS = 1e-6
GRID_W = 64
ROPE_BASE = 10000.0
HEAD_DIM = 64
HEADS = 8
PAIR = 2 * HEAD_DIM
QK_DIM = HEADS * PAIR
GROUP_DIM = 128
GROUPS = 4
BRANCH_DIM = GROUPS * GROUP_DIM
CHUNK = 128
FFT_N1 = 128
N_EXPERTS = 8
TOP_K = 2
EXPERT_ROWS = 8
ATTN_SCALE = HEAD_DIM ** -0.5
Q_SCALE = ATTN_SCALE * math.log2(math.e)
VMEM_LIMIT = 56 * 1024 * 1024


class Cfg(NamedTuple):
    d: int
    batch: int
    seq: int
    ctx: int
    dff: int
    depth: int
    tm_norm: int
    tm_mm: int
    tn_mm: int
    tm_rope: int
    tq: int
    tk: int
    tm_sgu: int
    tm_merge: int
    tm_up: int
    tn_up: int
    tm_down: int
    tk_down: int
    tn_mod: int

    @property
    def n_in(self):
        return 3 * self.d + 3 * QK_DIM + 3 * BRANCH_DIM

    @property
    def off_q(self):
        return 3 * self.d

    @property
    def off_k(self):
        return self.off_q + QK_DIM

    @property
    def off_va(self):
        return self.off_k + QK_DIM

    @property
    def off_f(self):
        return self.off_va + QK_DIM

    @property
    def off_u(self):
        return self.off_f + BRANCH_DIM

    @property
    def off_v(self):
        return self.off_u + BRANCH_DIM


PROD = Cfg(d=2048, batch=4, seq=8192, ctx=256, dff=5632, depth=4,
           tm_norm=512, tm_mm=1024, tn_mm=1536, tm_rope=512, tq=256, tk=512,
           tm_sgu=1024, tm_merge=256, tm_up=1024, tn_up=512, tm_down=512, tk_down=1408,
           tn_mod=1024)


def _params(*sem):
    return pltpu.CompilerParams(dimension_semantics=sem, vmem_limit_bytes=VMEM_LIMIT)


def _rms(y, g):
    return y * lax.rsqrt(jnp.mean(y * y, axis=-1, keepdims=True) + EPS) * g


def _mod_kernel(c_ref, w_ref, b_ref, o_ref):
    sc = jax.nn.silu(c_ref[...])
    o_ref[0] = jnp.dot(sc.astype(BF16), w_ref[0].astype(BF16), preferred_element_type=F32) + b_ref[0]


def _modulation(cfg, c_rows, w_mod, b_mod):
    depth, d, n6 = w_mod.shape
    tn = cfg.tn_mod
    return pl.pallas_call(
        _mod_kernel,
        grid=(depth, n6 // tn),
        in_specs=[pl.BlockSpec((8, d), lambda l, j: (0, 0)),
                  pl.BlockSpec((1, d, tn), lambda l, j: (l, 0, j)),
                  pl.BlockSpec((1, 1, tn), lambda l, j: (l, 0, j))],
        out_specs=pl.BlockSpec((1, 8, tn), lambda l, j: (l, 0, j)),
        out_shape=jax.ShapeDtypeStruct((depth, 8, n6), F32),
        compiler_params=_params("parallel", "parallel"),
        name="modulation",
    )(c_rows, w_mod, b_mod.reshape(depth, 1, n6))


def _norm_mod_kernel(x_ref, g_ref, mod_ref, o_ref, *, shift_idx, scale_idx):
    y = _rms(x_ref[...], g_ref[...])
    m = mod_ref[0]
    o_ref[...] = (y * (1.0 + m[scale_idx:scale_idx + 1]) + m[shift_idx:shift_idx + 1]).astype(o_ref.dtype)


def _norm_mod_router_kernel(x_ref, g_ref, mod_ref, rw_ref, o_ref, idx_ref, wt_ref, *, shift_idx, scale_idx):
    y = _rms(x_ref[...], g_ref[...])
    m = mod_ref[0]
    h = y * (1.0 + m[scale_idx:scale_idx + 1]) + m[shift_idx:shift_idx + 1]
    o_ref[...] = h.astype(o_ref.dtype)
    logits = lax.dot_general(rw_ref[...], h, (((1,), (1,)), ((), ())),
                             precision=lax.Precision.HIGHEST, preferred_element_type=F32)
    row = lax.broadcasted_iota(jnp.int32, logits.shape, 0).astype(F32)
    m1 = jnp.max(logits, axis=0, keepdims=True)
    i1 = jnp.min(jnp.where(logits == m1, row, float(N_EXPERTS)), axis=0, keepdims=True)
    rest = jnp.where(row == i1, -jnp.inf, logits)
    m2 = jnp.max(rest, axis=0, keepdims=True)
    i2 = jnp.min(jnp.where(rest == m2, row, float(N_EXPERTS)), axis=0, keepdims=True)
    e2 = jnp.exp(m2 - m1)
    w1 = 1.0 / (1.0 + e2)
    w2 = e2 / (1.0 + e2)
    idx_ref[...] = jnp.where(row == 0.0, i1, jnp.where(row == 1.0, i2, 0.0)).astype(jnp.int32)
    wt_ref[...] = jnp.where(row == 0.0, w1, jnp.where(row == 1.0, w2, 0.0))


def _norm_mod(cfg, x, g, mod, rows_per_mod, shift_idx, scale_idx, router_wt=None):
    t, d = x.shape
    tm = min(cfg.tm_norm, t)
    in_specs = [pl.BlockSpec((tm, d), lambda i: (i, 0)),
                pl.BlockSpec((1, d), lambda i: (0, 0)),
                pl.BlockSpec((1, 6, d), lambda i: (i * tm // rows_per_mod, 0, 0))]
    h_spec = pl.BlockSpec((tm, d), lambda i: (i, 0))
    h_shape = jax.ShapeDtypeStruct((t, d), BF16)
    if router_wt is None:
        return pl.pallas_call(
            functools.partial(_norm_mod_kernel, shift_idx=shift_idx, scale_idx=scale_idx),
            grid=(t // tm,), in_specs=in_specs, out_specs=h_spec, out_shape=h_shape,
            compiler_params=_params("parallel"), name="norm_mod",
        )(x, g, mod)
    r_spec = pl.BlockSpec((EXPERT_ROWS, tm), lambda i: (0, i))
    return pl.pallas_call(
        functools.partial(_norm_mod_router_kernel, shift_idx=shift_idx, scale_idx=scale_idx),
        grid=(t // tm,),
        in_specs=in_specs + [pl.BlockSpec((N_EXPERTS, d), lambda i: (0, 0))],
        out_specs=[h_spec, r_spec, r_spec],
        out_shape=[h_shape, jax.ShapeDtypeStruct((EXPERT_ROWS, t), jnp.int32),
                   jax.ShapeDtypeStruct((EXPERT_ROWS, t), F32)],
        compiler_params=_params("parallel"), name="norm_mod_router",
    )(x, g, mod, router_wt)


def _matmul_kernel(a_ref, w_ref, o_ref):
    o_ref[...] = jnp.dot(a_ref[...], w_ref[...], preferred_element_type=F32).astype(o_ref.dtype)


def _matmul(cfg, a, w):
    t, k = a.shape
    n = w.shape[1]
    tm = min(cfg.tm_mm, t)
    tn = cfg.tn_mm if n % cfg.tn_mm == 0 else 512
    return pl.pallas_call(
        _matmul_kernel,
        grid=(t // tm, n // tn),
        in_specs=[pl.BlockSpec((tm, k), lambda i, j: (i, 0)),
                  pl.BlockSpec((k, tn), lambda i, j: (0, j))],
        out_specs=pl.BlockSpec((tm, tn), lambda i, j: (i, j)),
        out_shape=jax.ShapeDtypeStruct((t, n), BF16),
        compiler_params=_params("parallel", "parallel"), name="in_proj",
    )(a, w)


def _rope_tables(seq):
    rows = seq // GRID_W
    r = jnp.repeat(jnp.arange(rows, dtype=F32), GRID_W)
    col = jnp.tile(jnp.arange(GRID_W, dtype=F32), rows)
    n_freq = HEAD_DIM // 4
    inv = ROPE_BASE ** (-jnp.arange(n_freq, dtype=F32) / n_freq)
    ang = jnp.concatenate([r[:, None] * inv, col[:, None] * inv], axis=-1)
    cos, sin = jnp.cos(ang), jnp.sin(ang)
    return jnp.tile(cos, (1, 4)), jnp.tile(jnp.concatenate([-sin, sin], axis=-1), (1, 2))


def _rope_kernel(p_ref, cos_ref, sin_ref, o_ref):
    scale = jnp.where(pl.program_id(2) == 0, Q_SCALE, 1.0).astype(F32)
    cos = cos_ref[...] * scale
    sin = sin_ref[...] * scale
    lane = lax.broadcasted_iota(jnp.int32, cos.shape, 1)
    first_half = (lane % HEAD_DIM) < (HEAD_DIM // 2)
    for c in range(QK_DIM // PAIR):
        t = p_ref[0, :, c * PAIR:(c + 1) * PAIR].astype(F32)
        partner = jnp.where(first_half, pltpu.roll(t, PAIR - HEAD_DIM // 2, 1), pltpu.roll(t, HEAD_DIM // 2, 1))
        o_ref[0, 0, :, c * PAIR:(c + 1) * PAIR] = (t * cos + partner * sin).astype(o_ref.dtype)


def _rope(cfg, p3, cos, sin):
    b, s, _ = p3.shape
    tm = cfg.tm_rope
    qblk = cfg.off_q // QK_DIM
    return pl.pallas_call(
        _rope_kernel,
        grid=(b, s // tm, 2),
        in_specs=[pl.BlockSpec((1, tm, QK_DIM), lambda bi, i, j: (bi, i, qblk + j)),
                  pl.BlockSpec((tm, PAIR), lambda bi, i, j: (i, 0)),
                  pl.BlockSpec((tm, PAIR), lambda bi, i, j: (i, 0))],
        out_specs=pl.BlockSpec((1, 1, tm, QK_DIM), lambda bi, i, j: (j, bi, i, 0)),
        out_shape=jax.ShapeDtypeStruct((2, b, s, QK_DIM), BF16),
        compiler_params=_params("parallel", "parallel", "parallel"), name="rope",
    )(p3, cos, sin)


def _attn_kernel(*refs, tq, tk, n_lat, lam_init, q_prescaled):
    if n_lat:
        q_ref, kc_ref, vc_ref, k_ref, v_ref, dl_ref, g_ref, o_ref, q2_ref, vx_ref = refs
    else:
        q_ref, kc_ref, vc_ref, dl_ref, g_ref, o_ref, q2_ref, vx_ref = refs
    lc = kc_ref.shape[1]

    @pl.when(pl.program_id(2) == 0)
    def _():
        vx_ref[:lc, :PAIR] = vc_ref[0]
        vx_ref[:lc, PAIR:] = jnp.ones((lc, PAIR), BF16)
        if n_lat:
            vx_ref[lc:, :PAIR] = v_ref[0]
            vx_ref[lc:, PAIR:] = jnp.ones((n_lat * tk, PAIR), BF16)

    q = q_ref[0]
    if not q_prescaled:
        q = (q.astype(F32) * Q_SCALE).astype(BF16)
    lane = lax.broadcasted_iota(jnp.int32, q.shape, 1)
    zero = jnp.zeros_like(q)
    q2_ref[:tq] = jnp.where(lane < HEAD_DIM, q, zero)
    q2_ref[tq:] = jnp.where(lane >= HEAD_DIM, q, zero)
    q2 = q2_ref[...]

    chunks = [(kc_ref[0], 0, lc)]
    for j in range(n_lat):
        chunks.append((k_ref[0, 0, j * tk:(j + 1) * tk, :], lc + j * tk, tk))
    m = jnp.full((2 * tq, 1), -jnp.inf, F32)
    acc = jnp.zeros((2 * tq, 2 * PAIR), F32)
    for k, off, n in chunks:
        s = lax.dot_general(q2, k, (((1,), (1,)), ((), ())), preferred_element_type=F32)
        m_new = jnp.maximum(m, jnp.max(s, axis=-1, keepdims=True))
        p = jnp.exp2(s - m_new).astype(BF16)
        acc = jnp.exp2(m - m_new) * acc + jnp.dot(p, vx_ref[off:off + n, :], preferred_element_type=F32)
        m = m_new

    o12 = acc[:, :PAIR] / acc[:, PAIR:]
    dl = dl_ref[...]
    lam = (jnp.exp(jnp.sum(dl[0:1] * dl[1:2], axis=-1, keepdims=True))
           - jnp.exp(jnp.sum(dl[2:3] * dl[3:4], axis=-1, keepdims=True)) + lam_init)
    o = o12[:tq] - lam * o12[tq:]
    o_ref[0] = (_rms(o, g_ref[...]) * (1.0 - lam_init)).astype(o_ref.dtype)


def _attention(cfg, q_arr, q_blk, kc_arr, kc_blk, vc_arr, vc_blk, dl, g_sub, lam_init, tq,
               k_lat=None, v_arr=None, v_blk=0):
    b, lq = q_arr.shape[0], q_arr.shape[1]
    lc = kc_arr.shape[1]
    in_specs = [pl.BlockSpec((1, tq, PAIR), lambda bi, h, i: (bi, i, q_blk + h)),
                pl.BlockSpec((1, lc, PAIR), lambda bi, h, i: (bi, 0, kc_blk + h)),
                pl.BlockSpec((1, lc, PAIR), lambda bi, h, i: (bi, 0, vc_blk + h))]
    args = [q_arr, kc_arr, vc_arr]
    n_lat = 0
    if k_lat is not None:
        s = k_lat.shape[2]
        n_lat = s // cfg.tk
        in_specs += [pl.BlockSpec((1, 1, s, PAIR), lambda bi, h, i: (1, bi, 0, h)),
                     pl.BlockSpec((1, s, PAIR), lambda bi, h, i: (bi, 0, v_blk + h))]
        args += [k_lat, v_arr]
    in_specs += [pl.BlockSpec((4, HEAD_DIM), lambda bi, h, i: (0, 0)),
                 pl.BlockSpec((1, PAIR), lambda bi, h, i: (0, 0))]
    args += [dl, g_sub]
    lk = lc + n_lat * cfg.tk
    return pl.pallas_call(
        functools.partial(_attn_kernel, tq=tq, tk=cfg.tk, n_lat=n_lat, lam_init=lam_init,
                          q_prescaled=k_lat is not None),
        grid=(b, HEADS, lq // tq),
        in_specs=in_specs,
        out_specs=pl.BlockSpec((1, tq, PAIR), lambda bi, h, i: (bi, i, h)),
        out_shape=jax.ShapeDtypeStruct((b, lq, QK_DIM), BF16),
        scratch_shapes=[pltpu.VMEM((2 * tq, PAIR), BF16), pltpu.VMEM((lk, 2 * PAIR), BF16)],
        compiler_params=_params("parallel", "parallel", "arbitrary"),
        name="diff_attention" if n_lat else "diff_attention_ctx",
    )(*args)


def _dft_cos_sin(n, scale=1.0):
    j = jnp.arange(n, dtype=jnp.int32)
    ang = ((j[:, None] * j[None, :]) % n).astype(F32) * (2.0 * math.pi / n)
    return jnp.cos(ang) * scale, jnp.sin(ang) * scale


def _fft_tables(seq):
    n1, n2 = FFT_N1, seq // FFT_N1
    c1, s1 = _dft_cos_sin(n1)
    stage1 = jnp.concatenate([c1, -s1], axis=0).astype(BF16)
    k1 = jnp.arange(n1, dtype=jnp.int32)
    m = jnp.arange(n2, dtype=jnp.int32)
    ang = (m[:, None] * k1[None, :]).astype(F32) * (2.0 * math.pi / seq)
    tw_cos, tw_sin = jnp.cos(ang)[:, :, None], jnp.sin(ang)[:, :, None]
    c2, s2 = _dft_cos_sin(n2)
    stage2 = jnp.concatenate([jnp.concatenate([c2, s2], axis=1),
                              jnp.concatenate([-s2, c2], axis=1)], axis=0).astype(BF16)
    cc, sc = _dft_cos_sin(GROUP_DIM, scale=(seq * GROUP_DIM) ** -0.5)
    return stage1, tw_cos, tw_sin, stage2, cc.astype(BF16), sc.astype(BF16)


def _fft_a_kernel(x_ref, w_ref, c_ref, s_ref, o_ref):
    a = jnp.dot(w_ref[...], x_ref[0], preferred_element_type=F32)
    re, im = a[:FFT_N1], a[FFT_N1:]
    c, s = c_ref[0], s_ref[0]
    o_ref[0, :, :BRANCH_DIM] = (re * c + im * s).astype(o_ref.dtype)
    o_ref[0, :, BRANCH_DIM:] = (im * c - re * s).astype(o_ref.dtype)


def _fft_b_kernel(x_ref, w_ref, cc_ref, sc_ref, o_ref, *, n2):
    x = x_ref[0, 0]
    z = jnp.concatenate([x[:, :BRANCH_DIM], x[:, BRANCH_DIM:]], axis=0)
    y = jnp.dot(w_ref[...], z, preferred_element_type=F32).astype(BF16)
    for g in range(GROUPS):
        sl = slice(g * GROUP_DIM, (g + 1) * GROUP_DIM)
        o_ref[0, :, sl] = (jnp.dot(y[:n2, sl], cc_ref[...], preferred_element_type=F32)
                           + jnp.dot(y[n2:, sl], sc_ref[...], preferred_element_type=F32)).astype(o_ref.dtype)


def _fourier_latent(cfg, p_lat, tables):
    b, s = cfg.batch, cfg.seq
    n1, n2 = FFT_N1, s // FFT_N1
    stage1, tw_cos, tw_sin, stage2, cc, sc = tables
    f = p_lat[:, cfg.off_f:cfg.off_f + BRANCH_DIM]
    mid = pl.pallas_call(
        _fft_a_kernel,
        grid=(b, n2),
        in_specs=[pl.BlockSpec((1, n1, BRANCH_DIM), lambda bi, m: (bi, 0, m)),
                  pl.BlockSpec((2 * n1, n1), lambda bi, m: (0, 0)),
                  pl.BlockSpec((1, n1, 1), lambda bi, m: (m, 0, 0)),
                  pl.BlockSpec((1, n1, 1), lambda bi, m: (m, 0, 0))],
        out_specs=pl.BlockSpec((1, n1, 2 * BRANCH_DIM), lambda bi, m: (bi, 0, m)),
        out_shape=jax.ShapeDtypeStruct((b, n1, n2 * 2 * BRANCH_DIM), BF16),
        compiler_params=_params("parallel", "parallel"), name="fft_stage_a",
    )(f.reshape(b, n1, n2 * BRANCH_DIM), stage1, tw_cos, tw_sin)
    out = pl.pallas_call(
        functools.partial(_fft_b_kernel, n2=n2),
        grid=(b, n1),
        in_specs=[pl.BlockSpec((1, 1, n2, 2 * BRANCH_DIM), lambda bi, k: (bi, k, 0, 0)),
                  pl.BlockSpec((2 * n2, 2 * n2), lambda bi, k: (0, 0)),
                  pl.BlockSpec((GROUP_DIM, GROUP_DIM), lambda bi, k: (0, 0)),
                  pl.BlockSpec((GROUP_DIM, GROUP_DIM), lambda bi, k: (0, 0))],
        out_specs=pl.BlockSpec((1, n2, BRANCH_DIM), lambda bi, k: (bi, 0, k)),
        out_shape=jax.ShapeDtypeStruct((b, n2, n1 * BRANCH_DIM), BF16),
        compiler_params=_params("parallel", "parallel"), name="fft_stage_b",
    )(mid.reshape(b, n1, n2, 2 * BRANCH_DIM), stage2, cc, sc)
    return out.reshape(b * s, BRANCH_DIM)


def _dft_dense_kernel(x_ref, c_ref, s_ref, cc_ref, sc_ref, o_ref):
    x = x_ref[0]
    yr = jnp.dot(c_ref[...], x, preferred_element_type=F32).astype(BF16)
    yi = (-jnp.dot(s_ref[...], x, preferred_element_type=F32)).astype(BF16)
    for g in range(GROUPS):
        sl = slice(g * GROUP_DIM, (g + 1) * GROUP_DIM)
        o_ref[0, :, sl] = (jnp.dot(yr[:, sl], cc_ref[...], preferred_element_type=F32)
                           + jnp.dot(yi[:, sl], sc_ref[...], preferred_element_type=F32)).astype(o_ref.dtype)


def _fourier_ctx(cfg, p_ctx):
    b, n = cfg.batch, cfg.ctx
    cn, sn = _dft_cos_sin(n)
    cc, sc = _dft_cos_sin(GROUP_DIM, scale=(n * GROUP_DIM) ** -0.5)
    fblk = cfg.off_f // BRANCH_DIM
    full = lambda bi: (0, 0)
    out = pl.pallas_call(
        _dft_dense_kernel,
        grid=(b,),
        in_specs=[pl.BlockSpec((1, n, BRANCH_DIM), lambda bi: (bi, 0, fblk)),
                  pl.BlockSpec((n, n), full), pl.BlockSpec((n, n), full),
                  pl.BlockSpec((GROUP_DIM, GROUP_DIM), full), pl.BlockSpec((GROUP_DIM, GROUP_DIM), full)],
        out_specs=pl.BlockSpec((1, n, BRANCH_DIM), lambda bi: (bi, 0, 0)),
        out_shape=jax.ShapeDtypeStruct((b, n, BRANCH_DIM), BF16),
        compiler_params=_params("parallel"), name="dft_ctx",
    )(p_ctx.reshape(b, n, cfg.n_in), cn.astype(BF16), sn.astype(BF16), cc.astype(BF16), sc.astype(BF16))
    return out.reshape(b * n, BRANCH_DIM)


def _sgu_kernel(u_ref, v_ref, g_ref, w_ref, b_ref, o_ref, *, n_chunks):
    gv = g_ref[...]
    for c in range(n_chunks):
        rows = slice(c * CHUNK, (c + 1) * CHUNK)
        u = jax.nn.gelu(u_ref[rows, :].astype(F32))
        v = jax.nn.gelu(v_ref[rows, :].astype(F32))
        vc = v - jnp.mean(v, axis=-1, keepdims=True)
        vn = (vc * lax.rsqrt(jnp.mean(vc * vc, axis=-1, keepdims=True) + EPS) * gv).astype(BF16)
        for g in range(GROUPS):
            sl = slice(g * GROUP_DIM, (g + 1) * GROUP_DIM)
            sv = jnp.dot(w_ref[g], vn[:, sl], preferred_element_type=F32) + b_ref[:, g:g + 1]
            o_ref[rows, sl] = (u[:, sl] * sv).astype(o_ref.dtype)


def _sgu(cfg, p2, sgu_w, sgu_bt, sgu_g):
    t = p2.shape[0]
    tm = min(cfg.tm_sgu, t)
    ublk, vblk = cfg.off_u // BRANCH_DIM, cfg.off_v // BRANCH_DIM
    return pl.pallas_call(
        functools.partial(_sgu_kernel, n_chunks=tm // CHUNK),
        grid=(t // tm,),
        in_specs=[pl.BlockSpec((tm, BRANCH_DIM), lambda i: (i, ublk)),
                  pl.BlockSpec((tm, BRANCH_DIM), lambda i: (i, vblk)),
                  pl.BlockSpec((1, BRANCH_DIM), lambda i: (0, 0)),
                  pl.BlockSpec((GROUPS, CHUNK, CHUNK), lambda i: (0, 0, 0)),
                  pl.BlockSpec((CHUNK, GROUPS), lambda i: (0, 0))],
        out_specs=pl.BlockSpec((tm, BRANCH_DIM), lambda i: (i, 0)),
        out_shape=jax.ShapeDtypeStruct((t, BRANCH_DIM), BF16),
        compiler_params=_params("parallel"), name="spatial_gating",
    )(p2, p2, sgu_g, sgu_w, sgu_bt)


def _merge_kernel(pg0_ref, pg1_ref, pg2_ref, f_ref, s_ref, a_ref, x_ref, bg_ref,
                  wf_ref, ws_ref, wa_ref, wo_ref, g_ref, mod_ref, o_ref, *, d):
    def gate(pg_ref, k):
        return jax.nn.sigmoid(pg_ref[...].astype(F32) + bg_ref[:, k * d:(k + 1) * d])

    merged = gate(pg0_ref, 0) * jnp.dot(f_ref[...], wf_ref[...], preferred_element_type=F32)
    merged += gate(pg1_ref, 1) * jnp.dot(s_ref[...], ws_ref[...], preferred_element_type=F32)
    merged += gate(pg2_ref, 2) * jnp.dot(a_ref[...], wa_ref[...], preferred_element_type=F32)
    y = jnp.dot(merged.astype(BF16), wo_ref[...], preferred_element_type=F32)
    o_ref[...] = x_ref[...] + mod_ref[0][2:3] * _rms(y, g_ref[...])


def _resident(shape):
    return pl.BlockSpec(shape, lambda *_: (0,) * len(shape), pipeline_mode=pl.Buffered(1))


def _merge(cfg, p2, four, sgu, attn, x, b_gate, wf, ws, wa, wo, g, mod, rows_per_mod):
    t, d = x.shape
    tm = cfg.tm_merge
    row = lambda i: (i, 0)
    return pl.pallas_call(
        functools.partial(_merge_kernel, d=d),
        grid=(t // tm,),
        in_specs=[pl.BlockSpec((tm, d), lambda i: (i, 0)),
                  pl.BlockSpec((tm, d), lambda i: (i, 1)),
                  pl.BlockSpec((tm, d), lambda i: (i, 2)),
                  pl.BlockSpec((tm, BRANCH_DIM), row), pl.BlockSpec((tm, BRANCH_DIM), row),
                  pl.BlockSpec((tm, QK_DIM), row), pl.BlockSpec((tm, d), row),
                  _resident((1, 3 * d)), _resident((BRANCH_DIM, d)), _resident((BRANCH_DIM, d)),
                  _resident((QK_DIM, d)), _resident((d, d)), _resident((1, d)),
                  pl.BlockSpec((1, 6, d), lambda i: (i * tm // rows_per_mod, 0, 0))],
        out_specs=pl.BlockSpec((tm, d), row),
        out_shape=jax.ShapeDtypeStruct((t, d), F32),
        compiler_params=_params("parallel"), name="merge",
    )(p2, p2, p2, four, sgu, attn, x, b_gate, wf, ws, wa, wo, g, mod)


def _ffn_up_kernel(eid_ref, nact_ref, a_ref, w1_ref, w3_ref, o_ref):
    active = pl.program_id(0) < nact_ref[0]

    @pl.when(active)
    def _():
        a = a_ref[...]
        h1 = jnp.dot(a, w1_ref[0], preferred_element_type=F32)
        h3 = jnp.dot(a, w3_ref[0], preferred_element_type=F32)
        o_ref[...] = (jax.nn.silu(h1) * h3).astype(o_ref.dtype)

    @pl.when(jnp.logical_not(active))
    def _():
        o_ref[...] = jnp.zeros(o_ref.shape, o_ref.dtype)


def _ffn_up(cfg, a, w1, w3, tile_eid, n_active, tm):
    r, d = a.shape
    dff = w1.shape[2]
    tn = cfg.tn_up
    grid_spec = pltpu.PrefetchScalarGridSpec(
        num_scalar_prefetch=2,
        grid=(r // tm, dff // tn),
        in_specs=[pl.BlockSpec((tm, d), lambda i, j, eid, na: (jnp.minimum(i, na[0] - 1), 0)),
                  pl.BlockSpec((1, d, tn), lambda i, j, eid, na: (eid[i], 0, j)),
                  pl.BlockSpec((1, d, tn), lambda i, j, eid, na: (eid[i], 0, j))],
        out_specs=pl.BlockSpec((tm, tn), lambda i, j, eid, na: (i, j)),
    )
    return pl.pallas_call(
        _ffn_up_kernel, grid_spec=grid_spec,
        out_shape=jax.ShapeDtypeStruct((r, dff), BF16),
        compiler_params=_params("parallel", "parallel"), name="swiglu_up",
    )(tile_eid, n_active, a, w1, w3)


def _ffn_down_kernel(eid_ref, nact_ref, *refs, residual):
    if residual:
        h_ref, w_ref, x_ref, g_ref, mod_ref, o_ref, acc_ref = refs
    else:
        h_ref, w_ref, o_ref, acc_ref = refs
    k = pl.program_id(1)

    @pl.when(k == 0)
    def _():
        acc_ref[...] = jnp.zeros(acc_ref.shape, F32)

    @pl.when(pl.program_id(0) < nact_ref[0])
    def _():
        acc_ref[...] += jnp.dot(h_ref[...], w_ref[0], preferred_element_type=F32)

    @pl.when(k == pl.num_programs(1) - 1)
    def _():
        if residual:
            o_ref[...] = x_ref[...] + mod_ref[0][5:6] * _rms(acc_ref[...], g_ref[...])
        else:
            o_ref[...] = acc_ref[...].astype(o_ref.dtype)


def _ffn_down(cfg, h, w2, tile_eid, n_active, tm, x=None, g=None, mod=None, rows_per_mod=None):
    r, dff = h.shape
    d = w2.shape[2]
    tk = cfg.tk_down
    residual = x is not None
    in_specs = [pl.BlockSpec((tm, tk), lambda i, k, eid, na: (jnp.minimum(i, na[0] - 1), k)),
                pl.BlockSpec((1, tk, d), lambda i, k, eid, na: (eid[i], k, 0))]
    args = [h, w2]
    if residual:
        in_specs += [pl.BlockSpec((tm, d), lambda i, k, eid, na: (i, 0)),
                     pl.BlockSpec((1, d), lambda i, k, eid, na: (0, 0)),
                     pl.BlockSpec((1, 6, d), lambda i, k, eid, na: (i * tm // rows_per_mod, 0, 0))]
        args += [x, g, mod]
    grid_spec = pltpu.PrefetchScalarGridSpec(
        num_scalar_prefetch=2,
        grid=(r // tm, dff // tk),
        in_specs=in_specs,
        out_specs=pl.BlockSpec((tm, d), lambda i, k, eid, na: (i, 0)),
        scratch_shapes=[pltpu.VMEM((tm, d), F32)],
    )
    return pl.pallas_call(
        functools.partial(_ffn_down_kernel, residual=residual), grid_spec=grid_spec,
        out_shape=jax.ShapeDtypeStruct((r, d), F32 if residual else BF16),
        compiler_params=_params("parallel", "arbitrary"),
        name="swiglu_down_residual" if residual else "swiglu_down",
    )(tile_eid, n_active, *args)


def _combine_kernel(y1_ref, y2_ref, wt_ref, x_ref, g_ref, mod_ref, o_ref):
    wt = wt_ref[...]
    y = wt[:, 0:1] * y1_ref[...].astype(F32) + wt[:, 1:2] * y2_ref[...].astype(F32)
    o_ref[...] = x_ref[...] + mod_ref[0][5:6] * _rms(y, g_ref[...])


def _combine(cfg, y1, y2, wt, x, g, mod, rows_per_mod):
    t, d = x.shape
    tm = min(cfg.tm_norm, t)
    row = lambda i: (i, 0)
    return pl.pallas_call(
        _combine_kernel,
        grid=(t // tm,),
        in_specs=[pl.BlockSpec((tm, d), row), pl.BlockSpec((tm, d), row),
                  pl.BlockSpec((tm, TOP_K), row), pl.BlockSpec((tm, d), row),
                  pl.BlockSpec((1, d), lambda i: (0, 0)),
                  pl.BlockSpec((1, 6, d), lambda i: (i * tm // rows_per_mod, 0, 0))],
        out_specs=pl.BlockSpec((tm, d), row),
        out_shape=jax.ShapeDtypeStruct((t, d), F32),
        compiler_params=_params("parallel"), name="moe_combine",
    )(y1, y2, wt, x, g, mod)


def _dispatch_plan(top_i, tm):
    t = top_i.shape[1]
    n_pairs = TOP_K * t
    n_rows = n_pairs + N_EXPERTS * tm
    e_flat = top_i.reshape(n_pairs)
    onehot = (e_flat[:, None] == jnp.arange(N_EXPERTS, dtype=jnp.int32)[None, :]).astype(jnp.int32)
    rank = jnp.sum(jnp.cumsum(onehot, axis=0) * onehot, axis=1) - 1
    counts = jnp.sum(onehot, axis=0)
    padded = (counts + tm - 1) // tm * tm
    ends = jnp.cumsum(padded)
    starts = ends - padded
    dest = starts[e_flat] + rank
    tok = jnp.tile(jnp.arange(t, dtype=jnp.int32), TOP_K)
    row_tok = jnp.zeros((n_rows,), jnp.int32).at[dest].set(tok)
    tile_start = jnp.arange(n_rows // tm, dtype=jnp.int32) * tm
    tile_eid = jnp.minimum(jnp.sum((tile_start[:, None] >= ends[None, :]).astype(jnp.int32), axis=1),
                           N_EXPERTS - 1)
    n_active = (ends[-1] // tm).astype(jnp.int32).reshape(1)
    return row_tok, dest.reshape(TOP_K, t), tile_eid, n_active


def _dense_plan(t, tm):
    return jnp.zeros((t // tm,), jnp.int32), jnp.full((1,), t // tm, jnp.int32)


def _permute_w_in(w):
    b3 = 3 * BRANCH_DIM
    q3 = 3 * QK_DIM
    return jnp.concatenate([w[:, b3 + q3:], w[:, b3:b3 + q3], w[:, :b3]], axis=1).astype(BF16)


def _forward(cfg, x, c, ctx, c_ctx, w_mod, b_mod, g_norm, w_in, b_gate, w_fourier_out, w_sgu_out,
             w_attn_out, w_o, sgu_w, sgu_b, sgu_g, diff_lambda, diff_subln_g,
             ffn_w1, ffn_w3, ffn_w2, router_w, moe_w1, moe_w3, moe_w2):
    b, s, d = x.shape
    n_ctx = ctx.shape[1]
    t_lat, t_ctx = b * s, b * n_ctx
    cos, sin = _rope_tables(s)
    fft_tables = _fft_tables(s)

    c_rows = jnp.concatenate([c, c_ctx[None, :], jnp.zeros((8 - b - 1, d), F32)], axis=0)
    mod_all = _modulation(cfg, c_rows, w_mod, b_mod)

    xl = x.reshape(t_lat, d)
    xc = ctx.reshape(t_ctx, d)
    for l in range(cfg.depth):
        last = l == cfg.depth - 1
        lam_init = 0.8 - 0.6 * math.exp(-0.3 * l)
        mod_l = mod_all[l, :b].reshape(b, 6, d)
        mod_c = mod_all[l, b:b + 1].reshape(1, 6, d)
        g = g_norm[l].reshape(4, 1, d)
        w_in_l = _permute_w_in(w_in[l])
        wf, ws = w_fourier_out[l].astype(BF16), w_sgu_out[l].astype(BF16)
        wa, wo = w_attn_out[l].astype(BF16), w_o[l].astype(BF16)
        bg = b_gate[l].reshape(1, 3 * d)
        sw = sgu_w[l].astype(BF16)
        sbt = sgu_b[l].T
        sg = sgu_g[l].reshape(1, BRANCH_DIM)
        dl = diff_lambda[l]
        gsub = diff_subln_g[l].reshape(1, PAIR)
        pair_blk = lambda off: off // PAIR

        hl = _norm_mod(cfg, xl, g[0], mod_l, s, 0, 1)
        hc = _norm_mod(cfg, xc, g[0], mod_c, t_ctx, 0, 1)
        p_lat = _matmul(cfg, hl, w_in_l)
        p_lat3 = p_lat.reshape(b, s, cfg.n_in)
        qk_rot = _rope(cfg, p_lat3, cos, sin)
        if last:
            p_ctx3 = _matmul(cfg, hc, w_in_l[:, cfg.off_k:cfg.off_f]).reshape(b, n_ctx, 2 * QK_DIM)
            kc_blk, vc_blk = 0, pair_blk(QK_DIM)
        else:
            p_ctx = _matmul(cfg, hc, w_in_l)
            p_ctx3 = p_ctx.reshape(b, n_ctx, cfg.n_in)
            kc_blk, vc_blk = pair_blk(cfg.off_k), pair_blk(cfg.off_va)
        al = _attention(cfg, qk_rot[0], 0, p_ctx3, kc_blk, p_ctx3, vc_blk, dl, gsub, lam_init, cfg.tq,
                        k_lat=qk_rot, v_arr=p_lat3, v_blk=pair_blk(cfg.off_va))
        four_l = _fourier_latent(cfg, p_lat, fft_tables)
        sgu_l = _sgu(cfg, p_lat, sw, sbt, sg)
        xl = _merge(cfg, p_lat, four_l, sgu_l, al.reshape(t_lat, QK_DIM), xl, bg, wf, ws, wa, wo, g[1], mod_l, s)
        if not last:
            ac = _attention(cfg, p_ctx3, pair_blk(cfg.off_q), p_ctx3, kc_blk, p_ctx3, vc_blk, dl, gsub,
                            lam_init, n_ctx)
            four_c = _fourier_ctx(cfg, p_ctx)
            sgu_c = _sgu(cfg, p_ctx, sw, sbt, sg)
            xc = _merge(cfg, p_ctx, four_c, sgu_c, ac.reshape(t_ctx, QK_DIM), xc, bg, wf, ws, wa, wo, g[1],
                        mod_c, t_ctx)

        i = l // 2
        streams = [(xl, mod_l, s)] + ([] if last else [(xc, mod_c, t_ctx)])
        outs = []
        if l % 2 == 0:
            w1, w3, w2 = (ffn_w1[i][None].astype(BF16), ffn_w3[i][None].astype(BF16),
                          ffn_w2[i][None].astype(BF16))
            for xs, mod, rpm in streams:
                t = xs.shape[0]
                fl = _norm_mod(cfg, xs, g[2], mod, rpm, 3, 4)
                tm_up = min(cfg.tm_up, t)
                h = _ffn_up(cfg, fl, w1, w3, *_dense_plan(t, tm_up), tm_up)
                tm_dn = min(cfg.tm_down, t)
                outs.append(_ffn_down(cfg, h, w2, *_dense_plan(t, tm_dn), tm_dn, x=xs, g=g[3], mod=mod,
                                      rows_per_mod=rpm))
        else:
            w1, w3, w2 = moe_w1[i].astype(BF16), moe_w3[i].astype(BF16), moe_w2[i].astype(BF16)
            rwt = router_w[i].T
            tm = cfg.tm_down
            for xs, mod, rpm in streams:
                fl, top_i, top_w = _norm_mod(cfg, xs, g[2], mod, rpm, 3, 4, router_wt=rwt)
                row_tok, pos, tile_eid, n_active = _dispatch_plan(top_i[:TOP_K], tm)
                a = jnp.take(fl, row_tok, axis=0)
                h = _ffn_up(cfg, a, w1, w3, tile_eid, n_active, tm)
                y = _ffn_down(cfg, h, w2, tile_eid, n_active, tm)
                outs.append(_combine(cfg, jnp.take(y, pos[0], axis=0), jnp.take(y, pos[1], axis=0),
                                     top_w[:TOP_K].T, xs, g[3], mod, rpm))
        xl = outs[0]
        if not last:
            xc = outs[1]
    return xl.reshape(b, s, d)


def kernel(x, c, ctx, c_ctx, w_mod, b_mod, g_norm, w_in, b_gate, w_fourier_out, w_sgu_out, w_attn_out, w_o,
           sgu_w, sgu_b, sgu_g, diff_lambda, diff_subln_g, ffn_w1, ffn_w3, ffn_w2, router_w,
           moe_w1, moe_w3, moe_w2):
    return _forward(PROD, x, c, ctx, c_ctx, w_mod, b_mod, g_norm, w_in, b_gate, w_fourier_out, w_sgu_out,
                    w_attn_out, w_o, sgu_w, sgu_b, sgu_g, diff_lambda, diff_subln_g,
                    ffn_w1, ffn_w3, ffn_w2, router_w, moe_w1, moe_w3, moe_w2)
```

```python
import functools
import math
from typing import NamedTuple

import jax
import jax.numpy as jnp
from jax import lax
from jax.experimental import pallas as pl
from jax.experimental.pallas import tpu as pltpu

F32 = jnp.float32
BF16 = jnp.bfloat16

EPS = 1e-6
GRID_W = 64
ROPE_BASE = 10000.0
HEAD_DIM = 64
HEADS = 8
PAIR = 2 * HEAD_DIM
QK_DIM = HEADS * PAIR
GROUP_DIM = 128
GROUPS = 4
BRANCH_DIM = GROUPS * GROUP_DIM
CHUNK = 128
FFT_N1 = 128
FFT_A_PER_STEP = 2
FFT_B_PER_STEP = 4
N_EXPERTS = 8
TOP_K = 2
EXPERT_ROWS = 8
ATTN_SCALE = HEAD_DIM ** -0.5
Q_SCALE = ATTN_SCALE * math.log2(math.e)
VMEM_LIMIT = 56 * 1024 * 1024


class Cfg(NamedTuple):
    d: int
    batch: int
    seq: int
    ctx: int
    dff: int
    depth: int
    tm_norm: int
    tm_mm: int
    tn_mm: int
    tm_rope: int
    tq: int
    tk: int
    tm_sgu: int
    tm_merge: int
    tm_up: int
    tn_up: int
    tm_down: int
    tk_down: int
    tm_down_dense: int
    tk_down_dense: int
    tn_mod: int

    @property
    def n_in(self):
        return 3 * self.d + 3 * QK_DIM + 3 * BRANCH_DIM

    @property
    def off_q(self):
        return 3 * self.d

    @property
    def off_k(self):
        return self.off_q + QK_DIM

    @property
    def off_va(self):
        return self.off_k + QK_DIM

    @property
    def off_f(self):
        return self.off_va + QK_DIM

    @property
    def off_u(self):
        return self.off_f + BRANCH_DIM

    @property
    def off_v(self):
        return self.off_u + BRANCH_DIM


PROD = Cfg(d=2048, batch=4, seq=8192, ctx=256, dff=5632, depth=4,
           tm_norm=512, tm_mm=1024, tn_mm=1536, tm_rope=512, tq=512, tk=256,
           tm_sgu=1024, tm_merge=256, tm_up=1024, tn_up=512, tm_down=512, tk_down=1408,
           tm_down_dense=1024, tk_down_dense=512, tn_mod=1024)


def _params(*sem):
    return pltpu.CompilerParams(dimension_semantics=sem, vmem_limit_bytes=VMEM_LIMIT)


def _rms(y, g):
    return y * lax.rsqrt(jnp.mean(y * y, axis=-1, keepdims=True) + EPS) * g


def _mod_kernel(c_ref, w_ref, b_ref, o_ref):
    sc = jax.nn.silu(c_ref[...])
    o_ref[0] = jnp.dot(sc.astype(BF16), w_ref[0].astype(BF16), preferred_element_type=F32) + b_ref[0]


def _modulation(cfg, c_rows, w_mod, b_mod):
    depth, d, n6 = w_mod.shape
    tn = cfg.tn_mod
    return pl.pallas_call(
        _mod_kernel,
        grid=(depth, n6 // tn),
        in_specs=[pl.BlockSpec((8, d), lambda l, j: (0, 0)),
                  pl.BlockSpec((1, d, tn), lambda l, j: (l, 0, j)),
                  pl.BlockSpec((1, 1, tn), lambda l, j: (l, 0, j))],
        out_specs=pl.BlockSpec((1, 8, tn), lambda l, j: (l, 0, j)),
        out_shape=jax.ShapeDtypeStruct((depth, 8, n6), F32),
        compiler_params=_params("parallel", "parallel"),
        name="modulation",
    )(c_rows, w_mod, b_mod.reshape(depth, 1, n6))


def _norm_mod_kernel(x_ref, g_ref, mod_ref, o_ref, *, shift_idx, scale_idx):
    y = _rms(x_ref[...], g_ref[...])
    m = mod_ref[0]
    o_ref[...] = (y * (1.0 + m[scale_idx:scale_idx + 1]) + m[shift_idx:shift_idx + 1]).astype(o_ref.dtype)


def _norm_mod_router_kernel(x_ref, g_ref, mod_ref, rw_ref, o_ref, idx_ref, wt_ref, *, shift_idx, scale_idx):
    y = _rms(x_ref[...], g_ref[...])
    m = mod_ref[0]
    h = y * (1.0 + m[scale_idx:scale_idx + 1]) + m[shift_idx:shift_idx + 1]
    o_ref[...] = h.astype(o_ref.dtype)
    logits = lax.dot_general(rw_ref[...], h, (((1,), (1,)), ((), ())),
                             precision=lax.Precision.HIGHEST, preferred_element_type=F32)
    row = lax.broadcasted_iota(jnp.int32, logits.shape, 0).astype(F32)
    m1 = jnp.max(logits, axis=0, keepdims=True)
    i1 = jnp.min(jnp.where(logits == m1, row, float(N_EXPERTS)), axis=0, keepdims=True)
    rest = jnp.where(row == i1, -jnp.inf, logits)
    m2 = jnp.max(rest, axis=0, keepdims=True)
    i2 = jnp.min(jnp.where(rest == m2, row, float(N_EXPERTS)), axis=0, keepdims=True)
    e2 = jnp.exp(m2 - m1)
    w1 = 1.0 / (1.0 + e2)
    w2 = e2 / (1.0 + e2)
    idx_ref[...] = jnp.where(row == 0.0, i1, jnp.where(row == 1.0, i2, 0.0)).astype(jnp.int32)
    wt_ref[...] = jnp.where(row == 0.0, w1, jnp.where(row == 1.0, w2, 0.0))


def _norm_mod(cfg, x, g, mod, rows_per_mod, shift_idx, scale_idx, router_wt=None):
    t, d = x.shape
    tm = min(cfg.tm_norm, t)
    in_specs = [pl.BlockSpec((tm, d), lambda i: (i, 0)),
                pl.BlockSpec((1, d), lambda i: (0, 0)),
                pl.BlockSpec((1, 6, d), lambda i: (i * tm // rows_per_mod, 0, 0))]
    h_spec = pl.BlockSpec((tm, d), lambda i: (i, 0))
    h_shape = jax.ShapeDtypeStruct((t, d), BF16)
    if router_wt is None:
        return pl.pallas_call(
            functools.partial(_norm_mod_kernel, shift_idx=shift_idx, scale_idx=scale_idx),
            grid=(t // tm,), in_specs=in_specs, out_specs=h_spec, out_shape=h_shape,
            compiler_params=_params("parallel"), name="norm_mod",
        )(x, g, mod)
    r_spec = pl.BlockSpec((EXPERT_ROWS, tm), lambda i: (0, i))
    return pl.pallas_call(
        functools.partial(_norm_mod_router_kernel, shift_idx=shift_idx, scale_idx=scale_idx),
        grid=(t // tm,),
        in_specs=in_specs + [pl.BlockSpec((N_EXPERTS, d), lambda i: (0, 0))],
        out_specs=[h_spec, r_spec, r_spec],
        out_shape=[h_shape, jax.ShapeDtypeStruct((EXPERT_ROWS, t), jnp.int32),
                   jax.ShapeDtypeStruct((EXPERT_ROWS, t), F32)],
        compiler_params=_params("parallel"), name="norm_mod_router",
    )(x, g, mod, router_wt)


def _matmul_kernel(a_ref, w_ref, o_ref):
    o_ref[...] = jnp.dot(a_ref[...], w_ref[...], preferred_element_type=F32).astype(o_ref.dtype)


def _matmul(cfg, a, w):
    t, k = a.shape
    n = w.shape[1]
    tm = min(cfg.tm_mm, t)
    tn = cfg.tn_mm if n % cfg.tn_mm == 0 else 512
    return pl.pallas_call(
        _matmul_kernel,
        grid=(t // tm, n // tn),
        in_specs=[pl.BlockSpec((tm, k), lambda i, j: (i, 0)),
                  pl.BlockSpec((k, tn), lambda i, j: (0, j))],
        out_specs=pl.BlockSpec((tm, tn), lambda i, j: (i, j)),
        out_shape=jax.ShapeDtypeStruct((t, n), BF16),
        compiler_params=_params("parallel", "parallel"), name="in_proj",
    )(a, w)


def _rope_tables(seq):
    rows = seq // GRID_W
    r = jnp.repeat(jnp.arange(rows, dtype=F32), GRID_W)
    col = jnp.tile(jnp.arange(GRID_W, dtype=F32), rows)
    n_freq = HEAD_DIM // 4
    inv = ROPE_BASE ** (-jnp.arange(n_freq, dtype=F32) / n_freq)
    ang = jnp.concatenate([r[:, None] * inv, col[:, None] * inv], axis=-1)
    cos, sin = jnp.cos(ang), jnp.sin(ang)
    return jnp.tile(cos, (1, 4)), jnp.tile(jnp.concatenate([-sin, sin], axis=-1), (1, 2))


def _rope_kernel(p_ref, cos_ref, sin_ref, o_ref):
    scale = jnp.where(pl.program_id(2) == 0, Q_SCALE, 1.0).astype(F32)
    cos = cos_ref[...] * scale
    sin = sin_ref[...] * scale
    lane = lax.broadcasted_iota(jnp.int32, cos.shape, 1)
    first_half = (lane % HEAD_DIM) < (HEAD_DIM // 2)
    for c in range(QK_DIM // PAIR):
        t = p_ref[0, :, c * PAIR:(c + 1) * PAIR].astype(F32)
        partner = jnp.where(first_half, pltpu.roll(t, PAIR - HEAD_DIM // 2, 1), pltpu.roll(t, HEAD_DIM // 2, 1))
        o_ref[0, 0, :, c * PAIR:(c + 1) * PAIR] = (t * cos + partner * sin).astype(o_ref.dtype)


def _rope(cfg, p3, cos, sin):
    b, s, _ = p3.shape
    tm = cfg.tm_rope
    qblk = cfg.off_q // QK_DIM
    return pl.pallas_call(
        _rope_kernel,
        grid=(b, s // tm, 2),
        in_specs=[pl.BlockSpec((1, tm, QK_DIM), lambda bi, i, j: (bi, i, qblk + j)),
                  pl.BlockSpec((tm, PAIR), lambda bi, i, j: (i, 0)),
                  pl.BlockSpec((tm, PAIR), lambda bi, i, j: (i, 0))],
        out_specs=pl.BlockSpec((1, 1, tm, QK_DIM), lambda bi, i, j: (j, bi, i, 0)),
        out_shape=jax.ShapeDtypeStruct((2, b, s, QK_DIM), BF16),
        compiler_params=_params("parallel", "parallel", "parallel"), name="rope",
    )(p3, cos, sin)


def _attn_kernel(*refs, tq, tk, n_lat, q_prescaled):
    if n_lat:
        q_ref, kc_ref, vc_ref, k_ref, v_ref, dl_ref, g_ref, li_ref, o_ref, q2_ref, vx_ref = refs
    else:
        q_ref, kc_ref, vc_ref, dl_ref, g_ref, li_ref, o_ref, q2_ref, vx_ref = refs
    lc = kc_ref.shape[1]

    @pl.when(pl.program_id(2) == 0)
    def _():
        vx_ref[:lc, :PAIR] = vc_ref[0]
        vx_ref[:lc, PAIR:] = jnp.ones((lc, PAIR), BF16)
        if n_lat:
            vx_ref[lc:, :PAIR] = v_ref[0]
            vx_ref[lc:, PAIR:] = jnp.ones((n_lat * tk, PAIR), BF16)

    q = q_ref[0]
    if not q_prescaled:
        q = (q.astype(F32) * Q_SCALE).astype(BF16)
    lane = lax.broadcasted_iota(jnp.int32, q.shape, 1)
    zero = jnp.zeros_like(q)
    q2_ref[:tq] = jnp.where(lane < HEAD_DIM, q, zero)
    q2_ref[tq:] = jnp.where(lane >= HEAD_DIM, q, zero)
    q2 = q2_ref[...]

    chunks = [(kc_ref[0], 0, lc)]
    for j in range(n_lat):
        chunks.append((k_ref[0, 0, j * tk:(j + 1) * tk, :], lc + j * tk, tk))
    m = jnp.full((2 * tq, 1), -jnp.inf, F32)
    acc = jnp.zeros((2 * tq, 2 * PAIR), F32)
    for k, off, n in chunks:
        s = lax.dot_general(q2, k, (((1,), (1,)), ((), ())), preferred_element_type=F32)
        m_new = jnp.maximum(m, jnp.max(s, axis=-1, keepdims=True))
        p = jnp.exp2(s - m_new).astype(BF16)
        acc = jnp.exp2(m - m_new) * acc + jnp.dot(p, vx_ref[off:off + n, :], preferred_element_type=F32)
        m = m_new

    o12 = acc[:, :PAIR] / acc[:, PAIR:]
    dl = dl_ref[...]
    lam_init = li_ref[...]
    lam = (jnp.exp(jnp.sum(dl[0:1] * dl[1:2], axis=-1, keepdims=True))
           - jnp.exp(jnp.sum(dl[2:3] * dl[3:4], axis=-1, keepdims=True)) + lam_init)
    o = o12[:tq] - lam * o12[tq:]
    o_ref[0] = (_rms(o, g_ref[...]) * (1.0 - lam_init)).astype(o_ref.dtype)


def _attention(cfg, q_arr, q_blk, kc_arr, kc_blk, vc_arr, vc_blk, dl, g_sub, lam_init, tq,
               k_lat=None, v_arr=None, v_blk=0):
    b, lq = q_arr.shape[0], q_arr.shape[1]
    lc = kc_arr.shape[1]
    in_specs = [pl.BlockSpec((1, tq, PAIR), lambda bi, h, i: (bi, i, q_blk + h)),
                pl.BlockSpec((1, lc, PAIR), lambda bi, h, i: (bi, 0, kc_blk + h)),
                pl.BlockSpec((1, lc, PAIR), lambda bi, h, i: (bi, 0, vc_blk + h))]
    args = [q_arr, kc_arr, vc_arr]
    n_lat = 0
    if k_lat is not None:
        s = k_lat.shape[2]
        n_lat = s // cfg.tk
        in_specs += [pl.BlockSpec((1, 1, s, PAIR), lambda bi, h, i: (1, bi, 0, h)),
                     pl.BlockSpec((1, s, PAIR), lambda bi, h, i: (bi, 0, v_blk + h))]
        args += [k_lat, v_arr]
    in_specs += [pl.BlockSpec((4, HEAD_DIM), lambda bi, h, i: (0, 0)),
                 pl.BlockSpec((1, PAIR), lambda bi, h, i: (0, 0)),
                 pl.BlockSpec((1, PAIR), lambda bi, h, i: (0, 0))]
    args += [dl, g_sub, jnp.full((1, PAIR), lam_init, F32)]
    lk = lc + n_lat * cfg.tk
    return pl.pallas_call(
        functools.partial(_attn_kernel, tq=tq, tk=cfg.tk, n_lat=n_lat, q_prescaled=k_lat is not None),
        grid=(b, HEADS, lq // tq),
        in_specs=in_specs,
        out_specs=pl.BlockSpec((1, tq, PAIR), lambda bi, h, i: (bi, i, h)),
        out_shape=jax.ShapeDtypeStruct((b, lq, QK_DIM), BF16),
        scratch_shapes=[pltpu.VMEM((2 * tq, PAIR), BF16), pltpu.VMEM((lk, 2 * PAIR), BF16)],
        compiler_params=_params("parallel", "parallel", "arbitrary"),
        name="diff_attention" if n_lat else "diff_attention_ctx",
    )(*args)


def _dft_cos_sin(n, scale=1.0):
    j = jnp.arange(n, dtype=jnp.int32)
    ang = ((j[:, None] * j[None, :]) % n).astype(F32) * (2.0 * math.pi / n)
    return jnp.cos(ang) * scale, jnp.sin(ang) * scale


def _fft_tables(seq):
    n1, n2 = FFT_N1, seq // FFT_N1
    c1, s1 = _dft_cos_sin(n1)
    stage1 = jnp.concatenate([c1, -s1], axis=0).astype(BF16)
    k1 = jnp.arange(n1, dtype=jnp.int32)
    m = jnp.arange(n2, dtype=jnp.int32)
    ang = (m[:, None] * k1[None, :]).astype(F32) * (2.0 * math.pi / seq)
    tw_cos, tw_sin = jnp.cos(ang)[:, :, None], jnp.sin(ang)[:, :, None]
    c2, s2 = _dft_cos_sin(n2)
    stage2 = jnp.concatenate([jnp.concatenate([c2, s2], axis=1),
                              jnp.concatenate([-s2, c2], axis=1)], axis=0).astype(BF16)
    cc, sc = _dft_cos_sin(GROUP_DIM, scale=(seq * GROUP_DIM) ** -0.5)
    return stage1, tw_cos, tw_sin, stage2, cc.astype(BF16), sc.astype(BF16)


def _fft_a_kernel(x_ref, w_ref, c_ref, s_ref, o_ref):
    for t in range(FFT_A_PER_STEP):
        x = x_ref[0, :, t * BRANCH_DIM:(t + 1) * BRANCH_DIM]
        a = jnp.dot(w_ref[...], x, preferred_element_type=F32)
        re, im = a[:FFT_N1], a[FFT_N1:]
        c, s = c_ref[t], s_ref[t]
        base = 2 * t * BRANCH_DIM
        o_ref[0, :, base:base + BRANCH_DIM] = (re * c + im * s).astype(o_ref.dtype)
        o_ref[0, :, base + BRANCH_DIM:base + 2 * BRANCH_DIM] = (im * c - re * s).astype(o_ref.dtype)


def _fft_b_kernel(x_ref, w_ref, cc_ref, sc_ref, o_ref, *, n2):
    for t in range(FFT_B_PER_STEP):
        x = x_ref[0, t]
        z = jnp.concatenate([x[:, :BRANCH_DIM], x[:, BRANCH_DIM:]], axis=0)
        y = jnp.dot(w_ref[...], z, preferred_element_type=F32).astype(BF16)
        for g in range(GROUPS):
            sl = slice(g * GROUP_DIM, (g + 1) * GROUP_DIM)
            o_ref[0, :, t * BRANCH_DIM + g * GROUP_DIM:t * BRANCH_DIM + (g + 1) * GROUP_DIM] = (
                jnp.dot(y[:n2, sl], cc_ref[...], preferred_element_type=F32)
                + jnp.dot(y[n2:, sl], sc_ref[...], preferred_element_type=F32)).astype(o_ref.dtype)


def _fourier_latent(cfg, p_lat, tables):
    b, s = cfg.batch, cfg.seq
    n1, n2 = FFT_N1, s // FFT_N1
    stage1, tw_cos, tw_sin, stage2, cc, sc = tables
    f = p_lat[:, cfg.off_f:cfg.off_f + BRANCH_DIM]
    pa, pb = FFT_A_PER_STEP, FFT_B_PER_STEP
    mid = pl.pallas_call(
        _fft_a_kernel,
        grid=(b, n2 // pa),
        in_specs=[pl.BlockSpec((1, n1, pa * BRANCH_DIM), lambda bi, m: (bi, 0, m)),
                  pl.BlockSpec((2 * n1, n1), lambda bi, m: (0, 0)),
                  pl.BlockSpec((pa, n1, 1), lambda bi, m: (m, 0, 0)),
                  pl.BlockSpec((pa, n1, 1), lambda bi, m: (m, 0, 0))],
        out_specs=pl.BlockSpec((1, n1, pa * 2 * BRANCH_DIM), lambda bi, m: (bi, 0, m)),
        out_shape=jax.ShapeDtypeStruct((b, n1, n2 * 2 * BRANCH_DIM), BF16),
        compiler_params=_params("parallel", "parallel"), name="fft_stage_a",
    )(f.reshape(b, n1, n2 * BRANCH_DIM), stage1, tw_cos, tw_sin)
    out = pl.pallas_call(
        functools.partial(_fft_b_kernel, n2=n2),
        grid=(b, n1 // pb),
        in_specs=[pl.BlockSpec((1, pb, n2, 2 * BRANCH_DIM), lambda bi, k: (bi, k, 0, 0)),
                  pl.BlockSpec((2 * n2, 2 * n2), lambda bi, k: (0, 0)),
                  pl.BlockSpec((GROUP_DIM, GROUP_DIM), lambda bi, k: (0, 0)),
                  pl.BlockSpec((GROUP_DIM, GROUP_DIM), lambda bi, k: (0, 0))],
        out_specs=pl.BlockSpec((1, n2, pb * BRANCH_DIM), lambda bi, k: (bi, 0, k)),
        out_shape=jax.ShapeDtypeStruct((b, n2, n1 * BRANCH_DIM), BF16),
        compiler_params=_params("parallel", "parallel"), name="fft_stage_b",
    )(mid.reshape(b, n1, n2, 2 * BRANCH_DIM), stage2, cc, sc)
    return out.reshape(b * s, BRANCH_DIM)


def _dft_dense_kernel(x_ref, c_ref, s_ref, cc_ref, sc_ref, o_ref):
    x = x_ref[0]
    yr = jnp.dot(c_ref[...], x, preferred_element_type=F32).astype(BF16)
    yi = (-jnp.dot(s_ref[...], x, preferred_element_type=F32)).astype(BF16)
    for g in range(GROUPS):
        sl = slice(g * GROUP_DIM, (g + 1) * GROUP_DIM)
        o_ref[0, :, sl] = (jnp.dot(yr[:, sl], cc_ref[...], preferred_element_type=F32)
                           + jnp.dot(yi[:, sl], sc_ref[...], preferred_element_type=F32)).astype(o_ref.dtype)


def _fourier_ctx(cfg, p_ctx):
    b, n = cfg.batch, cfg.ctx
    cn, sn = _dft_cos_sin(n)
    cc, sc = _dft_cos_sin(GROUP_DIM, scale=(n * GROUP_DIM) ** -0.5)
    fblk = cfg.off_f // BRANCH_DIM
    full = lambda bi: (0, 0)
    out = pl.pallas_call(
        _dft_dense_kernel,
        grid=(b,),
        in_specs=[pl.BlockSpec((1, n, BRANCH_DIM), lambda bi: (bi, 0, fblk)),
                  pl.BlockSpec((n, n), full), pl.BlockSpec((n, n), full),
                  pl.BlockSpec((GROUP_DIM, GROUP_DIM), full), pl.BlockSpec((GROUP_DIM, GROUP_DIM), full)],
        out_specs=pl.BlockSpec((1, n, BRANCH_DIM), lambda bi: (bi, 0, 0)),
        out_shape=jax.ShapeDtypeStruct((b, n, BRANCH_DIM), BF16),
        compiler_params=_params("parallel"), name="dft_ctx",
    )(p_ctx.reshape(b, n, cfg.n_in), cn.astype(BF16), sn.astype(BF16), cc.astype(BF16), sc.astype(BF16))
    return out.reshape(b * n, BRANCH_DIM)


def _sgu_kernel(u_ref, v_ref, g_ref, w_ref, b_ref, o_ref, *, n_chunks):
    gv = g_ref[...]
    for c in range(n_chunks):
        rows = slice(c * CHUNK, (c + 1) * CHUNK)
        u = jax.nn.gelu(u_ref[rows, :].astype(F32))
        v = jax.nn.gelu(v_ref[rows, :].astype(F32))
        vc = v - jnp.mean(v, axis=-1, keepdims=True)
        vn = (vc * lax.rsqrt(jnp.mean(vc * vc, axis=-1, keepdims=True) + EPS) * gv).astype(BF16)
        for g in range(GROUPS):
            sl = slice(g * GROUP_DIM, (g + 1) * GROUP_DIM)
            sv = jnp.dot(w_ref[g], vn[:, sl], preferred_element_type=F32) + b_ref[:, g:g + 1]
            o_ref[rows, sl] = (u[:, sl] * sv).astype(o_ref.dtype)


def _sgu(cfg, p2, sgu_w, sgu_bt, sgu_g):
    t = p2.shape[0]
    tm = min(cfg.tm_sgu, t)
    ublk, vblk = cfg.off_u // BRANCH_DIM, cfg.off_v // BRANCH_DIM
    return pl.pallas_call(
        functools.partial(_sgu_kernel, n_chunks=tm // CHUNK),
        grid=(t // tm,),
        in_specs=[pl.BlockSpec((tm, BRANCH_DIM), lambda i: (i, ublk)),
                  pl.BlockSpec((tm, BRANCH_DIM), lambda i: (i, vblk)),
                  pl.BlockSpec((1, BRANCH_DIM), lambda i: (0, 0)),
                  pl.BlockSpec((GROUPS, CHUNK, CHUNK), lambda i: (0, 0, 0)),
                  pl.BlockSpec((CHUNK, GROUPS), lambda i: (0, 0))],
        out_specs=pl.BlockSpec((tm, BRANCH_DIM), lambda i: (i, 0)),
        out_shape=jax.ShapeDtypeStruct((t, BRANCH_DIM), BF16),
        compiler_params=_params("parallel"), name="spatial_gating",
    )(p2, p2, sgu_g, sgu_w, sgu_bt)


def _merge_kernel(pg0_ref, pg1_ref, pg2_ref, f_ref, s_ref, a_ref, x_ref, bg_ref,
                  wf_ref, ws_ref, wa_ref, wo_ref, g_ref, mod_ref, o_ref, *, d):
    def gate(pg_ref, k):
        return jax.nn.sigmoid(pg_ref[...].astype(F32) + bg_ref[:, k * d:(k + 1) * d])

    merged = gate(pg0_ref, 0) * jnp.dot(f_ref[...], wf_ref[...], preferred_element_type=F32)
    merged += gate(pg1_ref, 1) * jnp.dot(s_ref[...], ws_ref[...], preferred_element_type=F32)
    merged += gate(pg2_ref, 2) * jnp.dot(a_ref[...], wa_ref[...], preferred_element_type=F32)
    y = jnp.dot(merged.astype(BF16), wo_ref[...], preferred_element_type=F32)
    o_ref[...] = x_ref[...] + mod_ref[0][2:3] * _rms(y, g_ref[...])


def _resident(shape):
    return pl.BlockSpec(shape, lambda *_: (0,) * len(shape), pipeline_mode=pl.Buffered(1))


def _merge(cfg, p2, four, sgu, attn, x, b_gate, wf, ws, wa, wo, g, mod, rows_per_mod):
    t, d = x.shape
    tm = cfg.tm_merge
    row = lambda i: (i, 0)
    return pl.pallas_call(
        functools.partial(_merge_kernel, d=d),
        grid=(t // tm,),
        in_specs=[pl.BlockSpec((tm, d), lambda i: (i, 0)),
                  pl.BlockSpec((tm, d), lambda i: (i, 1)),
                  pl.BlockSpec((tm, d), lambda i: (i, 2)),
                  pl.BlockSpec((tm, BRANCH_DIM), row), pl.BlockSpec((tm, BRANCH_DIM), row),
                  pl.BlockSpec((tm, QK_DIM), row), pl.BlockSpec((tm, d), row),
                  _resident((1, 3 * d)), _resident((BRANCH_DIM, d)), _resident((BRANCH_DIM, d)),
                  _resident((QK_DIM, d)), _resident((d, d)), _resident((1, d)),
                  pl.BlockSpec((1, 6, d), lambda i: (i * tm // rows_per_mod, 0, 0))],
        out_specs=pl.BlockSpec((tm, d), row),
        out_shape=jax.ShapeDtypeStruct((t, d), F32),
        compiler_params=_params("parallel"), name="merge",
    )(p2, p2, p2, four, sgu, attn, x, b_gate, wf, ws, wa, wo, g, mod)


def _ffn_up_kernel(eid_ref, nact_ref, a_ref, w1_ref, w3_ref, o_ref, w1b_ref, w3b_ref):
    i = pl.program_id(1)
    active = i < nact_ref[0]
    fresh = jnp.logical_or(i == 0, eid_ref[i] != eid_ref[jnp.maximum(i - 1, 0)])

    @pl.when(jnp.logical_and(active, fresh))
    def _():
        w1b_ref[...] = w1_ref[0].astype(BF16)
        w3b_ref[...] = w3_ref[0].astype(BF16)

    @pl.when(active)
    def _():
        a = a_ref[...]
        h1 = jnp.dot(a, w1b_ref[...], preferred_element_type=F32)
        h3 = jnp.dot(a, w3b_ref[...], preferred_element_type=F32)
        o_ref[...] = (jax.nn.silu(h1) * h3).astype(o_ref.dtype)

    @pl.when(jnp.logical_not(active))
    def _():
        o_ref[...] = jnp.zeros(o_ref.shape, o_ref.dtype)


def _ffn_up(cfg, a, w1, w3, tile_eid, n_active, tm):
    r, d = a.shape
    dff = w1.shape[2]
    tn = cfg.tn_up
    grid_spec = pltpu.PrefetchScalarGridSpec(
        num_scalar_prefetch=2,
        grid=(dff // tn, r // tm),
        in_specs=[pl.BlockSpec((tm, d), lambda j, i, eid, na: (jnp.minimum(i, na[0] - 1), 0)),
                  pl.BlockSpec((1, d, tn), lambda j, i, eid, na: (eid[i], 0, j)),
                  pl.BlockSpec((1, d, tn), lambda j, i, eid, na: (eid[i], 0, j))],
        out_specs=pl.BlockSpec((tm, tn), lambda j, i, eid, na: (i, j)),
        scratch_shapes=[pltpu.VMEM((d, tn), BF16), pltpu.VMEM((d, tn), BF16)],
    )
    return pl.pallas_call(
        _ffn_up_kernel, grid_spec=grid_spec,
        out_shape=jax.ShapeDtypeStruct((r, dff), BF16),
        compiler_params=_params("arbitrary", "arbitrary"), name="swiglu_up",
    )(tile_eid, n_active, a, w1, w3)


def _ffn_down_kernel(eid_ref, nact_ref, *refs, residual):
    if residual:
        h_ref, w_ref, x_ref, g_ref, mod_ref, o_ref, acc_ref = refs
    else:
        h_ref, w_ref, o_ref, acc_ref = refs
    k = pl.program_id(1)

    @pl.when(k == 0)
    def _():
        acc_ref[...] = jnp.zeros(acc_ref.shape, F32)

    @pl.when(pl.program_id(0) < nact_ref[0])
    def _():
        acc_ref[...] += jnp.dot(h_ref[...], w_ref[0], preferred_element_type=F32)

    @pl.when(k == pl.num_programs(1) - 1)
    def _():
        if residual:
            o_ref[...] = x_ref[...] + mod_ref[0][5:6] * _rms(acc_ref[...], g_ref[...])
        else:
            o_ref[...] = acc_ref[...].astype(o_ref.dtype)


def _ffn_down(cfg, h, w2, tile_eid, n_active, tm, tk, x=None, g=None, mod=None, rows_per_mod=None):
    r, dff = h.shape
    d = w2.shape[2]
    residual = x is not None
    in_specs = [pl.BlockSpec((tm, tk), lambda i, k, eid, na: (jnp.minimum(i, na[0] - 1), k)),
                pl.BlockSpec((1, tk, d), lambda i, k, eid, na: (eid[i], k, 0))]
    args = [h, w2]
    if residual:
        in_specs += [pl.BlockSpec((tm, d), lambda i, k, eid, na: (i, 0)),
                     pl.BlockSpec((1, d), lambda i, k, eid, na: (0, 0)),
                     pl.BlockSpec((1, 6, d), lambda i, k, eid, na: (i * tm // rows_per_mod, 0, 0))]
        args += [x, g, mod]
    grid_spec = pltpu.PrefetchScalarGridSpec(
        num_scalar_prefetch=2,
        grid=(r // tm, dff // tk),
        in_specs=in_specs,
        out_specs=pl.BlockSpec((tm, d), lambda i, k, eid, na: (i, 0)),
        scratch_shapes=[pltpu.VMEM((tm, d), F32)],
    )
    return pl.pallas_call(
        functools.partial(_ffn_down_kernel, residual=residual), grid_spec=grid_spec,
        out_shape=jax.ShapeDtypeStruct((r, d), F32 if residual else BF16),
        compiler_params=_params("parallel", "arbitrary"),
        name="swiglu_down_residual" if residual else "swiglu_down",
    )(tile_eid, n_active, *args)


def _combine_kernel(y1_ref, y2_ref, wt_ref, x_ref, g_ref, mod_ref, o_ref):
    wt = wt_ref[...]
    y = wt[:, 0:1] * y1_ref[...].astype(F32) + wt[:, 1:2] * y2_ref[...].astype(F32)
    o_ref[...] = x_ref[...] + mod_ref[0][5:6] * _rms(y, g_ref[...])


def _combine(cfg, y1, y2, wt, x, g, mod, rows_per_mod):
    t, d = x.shape
    tm = min(cfg.tm_norm, t)
    row = lambda i: (i, 0)
    return pl.pallas_call(
        _combine_kernel,
        grid=(t // tm,),
        in_specs=[pl.BlockSpec((tm, d), row), pl.BlockSpec((tm, d), row),
                  pl.BlockSpec((tm, TOP_K), row), pl.BlockSpec((tm, d), row),
                  pl.BlockSpec((1, d), lambda i: (0, 0)),
                  pl.BlockSpec((1, 6, d), lambda i: (i * tm // rows_per_mod, 0, 0))],
        out_specs=pl.BlockSpec((tm, d), row),
        out_shape=jax.ShapeDtypeStruct((t, d), F32),
        compiler_params=_params("parallel"), name="moe_combine",
    )(y1, y2, wt, x, g, mod)


def _dispatch_plan(top_i, tm):
    t = top_i.shape[1]
    n_pairs = TOP_K * t
    n_rows = n_pairs + N_EXPERTS * tm
    e_flat = top_i.reshape(n_pairs)
    onehot = (e_flat[:, None] == jnp.arange(N_EXPERTS, dtype=jnp.int32)[None, :]).astype(jnp.int32)
    rank = jnp.sum(jnp.cumsum(onehot, axis=0) * onehot, axis=1) - 1
    counts = jnp.sum(onehot, axis=0)
    padded = (counts + tm - 1) // tm * tm
    ends = jnp.cumsum(padded)
    starts = ends - padded
    dest = starts[e_flat] + rank
    tok = jnp.tile(jnp.arange(t, dtype=jnp.int32), TOP_K)
    row_tok = jnp.zeros((n_rows,), jnp.int32).at[dest].set(tok)
    tile_start = jnp.arange(n_rows // tm, dtype=jnp.int32) * tm
    tile_eid = jnp.minimum(jnp.sum((tile_start[:, None] >= ends[None, :]).astype(jnp.int32), axis=1),
                           N_EXPERTS - 1)
    n_active = (ends[-1] // tm).astype(jnp.int32).reshape(1)
    return row_tok, dest.reshape(TOP_K, t), tile_eid, n_active


def _dense_plan(t, tm):
    return jnp.zeros((t // tm,), jnp.int32), jnp.full((1,), t // tm, jnp.int32)


def _permute_w_in(w):
    b3 = 3 * BRANCH_DIM
    q3 = 3 * QK_DIM
    return jnp.concatenate([w[:, b3 + q3:], w[:, b3:b3 + q3], w[:, :b3]], axis=1).astype(BF16)


def _forward(cfg, x, c, ctx, c_ctx, w_mod, b_mod, g_norm, w_in, b_gate, w_fourier_out, w_sgu_out,
             w_attn_out, w_o, sgu_w, sgu_b, sgu_g, diff_lambda, diff_subln_g,
             ffn_w1, ffn_w3, ffn_w2, router_w, moe_w1, moe_w3, moe_w2):
    b, s, d = x.shape
    n_ctx = ctx.shape[1]
    t_lat, t_ctx = b * s, b * n_ctx
    cos, sin = _rope_tables(s)
    fft_tables = _fft_tables(s)

    c_rows = jnp.concatenate([c, c_ctx[None, :], jnp.zeros((8 - b - 1, d), F32)], axis=0)
    mod_all = _modulation(cfg, c_rows, w_mod, b_mod)

    xl = x.reshape(t_lat, d)
    xc = ctx.reshape(t_ctx, d)
    for l in range(cfg.depth):
        last = l == cfg.depth - 1
        lam_init = 0.8 - 0.6 * math.exp(-0.3 * l)
        mod_l = mod_all[l, :b].reshape(b, 6, d)
        mod_c = mod_all[l, b:b + 1].reshape(1, 6, d)
        g = g_norm[l].reshape(4, 1, d)
        w_in_l = _permute_w_in(w_in[l])
        wf, ws = w_fourier_out[l].astype(BF16), w_sgu_out[l].astype(BF16)
        wa, wo = w_attn_out[l].astype(BF16), w_o[l].astype(BF16)
        bg = b_gate[l].reshape(1, 3 * d)
        sw = sgu_w[l].astype(BF16)
        sbt = sgu_b[l].T
        sg = sgu_g[l].reshape(1, BRANCH_DIM)
        dl = diff_lambda[l]
        gsub = diff_subln_g[l].reshape(1, PAIR)
        pair_blk = lambda off: off // PAIR

        hl = _norm_mod(cfg, xl, g[0], mod_l, s, 0, 1)
        hc = _norm_mod(cfg, xc, g[0], mod_c, t_ctx, 0, 1)
        p_lat = _matmul(cfg, hl, w_in_l)
        p_lat3 = p_lat.reshape(b, s, cfg.n_in)
        qk_rot = _rope(cfg, p_lat3, cos, sin)
        if last:
            p_ctx3 = _matmul(cfg, hc, w_in_l[:, cfg.off_k:cfg.off_f]).reshape(b, n_ctx, 2 * QK_DIM)
            kc_blk, vc_blk = 0, pair_blk(QK_DIM)
        else:
            p_ctx = _matmul(cfg, hc, w_in_l)
            p_ctx3 = p_ctx.reshape(b, n_ctx, cfg.n_in)
            kc_blk, vc_blk = pair_blk(cfg.off_k), pair_blk(cfg.off_va)
        al = _attention(cfg, qk_rot[0], 0, p_ctx3, kc_blk, p_ctx3, vc_blk, dl, gsub, lam_init, cfg.tq,
                        k_lat=qk_rot, v_arr=p_lat3, v_blk=pair_blk(cfg.off_va))
        four_l = _fourier_latent(cfg, p_lat, fft_tables)
        sgu_l = _sgu(cfg, p_lat, sw, sbt, sg)
        xl = _merge(cfg, p_lat, four_l, sgu_l, al.reshape(t_lat, QK_DIM), xl, bg, wf, ws, wa, wo, g[1], mod_l, s)
        if not last:
            ac = _attention(cfg, p_ctx3, pair_blk(cfg.off_q), p_ctx3, kc_blk, p_ctx3, vc_blk, dl, gsub,
                            lam_init, n_ctx)
            four_c = _fourier_ctx(cfg, p_ctx)
            sgu_c = _sgu(cfg, p_ctx, sw, sbt, sg)
            xc = _merge(cfg, p_ctx, four_c, sgu_c, ac.reshape(t_ctx, QK_DIM), xc, bg, wf, ws, wa, wo, g[1],
                        mod_c, t_ctx)

        i = l // 2
        streams = [(xl, mod_l, s)] + ([] if last else [(xc, mod_c, t_ctx)])
        outs = []
        if l % 2 == 0:
            w2 = ffn_w2[i][None].astype(BF16)
            for xs, mod, rpm in streams:
                t = xs.shape[0]
                fl = _norm_mod(cfg, xs, g[2], mod, rpm, 3, 4)
                tm_up = min(cfg.tm_up, t)
                eid_up, n_up = _dense_plan(t, tm_up)
                h = _ffn_up(cfg, fl, ffn_w1, ffn_w3, eid_up + i, n_up, tm_up)
                tm_dn = min(cfg.tm_down_dense, t)
                outs.append(_ffn_down(cfg, h, w2, *_dense_plan(t, tm_dn), tm_dn, cfg.tk_down_dense, x=xs,
                                      g=g[3], mod=mod, rows_per_mod=rpm))
        else:
            w1 = moe_w1.reshape(-1, d, cfg.dff)
            w3 = moe_w3.reshape(-1, d, cfg.dff)
            w2 = moe_w2[i].astype(BF16)
            rwt = router_w[i].T
            tm = cfg.tm_down
            for xs, mod, rpm in streams:
                fl, top_i, top_w = _norm_mod(cfg, xs, g[2], mod, rpm, 3, 4, router_wt=rwt)
                row_tok, pos, tile_eid, n_active = _dispatch_plan(top_i[:TOP_K], tm)
                a = jnp.take(fl, row_tok, axis=0)
                h = _ffn_up(cfg, a, w1, w3, tile_eid + i * N_EXPERTS, n_active, tm)
                y = _ffn_down(cfg, h, w2, tile_eid, n_active, tm, cfg.tk_down)
                outs.append(_combine(cfg, jnp.take(y, pos[0], axis=0), jnp.take(y, pos[1], axis=0),
                                     top_w[:TOP_K].T, xs, g[3], mod, rpm))
        xl = outs[0]
        if not last:
            xc = outs[1]
    return xl.reshape(b, s, d)


def kernel(x, c, ctx, c_ctx, w_mod, b_mod, g_norm, w_in, b_gate, w_fourier_out, w_sgu_out, w_attn_out, w_o,
           sgu_w, sgu_b, sgu_g, diff_lambda, diff_subln_g, ffn_w1, ffn_w3, ffn_w2, router_w,
           moe_w1, moe_w3, moe_w2):
    return _forward(PROD, x, c, ctx, c_ctx, w_mod, b_mod, g_norm, w_in, b_gate, w_fourier_out, w_sgu_out,
                    w_attn_out, w_o, sgu_w, sgu_b, sgu_g, diff_lambda, diff_subln_g,
                    ffn_w1, ffn_w3, ffn_w2, router_w, moe_w1, moe_w3, moe_w2)
```

```python
import functools
import math
from typing import NamedTuple

import jax
import jax.numpy as jnp
from jax import lax
from jax.experimental import pallas as pl
from jax.experimental.pallas import tpu as pltpu

F32 = jnp.float32
BF16 = jnp.bfloat16

EPS = 1e-6
GRID_W = 64
ROPE_BASE = 10000.0
HEAD_DIM = 64
HEADS = 8
PAIR = 2 * HEAD_DIM
QK_DIM = HEADS * PAIR
GROUP_DIM = 128
GROUPS = 4
BRANCH_DIM = GROUPS * GROUP_DIM
CHUNK = 128
FFT_N1 = 128
FFT_A_PER_STEP = 4
FFT_B_PER_STEP = 8
N_EXPERTS = 8
TOP_K = 2
EXPERT_ROWS = 8
ATTN_SCALE = HEAD_DIM ** -0.5
Q_SCALE = ATTN_SCALE * math.log2(math.e)
VMEM_LIMIT = 56 * 1024 * 1024


class Cfg(NamedTuple):
    d: int
    batch: int
    seq: int
    ctx: int
    dff: int
    depth: int
    tm_norm: int
    tm_mm: int
    tn_mm: int
    tm_rope: int
    tq: int
    tk: int
    tm_sgu: int
    tm_merge: int
    tm_up: int
    tn_up: int
    tm_down: int
    tk_down: int
    tm_down_ctx: int
    tm_down_dense: int
    tk_down_dense: int
    tn_mod: int

    @property
    def n_in(self):
        return 3 * self.d + 3 * QK_DIM + 3 * BRANCH_DIM

    @property
    def off_q(self):
        return 3 * self.d

    @property
    def off_k(self):
        return self.off_q + QK_DIM

    @property
    def off_va(self):
        return self.off_k + QK_DIM

    @property
    def off_f(self):
        return self.off_va + QK_DIM

    @property
    def off_u(self):
        return self.off_f + BRANCH_DIM

    @property
    def off_v(self):
        return self.off_u + BRANCH_DIM


PROD = Cfg(d=2048, batch=4, seq=8192, ctx=256, dff=5632, depth=4,
           tm_norm=512, tm_mm=1024, tn_mm=1536, tm_rope=512, tq=512, tk=256,
           tm_sgu=1024, tm_merge=256, tm_up=1024, tn_up=512, tm_down=1024, tk_down=1408,
           tm_down_ctx=256, tm_down_dense=1024, tk_down_dense=512, tn_mod=1024)


def _params(*sem):
    return pltpu.CompilerParams(dimension_semantics=sem, vmem_limit_bytes=VMEM_LIMIT)


def _rms(y, g):
    return y * lax.rsqrt(jnp.mean(y * y, axis=-1, keepdims=True) + EPS) * g


def _mod_kernel(c_ref, w_ref, b_ref, o_ref):
    sc = jax.nn.silu(c_ref[...])
    o_ref[0] = jnp.dot(sc.astype(BF16), w_ref[0].astype(BF16), preferred_element_type=F32) + b_ref[0]


def _modulation(cfg, c_rows, w_mod, b_mod):
    depth, d, n6 = w_mod.shape
    tn = cfg.tn_mod
    return pl.pallas_call(
        _mod_kernel,
        grid=(depth, n6 // tn),
        in_specs=[pl.BlockSpec((8, d), lambda l, j: (0, 0)),
                  pl.BlockSpec((1, d, tn), lambda l, j: (l, 0, j)),
                  pl.BlockSpec((1, 1, tn), lambda l, j: (l, 0, j))],
        out_specs=pl.BlockSpec((1, 8, tn), lambda l, j: (l, 0, j)),
        out_shape=jax.ShapeDtypeStruct((depth, 8, n6), F32),
        compiler_params=_params("parallel", "parallel"),
        name="modulation",
    )(c_rows, w_mod, b_mod.reshape(depth, 1, n6))


def _norm_mod_kernel(x_ref, g_ref, mod_ref, o_ref, *, shift_idx, scale_idx):
    y = _rms(x_ref[...], g_ref[...])
    m = mod_ref[0]
    o_ref[...] = (y * (1.0 + m[scale_idx:scale_idx + 1]) + m[shift_idx:shift_idx + 1]).astype(o_ref.dtype)


def _norm_mod_router_kernel(x_ref, g_ref, mod_ref, rw_ref, o_ref, idx_ref, wt_ref, *, shift_idx, scale_idx):
    y = _rms(x_ref[...], g_ref[...])
    m = mod_ref[0]
    h = y * (1.0 + m[scale_idx:scale_idx + 1]) + m[shift_idx:shift_idx + 1]
    o_ref[...] = h.astype(o_ref.dtype)
    logits = lax.dot_general(rw_ref[...], h, (((1,), (1,)), ((), ())),
                             precision=lax.Precision.HIGHEST, preferred_element_type=F32)
    row = lax.broadcasted_iota(jnp.int32, logits.shape, 0).astype(F32)
    m1 = jnp.max(logits, axis=0, keepdims=True)
    i1 = jnp.min(jnp.where(logits == m1, row, float(N_EXPERTS)), axis=0, keepdims=True)
    rest = jnp.where(row == i1, -jnp.inf, logits)
    m2 = jnp.max(rest, axis=0, keepdims=True)
    i2 = jnp.min(jnp.where(rest == m2, row, float(N_EXPERTS)), axis=0, keepdims=True)
    e2 = jnp.exp(m2 - m1)
    w1 = 1.0 / (1.0 + e2)
    w2 = e2 / (1.0 + e2)
    idx_ref[...] = jnp.where(row == 0.0, i1, jnp.where(row == 1.0, i2, 0.0)).astype(jnp.int32)
    wt_ref[...] = jnp.where(row == 0.0, w1, jnp.where(row == 1.0, w2, 0.0))


def _norm_mod(cfg, x, g, mod, rows_per_mod, shift_idx, scale_idx, router_wt=None):
    t, d = x.shape
    tm = min(cfg.tm_norm, t)
    in_specs = [pl.BlockSpec((tm, d), lambda i: (i, 0)),
                pl.BlockSpec((1, d), lambda i: (0, 0)),
                pl.BlockSpec((1, 6, d), lambda i: (i * tm // rows_per_mod, 0, 0))]
    h_spec = pl.BlockSpec((tm, d), lambda i: (i, 0))
    h_shape = jax.ShapeDtypeStruct((t, d), BF16)
    if router_wt is None:
        return pl.pallas_call(
            functools.partial(_norm_mod_kernel, shift_idx=shift_idx, scale_idx=scale_idx),
            grid=(t // tm,), in_specs=in_specs, out_specs=h_spec, out_shape=h_shape,
            compiler_params=_params("parallel"), name="norm_mod",
        )(x, g, mod)
    r_spec = pl.BlockSpec((EXPERT_ROWS, tm), lambda i: (0, i))
    return pl.pallas_call(
        functools.partial(_norm_mod_router_kernel, shift_idx=shift_idx, scale_idx=scale_idx),
        grid=(t // tm,),
        in_specs=in_specs + [pl.BlockSpec((N_EXPERTS, d), lambda i: (0, 0))],
        out_specs=[h_spec, r_spec, r_spec],
        out_shape=[h_shape, jax.ShapeDtypeStruct((EXPERT_ROWS, t), jnp.int32),
                   jax.ShapeDtypeStruct((EXPERT_ROWS, t), F32)],
        compiler_params=_params("parallel"), name="norm_mod_router",
    )(x, g, mod, router_wt)


def _in_proj_kernel(x_ref, g_ref, mod_ref, w_ref, o_ref, h_ref):
    @pl.when(pl.program_id(1) == 0)
    def _():
        m = mod_ref[0]
        h_ref[...] = (_rms(x_ref[...], g_ref[...]) * (1.0 + m[1:2]) + m[0:1]).astype(h_ref.dtype)

    o_ref[...] = jnp.dot(h_ref[...], w_ref[...], preferred_element_type=F32).astype(o_ref.dtype)


def _in_proj(cfg, x, g, mod, rows_per_mod, w):
    t, d = x.shape
    n = w.shape[1]
    tm = min(cfg.tm_mm, t)
    tn = cfg.tn_mm if n % cfg.tn_mm == 0 else 512
    return pl.pallas_call(
        _in_proj_kernel,
        grid=(t // tm, n // tn),
        in_specs=[pl.BlockSpec((tm, d), lambda i, j: (i, 0)),
                  pl.BlockSpec((1, d), lambda i, j: (0, 0)),
                  pl.BlockSpec((1, 6, d), lambda i, j: (i * tm // rows_per_mod, 0, 0)),
                  pl.BlockSpec((d, tn), lambda i, j: (0, j))],
        out_specs=pl.BlockSpec((tm, tn), lambda i, j: (i, j)),
        out_shape=jax.ShapeDtypeStruct((t, n), BF16),
        scratch_shapes=[pltpu.VMEM((tm, d), BF16)],
        compiler_params=_params("parallel", "arbitrary"), name="in_proj",
    )(x, g, mod, w)


def _rope_tables(seq):
    rows = seq // GRID_W
    r = jnp.repeat(jnp.arange(rows, dtype=F32), GRID_W)
    col = jnp.tile(jnp.arange(GRID_W, dtype=F32), rows)
    n_freq = HEAD_DIM // 4
    inv = ROPE_BASE ** (-jnp.arange(n_freq, dtype=F32) / n_freq)
    ang = jnp.concatenate([r[:, None] * inv, col[:, None] * inv], axis=-1)
    cos, sin = jnp.cos(ang), jnp.sin(ang)
    return jnp.tile(cos, (1, 4)), jnp.tile(jnp.concatenate([-sin, sin], axis=-1), (1, 2))


def _rope_kernel(p_ref, cos_ref, sin_ref, o_ref):
    scale = jnp.where(pl.program_id(2) == 0, Q_SCALE, 1.0).astype(F32)
    cos = cos_ref[...] * scale
    sin = sin_ref[...] * scale
    lane = lax.broadcasted_iota(jnp.int32, cos.shape, 1)
    first_half = (lane % HEAD_DIM) < (HEAD_DIM // 2)
    for c in range(QK_DIM // PAIR):
        t = p_ref[0, :, c * PAIR:(c + 1) * PAIR].astype(F32)
        partner = jnp.where(first_half, pltpu.roll(t, PAIR - HEAD_DIM // 2, 1), pltpu.roll(t, HEAD_DIM // 2, 1))
        o_ref[0, 0, :, c * PAIR:(c + 1) * PAIR] = (t * cos + partner * sin).astype(o_ref.dtype)


def _rope(cfg, p3, cos, sin):
    b, s, _ = p3.shape
    tm = cfg.tm_rope
    qblk = cfg.off_q // QK_DIM
    return pl.pallas_call(
        _rope_kernel,
        grid=(b, s // tm, 2),
        in_specs=[pl.BlockSpec((1, tm, QK_DIM), lambda bi, i, j: (bi, i, qblk + j)),
                  pl.BlockSpec((tm, PAIR), lambda bi, i, j: (i, 0)),
                  pl.BlockSpec((tm, PAIR), lambda bi, i, j: (i, 0))],
        out_specs=pl.BlockSpec((1, 1, tm, QK_DIM), lambda bi, i, j: (j, bi, i, 0)),
        out_shape=jax.ShapeDtypeStruct((2, b, s, QK_DIM), BF16),
        compiler_params=_params("parallel", "parallel", "parallel"), name="rope",
    )(p3, cos, sin)


def _attn_kernel(*refs, tq, tk, n_lat, q_prescaled):
    if n_lat:
        q_ref, kc_ref, vc_ref, k_ref, v_ref, dl_ref, g_ref, li_ref, o_ref, q2_ref, vx_ref = refs
    else:
        q_ref, kc_ref, vc_ref, dl_ref, g_ref, li_ref, o_ref, q2_ref, vx_ref = refs
    lc = kc_ref.shape[1]

    @pl.when(pl.program_id(2) == 0)
    def _():
        vx_ref[:lc, :PAIR] = vc_ref[0]
        vx_ref[:lc, PAIR:] = jnp.ones((lc, PAIR), BF16)
        if n_lat:
            vx_ref[lc:, :PAIR] = v_ref[0]
            vx_ref[lc:, PAIR:] = jnp.ones((n_lat * tk, PAIR), BF16)

    q = q_ref[0]
    if not q_prescaled:
        q = (q.astype(F32) * Q_SCALE).astype(BF16)
    lane = lax.broadcasted_iota(jnp.int32, q.shape, 1)
    zero = jnp.zeros_like(q)
    q2_ref[:tq] = jnp.where(lane < HEAD_DIM, q, zero)
    q2_ref[tq:] = jnp.where(lane >= HEAD_DIM, q, zero)
    q2 = q2_ref[...]

    chunks = [(kc_ref[0], 0, lc)]
    for j in range(n_lat):
        chunks.append((k_ref[0, 0, j * tk:(j + 1) * tk, :], lc + j * tk, tk))
    m = jnp.full((2 * tq, 1), -jnp.inf, F32)
    acc = jnp.zeros((2 * tq, 2 * PAIR), F32)
    for k, off, n in chunks:
        s = lax.dot_general(q2, k, (((1,), (1,)), ((), ())), preferred_element_type=F32)
        m_new = jnp.maximum(m, jnp.max(s, axis=-1, keepdims=True))
        p = jnp.exp2(s - m_new).astype(BF16)
        acc = jnp.exp2(m - m_new) * acc + jnp.dot(p, vx_ref[off:off + n, :], preferred_element_type=F32)
        m = m_new

    o12 = acc[:, :PAIR] / acc[:, PAIR:]
    dl = dl_ref[...]
    lam_init = li_ref[...]
    lam = (jnp.exp(jnp.sum(dl[0:1] * dl[1:2], axis=-1, keepdims=True))
           - jnp.exp(jnp.sum(dl[2:3] * dl[3:4], axis=-1, keepdims=True)) + lam_init)
    o = o12[:tq] - lam * o12[tq:]
    o_ref[0] = (_rms(o, g_ref[...]) * (1.0 - lam_init)).astype(o_ref.dtype)


def _attention(cfg, q_arr, q_blk, kc_arr, kc_blk, vc_arr, vc_blk, dl, g_sub, lam_init, tq,
               k_lat=None, v_arr=None, v_blk=0):
    b, lq = q_arr.shape[0], q_arr.shape[1]
    lc = kc_arr.shape[1]
    in_specs = [pl.BlockSpec((1, tq, PAIR), lambda bi, h, i: (bi, i, q_blk + h)),
                pl.BlockSpec((1, lc, PAIR), lambda bi, h, i: (bi, 0, kc_blk + h)),
                pl.BlockSpec((1, lc, PAIR), lambda bi, h, i: (bi, 0, vc_blk + h))]
    args = [q_arr, kc_arr, vc_arr]
    n_lat = 0
    if k_lat is not None:
        s = k_lat.shape[2]
        n_lat = s // cfg.tk
        in_specs += [pl.BlockSpec((1, 1, s, PAIR), lambda bi, h, i: (1, bi, 0, h)),
                     pl.BlockSpec((1, s, PAIR), lambda bi, h, i: (bi, 0, v_blk + h))]
        args += [k_lat, v_arr]
    in_specs += [pl.BlockSpec((4, HEAD_DIM), lambda bi, h, i: (0, 0)),
                 pl.BlockSpec((1, PAIR), lambda bi, h, i: (0, 0)),
                 pl.BlockSpec((1, PAIR), lambda bi, h, i: (0, 0))]
    args += [dl, g_sub, jnp.full((1, PAIR), lam_init, F32)]
    lk = lc + n_lat * cfg.tk
    return pl.pallas_call(
        functools.partial(_attn_kernel, tq=tq, tk=cfg.tk, n_lat=n_lat, q_prescaled=k_lat is not None),
        grid=(b, HEADS, lq // tq),
        in_specs=in_specs,
        out_specs=pl.BlockSpec((1, tq, PAIR), lambda bi, h, i: (bi, i, h)),
        out_shape=jax.ShapeDtypeStruct((b, lq, QK_DIM), BF16),
        scratch_shapes=[pltpu.VMEM((2 * tq, PAIR), BF16), pltpu.VMEM((lk, 2 * PAIR), BF16)],
        compiler_params=_params("parallel", "parallel", "arbitrary"),
        name="diff_attention" if n_lat else "diff_attention_ctx",
    )(*args)


def _dft_cos_sin(n, scale=1.0):
    j = jnp.arange(n, dtype=jnp.int32)
    ang = ((j[:, None] * j[None, :]) % n).astype(F32) * (2.0 * math.pi / n)
    return jnp.cos(ang) * scale, jnp.sin(ang) * scale


def _fft_tables(seq):
    n1, n2 = FFT_N1, seq // FFT_N1
    c1, s1 = _dft_cos_sin(n1)
    stage1 = jnp.concatenate([c1, -s1], axis=0).astype(BF16)
    k1 = jnp.arange(n1, dtype=jnp.int32)
    m = jnp.arange(n2, dtype=jnp.int32)
    ang = (m[:, None] * k1[None, :]).astype(F32) * (2.0 * math.pi / seq)
    tw_cos, tw_sin = jnp.cos(ang)[:, :, None], jnp.sin(ang)[:, :, None]
    c2, s2 = _dft_cos_sin(n2)
    stage2 = jnp.concatenate([jnp.concatenate([c2, s2], axis=1),
                              jnp.concatenate([-s2, c2], axis=1)], axis=0).astype(BF16)
    cc, sc = _dft_cos_sin(GROUP_DIM, scale=(seq * GROUP_DIM) ** -0.5)
    return stage1, tw_cos, tw_sin, stage2, cc.astype(BF16), sc.astype(BF16)


def _fft_a_kernel(x_ref, w_ref, c_ref, s_ref, o_ref):
    for t in range(FFT_A_PER_STEP):
        x = x_ref[0, :, t * BRANCH_DIM:(t + 1) * BRANCH_DIM]
        a = jnp.dot(w_ref[...], x, preferred_element_type=F32)
        re, im = a[:FFT_N1], a[FFT_N1:]
        c, s = c_ref[t], s_ref[t]
        base = 2 * t * BRANCH_DIM
        o_ref[0, :, base:base + BRANCH_DIM] = (re * c + im * s).astype(o_ref.dtype)
        o_ref[0, :, base + BRANCH_DIM:base + 2 * BRANCH_DIM] = (im * c - re * s).astype(o_ref.dtype)


def _fft_b_kernel(x_ref, w_ref, cc_ref, sc_ref, o_ref, *, n2):
    for t in range(FFT_B_PER_STEP):
        x = x_ref[0, t]
        z = jnp.concatenate([x[:, :BRANCH_DIM], x[:, BRANCH_DIM:]], axis=0)
        y = jnp.dot(w_ref[...], z, preferred_element_type=F32).astype(BF16)
        for g in range(GROUPS):
            sl = slice(g * GROUP_DIM, (g + 1) * GROUP_DIM)
            o_ref[0, :, t * BRANCH_DIM + g * GROUP_DIM:t * BRANCH_DIM + (g + 1) * GROUP_DIM] = (
                jnp.dot(y[:n2, sl], cc_ref[...], preferred_element_type=F32)
                + jnp.dot(y[n2:, sl], sc_ref[...], preferred_element_type=F32)).astype(o_ref.dtype)


def _fourier_latent(cfg, p_lat, tables):
    b, s = cfg.batch, cfg.seq
    n1, n2 = FFT_N1, s // FFT_N1
    stage1, tw_cos, tw_sin, stage2, cc, sc = tables
    f = p_lat[:, cfg.off_f:cfg.off_f + BRANCH_DIM]
    pa, pb = FFT_A_PER_STEP, FFT_B_PER_STEP
    mid = pl.pallas_call(
        _fft_a_kernel,
        grid=(b, n2 // pa),
        in_specs=[pl.BlockSpec((1, n1, pa * BRANCH_DIM), lambda bi, m: (bi, 0, m)),
                  pl.BlockSpec((2 * n1, n1), lambda bi, m: (0, 0)),
                  pl.BlockSpec((pa, n1, 1), lambda bi, m: (m, 0, 0)),
                  pl.BlockSpec((pa, n1, 1), lambda bi, m: (m, 0, 0))],
        out_specs=pl.BlockSpec((1, n1, pa * 2 * BRANCH_DIM), lambda bi, m: (bi, 0, m)),
        out_shape=jax.ShapeDtypeStruct((b, n1, n2 * 2 * BRANCH_DIM), BF16),
        compiler_params=_params("parallel", "parallel"), name="fft_stage_a",
    )(f.reshape(b, n1, n2 * BRANCH_DIM), stage1, tw_cos, tw_sin)
    out = pl.pallas_call(
        functools.partial(_fft_b_kernel, n2=n2),
        grid=(b, n1 // pb),
        in_specs=[pl.BlockSpec((1, pb, n2, 2 * BRANCH_DIM), lambda bi, k: (bi, k, 0, 0)),
                  pl.BlockSpec((2 * n2, 2 * n2), lambda bi, k: (0, 0)),
                  pl.BlockSpec((GROUP_DIM, GROUP_DIM), lambda bi, k: (0, 0)),
                  pl.BlockSpec((GROUP_DIM, GROUP_DIM), lambda bi, k: (0, 0))],
        out_specs=pl.BlockSpec((1, n2, pb * BRANCH_DIM), lambda bi, k: (bi, 0, k)),
        out_shape=jax.ShapeDtypeStruct((b, n2, n1 * BRANCH_DIM), BF16),
        compiler_params=_params("parallel", "parallel"), name="fft_stage_b",
    )(mid.reshape(b, n1, n2, 2 * BRANCH_DIM), stage2, cc, sc)
    return out.reshape(b * s, BRANCH_DIM)


def _dft_dense_kernel(x_ref, c_ref, s_ref, cc_ref, sc_ref, o_ref):
    x = x_ref[0]
    yr = jnp.dot(c_ref[...], x, preferred_element_type=F32).astype(BF16)
    yi = (-jnp.dot(s_ref[...], x, preferred_element_type=F32)).astype(BF16)
    for g in range(GROUPS):
        sl = slice(g * GROUP_DIM, (g + 1) * GROUP_DIM)
        o_ref[0, :, sl] = (jnp.dot(yr[:, sl], cc_ref[...], preferred_element_type=F32)
                           + jnp.dot(yi[:, sl], sc_ref[...], preferred_element_type=F32)).astype(o_ref.dtype)


def _fourier_ctx(cfg, p_ctx):
    b, n = cfg.batch, cfg.ctx
    cn, sn = _dft_cos_sin(n)
    cc, sc = _dft_cos_sin(GROUP_DIM, scale=(n * GROUP_DIM) ** -0.5)
    fblk = cfg.off_f // BRANCH_DIM
    full = lambda bi: (0, 0)
    out = pl.pallas_call(
        _dft_dense_kernel,
        grid=(b,),
        in_specs=[pl.BlockSpec((1, n, BRANCH_DIM), lambda bi: (bi, 0, fblk)),
                  pl.BlockSpec((n, n), full), pl.BlockSpec((n, n), full),
                  pl.BlockSpec((GROUP_DIM, GROUP_DIM), full), pl.BlockSpec((GROUP_DIM, GROUP_DIM), full)],
        out_specs=pl.BlockSpec((1, n, BRANCH_DIM), lambda bi: (bi, 0, 0)),
        out_shape=jax.ShapeDtypeStruct((b, n, BRANCH_DIM), BF16),
        compiler_params=_params("parallel"), name="dft_ctx",
    )(p_ctx.reshape(b, n, cfg.n_in), cn.astype(BF16), sn.astype(BF16), cc.astype(BF16), sc.astype(BF16))
    return out.reshape(b * n, BRANCH_DIM)


def _sgu_kernel(u_ref, v_ref, g_ref, w_ref, b_ref, o_ref, *, n_chunks):
    gv = g_ref[...]
    for c in range(n_chunks):
        rows = slice(c * CHUNK, (c + 1) * CHUNK)
        u = jax.nn.gelu(u_ref[rows, :].astype(F32))
        v = jax.nn.gelu(v_ref[rows, :].astype(F32))
        vc = v - jnp.mean(v, axis=-1, keepdims=True)
        vn = (vc * lax.rsqrt(jnp.mean(vc * vc, axis=-1, keepdims=True) + EPS) * gv).astype(BF16)
        for g in range(GROUPS):
            sl = slice(g * GROUP_DIM, (g + 1) * GROUP_DIM)
            sv = jnp.dot(w_ref[g], vn[:, sl], preferred_element_type=F32) + b_ref[:, g:g + 1]
            o_ref[rows, sl] = (u[:, sl] * sv).astype(o_ref.dtype)


def _sgu(cfg, p2, sgu_w, sgu_bt, sgu_g):
    t = p2.shape[0]
    tm = min(cfg.tm_sgu, t)
    ublk, vblk = cfg.off_u // BRANCH_DIM, cfg.off_v // BRANCH_DIM
    return pl.pallas_call(
        functools.partial(_sgu_kernel, n_chunks=tm // CHUNK),
        grid=(t // tm,),
        in_specs=[pl.BlockSpec((tm, BRANCH_DIM), lambda i: (i, ublk)),
                  pl.BlockSpec((tm, BRANCH_DIM), lambda i: (i, vblk)),
                  pl.BlockSpec((1, BRANCH_DIM), lambda i: (0, 0)),
                  pl.BlockSpec((GROUPS, CHUNK, CHUNK), lambda i: (0, 0, 0)),
                  pl.BlockSpec((CHUNK, GROUPS), lambda i: (0, 0))],
        out_specs=pl.BlockSpec((tm, BRANCH_DIM), lambda i: (i, 0)),
        out_shape=jax.ShapeDtypeStruct((t, BRANCH_DIM), BF16),
        compiler_params=_params("parallel"), name="spatial_gating",
    )(p2, p2, sgu_g, sgu_w, sgu_bt)


def _merge_kernel(pg0_ref, pg1_ref, pg2_ref, f_ref, s_ref, a_ref, x_ref, bg_ref,
                  wf_ref, ws_ref, wa_ref, wo_ref, g_ref, mod_ref, o_ref, *, d):
    def gate(pg_ref, k):
        return jax.nn.sigmoid(pg_ref[...].astype(F32) + bg_ref[:, k * d:(k + 1) * d])

    merged = gate(pg0_ref, 0) * jnp.dot(f_ref[...], wf_ref[...], preferred_element_type=F32)
    merged += gate(pg1_ref, 1) * jnp.dot(s_ref[...], ws_ref[...], preferred_element_type=F32)
    merged += gate(pg2_ref, 2) * jnp.dot(a_ref[...], wa_ref[...], preferred_element_type=F32)
    y = jnp.dot(merged.astype(BF16), wo_ref[...], preferred_element_type=F32)
    o_ref[...] = x_ref[...] + mod_ref[0][2:3] * _rms(y, g_ref[...])


def _resident(shape):
    return pl.BlockSpec(shape, lambda *_: (0,) * len(shape), pipeline_mode=pl.Buffered(1))


def _merge(cfg, p2, four, sgu, attn, x, b_gate, wf, ws, wa, wo, g, mod, rows_per_mod):
    t, d = x.shape
    tm = cfg.tm_merge
    row = lambda i: (i, 0)
    return pl.pallas_call(
        functools.partial(_merge_kernel, d=d),
        grid=(t // tm,),
        in_specs=[pl.BlockSpec((tm, d), lambda i: (i, 0)),
                  pl.BlockSpec((tm, d), lambda i: (i, 1)),
                  pl.BlockSpec((tm, d), lambda i: (i, 2)),
                  pl.BlockSpec((tm, BRANCH_DIM), row), pl.BlockSpec((tm, BRANCH_DIM), row),
                  pl.BlockSpec((tm, QK_DIM), row), pl.BlockSpec((tm, d), row),
                  _resident((1, 3 * d)), _resident((BRANCH_DIM, d)), _resident((BRANCH_DIM, d)),
                  _resident((QK_DIM, d)), _resident((d, d)), _resident((1, d)),
                  pl.BlockSpec((1, 6, d), lambda i: (i * tm // rows_per_mod, 0, 0))],
        out_specs=pl.BlockSpec((tm, d), row),
        out_shape=jax.ShapeDtypeStruct((t, d), F32),
        compiler_params=_params("parallel"), name="merge",
    )(p2, p2, p2, four, sgu, attn, x, b_gate, wf, ws, wa, wo, g, mod)


def _ffn_up_kernel(eid_ref, nact_ref, a_ref, w1_ref, w3_ref, o_ref, w1b_ref, w3b_ref):
    i = pl.program_id(1)
    active = i < nact_ref[0]
    fresh = jnp.logical_or(i == 0, eid_ref[i] != eid_ref[jnp.maximum(i - 1, 0)])

    @pl.when(jnp.logical_and(active, fresh))
    def _():
        w1b_ref[...] = w1_ref[0].astype(BF16)
        w3b_ref[...] = w3_ref[0].astype(BF16)

    @pl.when(active)
    def _():
        a = a_ref[...]
        h1 = jnp.dot(a, w1b_ref[...], preferred_element_type=F32)
        h3 = jnp.dot(a, w3b_ref[...], preferred_element_type=F32)
        o_ref[...] = (jax.nn.silu(h1) * h3).astype(o_ref.dtype)

    @pl.when(jnp.logical_not(active))
    def _():
        o_ref[...] = jnp.zeros(o_ref.shape, o_ref.dtype)


def _ffn_up(cfg, a, w1, w3, tile_eid, n_active, tm):
    r, d = a.shape
    dff = w1.shape[2]
    tn = cfg.tn_up
    grid_spec = pltpu.PrefetchScalarGridSpec(
        num_scalar_prefetch=2,
        grid=(dff // tn, r // tm),
        in_specs=[pl.BlockSpec((tm, d), lambda j, i, eid, na: (jnp.minimum(i, na[0] - 1), 0)),
                  pl.BlockSpec((1, d, tn), lambda j, i, eid, na: (eid[i], 0, j)),
                  pl.BlockSpec((1, d, tn), lambda j, i, eid, na: (eid[i], 0, j))],
        out_specs=pl.BlockSpec((tm, tn), lambda j, i, eid, na: (i, j)),
        scratch_shapes=[pltpu.VMEM((d, tn), BF16), pltpu.VMEM((d, tn), BF16)],
    )
    return pl.pallas_call(
        _ffn_up_kernel, grid_spec=grid_spec,
        out_shape=jax.ShapeDtypeStruct((r, dff), BF16),
        compiler_params=_params("arbitrary", "arbitrary"), name="swiglu_up",
    )(tile_eid, n_active, a, w1, w3)


def _ffn_down_kernel(eid_ref, nact_ref, *refs, residual):
    if residual:
        h_ref, w_ref, x_ref, g_ref, mod_ref, o_ref, acc_ref = refs
    else:
        h_ref, w_ref, o_ref, acc_ref = refs
    k = pl.program_id(1)

    @pl.when(k == 0)
    def _():
        acc_ref[...] = jnp.zeros(acc_ref.shape, F32)

    @pl.when(pl.program_id(0) < nact_ref[0])
    def _():
        acc_ref[...] += jnp.dot(h_ref[...], w_ref[0], preferred_element_type=F32)

    @pl.when(k == pl.num_programs(1) - 1)
    def _():
        if residual:
            o_ref[...] = x_ref[...] + mod_ref[0][5:6] * _rms(acc_ref[...], g_ref[...])
        else:
            o_ref[...] = acc_ref[...].astype(o_ref.dtype)


def _ffn_down(cfg, h, w2, tile_eid, n_active, tm, tk, x=None, g=None, mod=None, rows_per_mod=None):
    r, dff = h.shape
    d = w2.shape[2]
    residual = x is not None
    in_specs = [pl.BlockSpec((tm, tk), lambda i, k, eid, na: (jnp.minimum(i, na[0] - 1), k)),
                pl.BlockSpec((1, tk, d), lambda i, k, eid, na: (eid[i], k, 0))]
    args = [h, w2]
    if residual:
        in_specs += [pl.BlockSpec((tm, d), lambda i, k, eid, na: (i, 0)),
                     pl.BlockSpec((1, d), lambda i, k, eid, na: (0, 0)),
                     pl.BlockSpec((1, 6, d), lambda i, k, eid, na: (i * tm // rows_per_mod, 0, 0))]
        args += [x, g, mod]
    grid_spec = pltpu.PrefetchScalarGridSpec(
        num_scalar_prefetch=2,
        grid=(r // tm, dff // tk),
        in_specs=in_specs,
        out_specs=pl.BlockSpec((tm, d), lambda i, k, eid, na: (i, 0)),
        scratch_shapes=[pltpu.VMEM((tm, d), F32)],
    )
    return pl.pallas_call(
        functools.partial(_ffn_down_kernel, residual=residual), grid_spec=grid_spec,
        out_shape=jax.ShapeDtypeStruct((r, d), F32 if residual else BF16),
        compiler_params=_params("parallel", "arbitrary"),
        name="swiglu_down_residual" if residual else "swiglu_down",
    )(tile_eid, n_active, *args)


def _combine_kernel(y1_ref, y2_ref, wt_ref, x_ref, g_ref, mod_ref, o_ref):
    wt = wt_ref[...]
    y = wt[:, 0:1] * y1_ref[...].astype(F32) + wt[:, 1:2] * y2_ref[...].astype(F32)
    o_ref[...] = x_ref[...] + mod_ref[0][5:6] * _rms(y, g_ref[...])


def _combine(cfg, y1, y2, wt, x, g, mod, rows_per_mod):
    t, d = x.shape
    tm = min(cfg.tm_norm, t)
    row = lambda i: (i, 0)
    return pl.pallas_call(
        _combine_kernel,
        grid=(t // tm,),
        in_specs=[pl.BlockSpec((tm, d), row), pl.BlockSpec((tm, d), row),
                  pl.BlockSpec((tm, TOP_K), row), pl.BlockSpec((tm, d), row),
                  pl.BlockSpec((1, d), lambda i: (0, 0)),
                  pl.BlockSpec((1, 6, d), lambda i: (i * tm // rows_per_mod, 0, 0))],
        out_specs=pl.BlockSpec((tm, d), row),
        out_shape=jax.ShapeDtypeStruct((t, d), F32),
        compiler_params=_params("parallel"), name="moe_combine",
    )(y1, y2, wt, x, g, mod)


def _dispatch_plan(top_i, tm):
    t = top_i.shape[1]
    n_pairs = TOP_K * t
    n_rows = n_pairs + N_EXPERTS * tm
    e_flat = top_i.reshape(n_pairs)
    onehot = (e_flat[:, None] == jnp.arange(N_EXPERTS, dtype=jnp.int32)[None, :]).astype(jnp.int32)
    rank = jnp.sum(jnp.cumsum(onehot, axis=0) * onehot, axis=1) - 1
    counts = jnp.sum(onehot, axis=0)
    padded = (counts + tm - 1) // tm * tm
    ends = jnp.cumsum(padded)
    starts = ends - padded
    dest = starts[e_flat] + rank
    tok = jnp.tile(jnp.arange(t, dtype=jnp.int32), TOP_K)
    row_tok = jnp.zeros((n_rows,), jnp.int32).at[dest].set(tok)
    tile_start = jnp.arange(n_rows // tm, dtype=jnp.int32) * tm
    tile_eid = jnp.minimum(jnp.sum((tile_start[:, None] >= ends[None, :]).astype(jnp.int32), axis=1),
                           N_EXPERTS - 1)
    n_active = (ends[-1] // tm).astype(jnp.int32).reshape(1)
    return row_tok, dest.reshape(TOP_K, t), tile_eid, n_active


def _dense_plan(t, tm):
    return jnp.zeros((t // tm,), jnp.int32), jnp.full((1,), t // tm, jnp.int32)


def _permute_w_in(w):
    b3 = 3 * BRANCH_DIM
    q3 = 3 * QK_DIM
    return jnp.concatenate([w[:, b3 + q3:], w[:, b3:b3 + q3], w[:, :b3]], axis=1).astype(BF16)


def _forward(cfg, x, c, ctx, c_ctx, w_mod, b_mod, g_norm, w_in, b_gate, w_fourier_out, w_sgu_out,
             w_attn_out, w_o, sgu_w, sgu_b, sgu_g, diff_lambda, diff_subln_g,
             ffn_w1, ffn_w3, ffn_w2, router_w, moe_w1, moe_w3, moe_w2):
    b, s, d = x.shape
    n_ctx = ctx.shape[1]
    t_lat, t_ctx = b * s, b * n_ctx
    cos, sin = _rope_tables(s)
    fft_tables = _fft_tables(s)

    c_rows = jnp.concatenate([c, c_ctx[None, :], jnp.zeros((8 - b - 1, d), F32)], axis=0)
    mod_all = _modulation(cfg, c_rows, w_mod, b_mod)

    xl = x.reshape(t_lat, d)
    xc = ctx.reshape(t_ctx, d)
    for l in range(cfg.depth):
        last = l == cfg.depth - 1
        lam_init = 0.8 - 0.6 * math.exp(-0.3 * l)
        mod_l = mod_all[l, :b].reshape(b, 6, d)
        mod_c = mod_all[l, b:b + 1].reshape(1, 6, d)
        g = g_norm[l].reshape(4, 1, d)
        w_in_l = _permute_w_in(w_in[l])
        wf, ws = w_fourier_out[l].astype(BF16), w_sgu_out[l].astype(BF16)
        wa, wo = w_attn_out[l].astype(BF16), w_o[l].astype(BF16)
        bg = b_gate[l].reshape(1, 3 * d)
        sw = sgu_w[l].astype(BF16)
        sbt = sgu_b[l].T
        sg = sgu_g[l].reshape(1, BRANCH_DIM)
        dl = diff_lambda[l]
        gsub = diff_subln_g[l].reshape(1, PAIR)
        pair_blk = lambda off: off // PAIR

        p_lat = _in_proj(cfg, xl, g[0], mod_l, s, w_in_l)
        p_lat3 = p_lat.reshape(b, s, cfg.n_in)
        qk_rot = _rope(cfg, p_lat3, cos, sin)
        if last:
            p_ctx3 = _in_proj(cfg, xc, g[0], mod_c, t_ctx,
                              w_in_l[:, cfg.off_k:cfg.off_f]).reshape(b, n_ctx, 2 * QK_DIM)
            kc_blk, vc_blk = 0, pair_blk(QK_DIM)
        else:
            p_ctx = _in_proj(cfg, xc, g[0], mod_c, t_ctx, w_in_l)
            p_ctx3 = p_ctx.reshape(b, n_ctx, cfg.n_in)
            kc_blk, vc_blk = pair_blk(cfg.off_k), pair_blk(cfg.off_va)
        al = _attention(cfg, qk_rot[0], 0, p_ctx3, kc_blk, p_ctx3, vc_blk, dl, gsub, lam_init, cfg.tq,
                        k_lat=qk_rot, v_arr=p_lat3, v_blk=pair_blk(cfg.off_va))
        four_l = _fourier_latent(cfg, p_lat, fft_tables)
        sgu_l = _sgu(cfg, p_lat, sw, sbt, sg)
        xl = _merge(cfg, p_lat, four_l, sgu_l, al.reshape(t_lat, QK_DIM), xl, bg, wf, ws, wa, wo, g[1], mod_l, s)
        if not last:
            ac = _attention(cfg, p_ctx3, pair_blk(cfg.off_q), p_ctx3, kc_blk, p_ctx3, vc_blk, dl, gsub,
                            lam_init, n_ctx)
            four_c = _fourier_ctx(cfg, p_ctx)
            sgu_c = _sgu(cfg, p_ctx, sw, sbt, sg)
            xc = _merge(cfg, p_ctx, four_c, sgu_c, ac.reshape(t_ctx, QK_DIM), xc, bg, wf, ws, wa, wo, g[1],
                        mod_c, t_ctx)

        i = l // 2
        streams = [(xl, mod_l, s)] + ([] if last else [(xc, mod_c, t_ctx)])
        outs = []
        if l % 2 == 0:
            w2 = ffn_w2[i][None].astype(BF16)
            for xs, mod, rpm in streams:
                t = xs.shape[0]
                fl = _norm_mod(cfg, xs, g[2], mod, rpm, 3, 4)
                tm_up = min(cfg.tm_up, t)
                eid_up, n_up = _dense_plan(t, tm_up)
                h = _ffn_up(cfg, fl, ffn_w1, ffn_w3, eid_up + i, n_up, tm_up)
                tm_dn = min(cfg.tm_down_dense, t)
                outs.append(_ffn_down(cfg, h, w2, *_dense_plan(t, tm_dn), tm_dn, cfg.tk_down_dense, x=xs,
                                      g=g[3], mod=mod, rows_per_mod=rpm))
        else:
            w1 = moe_w1.reshape(-1, d, cfg.dff)
            w3 = moe_w3.reshape(-1, d, cfg.dff)
            w2 = moe_w2[i].astype(BF16)
            rwt = router_w[i].T
            for xs, mod, rpm in streams:
                tm = cfg.tm_down if xs.shape[0] >= N_EXPERTS * cfg.tm_down else cfg.tm_down_ctx
                fl, top_i, top_w = _norm_mod(cfg, xs, g[2], mod, rpm, 3, 4, router_wt=rwt)
                row_tok, pos, tile_eid, n_active = _dispatch_plan(top_i[:TOP_K], tm)
                a = jnp.take(fl, row_tok, axis=0)
                h = _ffn_up(cfg, a, w1, w3, tile_eid + i * N_EXPERTS, n_active, tm)
                y = _ffn_down(cfg, h, w2, tile_eid, n_active, tm, cfg.tk_down)
                outs.append(_combine(cfg, jnp.take(y, pos[0], axis=0), jnp.take(y, pos[1], axis=0),
                                     top_w[:TOP_K].T, xs, g[3], mod, rpm))
        xl = outs[0]
        if not last:
            xc = outs[1]
    return xl.reshape(b, s, d)


def kernel(x, c, ctx, c_ctx, w_mod, b_mod, g_norm, w_in, b_gate, w_fourier_out, w_sgu_out, w_attn_out, w_o,
           sgu_w, sgu_b, sgu_g, diff_lambda, diff_subln_g, ffn_w1, ffn_w3, ffn_w2, router_w,
           moe_w1, moe_w3, moe_w2):
    return _forward(PROD, x, c, ctx, c_ctx, w_mod, b_mod, g_norm, w_in, b_gate, w_fourier_out, w_sgu_out,
                    w_attn_out, w_o, sgu_w, sgu_b, sgu_g, diff_lambda, diff_subln_g,
                    ffn_w1, ffn_w3, ffn_w2, router_w, moe_w1, moe_w3, moe_w2)
```

```python
import functools
import math
from typing import NamedTuple

import jax
import jax.numpy as jnp
from jax import lax
from jax.experimental import pallas as pl
from jax.experimental.pallas import tpu as pltpu

F32 = jnp.float32
BF16 = jnp.bfloat16

EPS = 1e-6
GRID_W = 64
ROPE_BASE = 10000.0
HEAD_DIM = 64
HEADS = 8
PAIR = 2 * HEAD_DIM
QK_DIM = HEADS * PAIR
GROUP_DIM = 128
GROUPS = 4
BRANCH_DIM = GROUPS * GROUP_DIM
CHUNK = 128
FFT_N1 = 128
FFT_A_PER_STEP = 4
FFT_B_PER_STEP = 8
N_EXPERTS = 8
TOP_K = 2
EXPERT_ROWS = 8
ATTN_SCALE = HEAD_DIM ** -0.5
Q_SCALE = ATTN_SCALE * math.log2(math.e)
VMEM_LIMIT = 56 * 1024 * 1024


class Cfg(NamedTuple):
    d: int
    batch: int
    seq: int
    ctx: int
    dff: int
    depth: int
    tm_norm: int
    tm_mm: int
    tn_mm: int
    tm_rope: int
    tq: int
    tk: int
    tm_sgu: int
    tm_merge: int
    tm_up: int
    tn_up: int
    tm_down: int
    tm_down_cols: int
    tn_down_cols: int
    tm_down_ctx: int
    tm_down_dense: int
    tk_down_dense: int
    tn_mod: int

    @property
    def n_in(self):
        return 3 * self.d + 3 * QK_DIM + 3 * BRANCH_DIM

    @property
    def off_q(self):
        return 3 * self.d

    @property
    def off_k(self):
        return self.off_q + QK_DIM

    @property
    def off_va(self):
        return self.off_k + QK_DIM

    @property
    def off_f(self):
        return self.off_va + QK_DIM

    @property
    def off_u(self):
        return self.off_f + BRANCH_DIM

    @property
    def off_v(self):
        return self.off_u + BRANCH_DIM


PROD = Cfg(d=2048, batch=4, seq=8192, ctx=256, dff=5632, depth=4,
           tm_norm=512, tm_mm=1024, tn_mm=1536, tm_rope=512, tq=512, tk=256,
           tm_sgu=1024, tm_merge=256, tm_up=1024, tn_up=512, tm_down=1024, tm_down_cols=512, tn_down_cols=512,
           tm_down_ctx=256, tm_down_dense=1024, tk_down_dense=512, tn_mod=1024)


def _params(*sem):
    return pltpu.CompilerParams(dimension_semantics=sem, vmem_limit_bytes=VMEM_LIMIT)


def _rms(y, g):
    return y * lax.rsqrt(jnp.mean(y * y, axis=-1, keepdims=True) + EPS) * g


def _mod_kernel(c_ref, w_ref, b_ref, o_ref):
    sc = jax.nn.silu(c_ref[...])
    o_ref[0] = jnp.dot(sc.astype(BF16), w_ref[0].astype(BF16), preferred_element_type=F32) + b_ref[0]


def _modulation(cfg, c_rows, w_mod, b_mod):
    depth, d, n6 = w_mod.shape
    tn = cfg.tn_mod
    return pl.pallas_call(
        _mod_kernel,
        grid=(depth, n6 // tn),
        in_specs=[pl.BlockSpec((8, d), lambda l, j: (0, 0)),
                  pl.BlockSpec((1, d, tn), lambda l, j: (l, 0, j)),
                  pl.BlockSpec((1, 1, tn), lambda l, j: (l, 0, j))],
        out_specs=pl.BlockSpec((1, 8, tn), lambda l, j: (l, 0, j)),
        out_shape=jax.ShapeDtypeStruct((depth, 8, n6), F32),
        compiler_params=_params("parallel", "parallel"),
        name="modulation",
    )(c_rows, w_mod, b_mod.reshape(depth, 1, n6))


def _norm_mod_kernel(x_ref, g_ref, mod_ref, o_ref, *, shift_idx, scale_idx):
    y = _rms(x_ref[...], g_ref[...])
    m = mod_ref[0]
    o_ref[...] = (y * (1.0 + m[scale_idx:scale_idx + 1]) + m[shift_idx:shift_idx + 1]).astype(o_ref.dtype)


def _norm_mod_router_kernel(x_ref, g_ref, mod_ref, rw_ref, tri_ref, o_ref, idx_ref, wt_ref, cnt_ref, *,
                            shift_idx, scale_idx):
    @pl.when(pl.program_id(0) == 0)
    def _():
        cnt_ref[...] = jnp.zeros(cnt_ref.shape, F32)

    y = _rms(x_ref[...], g_ref[...])
    m = mod_ref[0]
    h = y * (1.0 + m[scale_idx:scale_idx + 1]) + m[shift_idx:shift_idx + 1]
    o_ref[...] = h.astype(o_ref.dtype)
    logits = lax.dot_general(rw_ref[...], h, (((1,), (1,)), ((), ())),
                             precision=lax.Precision.HIGHEST, preferred_element_type=F32)
    row = lax.broadcasted_iota(jnp.int32, logits.shape, 0).astype(F32)
    m1 = jnp.max(logits, axis=0, keepdims=True)
    i1 = jnp.min(jnp.where(logits == m1, row, float(N_EXPERTS)), axis=0, keepdims=True)
    rest = jnp.where(row == i1, -jnp.inf, logits)
    m2 = jnp.max(rest, axis=0, keepdims=True)
    i2 = jnp.min(jnp.where(rest == m2, row, float(N_EXPERTS)), axis=0, keepdims=True)
    e2 = jnp.exp(m2 - m1)
    w1 = 1.0 / (1.0 + e2)
    w2 = e2 / (1.0 + e2)
    wt_ref[...] = jnp.where(row == 0.0, w1, jnp.where(row == 1.0, w2, 0.0))
    pick1 = jnp.where(row == i1, 1.0, 0.0)
    pick2 = jnp.where(row == i2, 1.0, 0.0)
    pre1 = jnp.dot(pick1.astype(BF16), tri_ref[...], preferred_element_type=F32)
    pre2 = jnp.dot(pick2.astype(BF16), tri_ref[...], preferred_element_type=F32)
    n1 = jnp.sum(pick1, axis=1, keepdims=True)
    n2 = jnp.sum(pick2, axis=1, keepdims=True)
    seen = cnt_ref[:, 0:1]
    r1 = jnp.sum(pick1 * (pre1 - 1.0 + seen), axis=0, keepdims=True)
    r2 = jnp.sum(pick2 * (pre2 - 1.0 + seen + n1), axis=0, keepdims=True)
    cnt_ref[...] = cnt_ref[...] + (n1 + n2)
    idx_ref[...] = jnp.where(row == 0.0, i1, jnp.where(row == 1.0, i2, jnp.where(row == 2.0, r1, jnp.where(
        row == 3.0, r2, 0.0)))).astype(jnp.int32)


def _norm_mod(cfg, x, g, mod, rows_per_mod, shift_idx, scale_idx, router_wt=None):
    t, d = x.shape
    tm = min(cfg.tm_norm, t)
    in_specs = [pl.BlockSpec((tm, d), lambda i: (i, 0)),
                pl.BlockSpec((1, d), lambda i: (0, 0)),
                pl.BlockSpec((1, 6, d), lambda i: (i * tm // rows_per_mod, 0, 0))]
    h_spec = pl.BlockSpec((tm, d), lambda i: (i, 0))
    h_shape = jax.ShapeDtypeStruct((t, d), BF16)
    if router_wt is None:
        return pl.pallas_call(
            functools.partial(_norm_mod_kernel, shift_idx=shift_idx, scale_idx=scale_idx),
            grid=(t // tm,), in_specs=in_specs, out_specs=h_spec, out_shape=h_shape,
            compiler_params=_params("parallel"), name="norm_mod",
        )(x, g, mod)
    r_spec = pl.BlockSpec((EXPERT_ROWS, tm), lambda i: (0, i))
    tri = jnp.triu(jnp.ones((tm, tm), BF16))
    return pl.pallas_call(
        functools.partial(_norm_mod_router_kernel, shift_idx=shift_idx, scale_idx=scale_idx),
        grid=(t // tm,),
        in_specs=in_specs + [pl.BlockSpec((N_EXPERTS, d), lambda i: (0, 0)),
                             pl.BlockSpec((tm, tm), lambda i: (0, 0))],
        out_specs=[h_spec, r_spec, r_spec, pl.BlockSpec((EXPERT_ROWS, 128), lambda i: (0, 0))],
        out_shape=[h_shape, jax.ShapeDtypeStruct((EXPERT_ROWS, t), jnp.int32),
                   jax.ShapeDtypeStruct((EXPERT_ROWS, t), F32),
                   jax.ShapeDtypeStruct((EXPERT_ROWS, 128), F32)],
        compiler_params=_params("arbitrary"), name="norm_mod_router",
    )(x, g, mod, router_wt, tri)


def _in_proj_kernel(x_ref, g_ref, mod_ref, w_ref, o_ref, h_ref):
    @pl.when(pl.program_id(1) == 0)
    def _():
        m = mod_ref[0]
        h_ref[...] = (_rms(x_ref[...], g_ref[...]) * (1.0 + m[1:2]) + m[0:1]).astype(h_ref.dtype)

    o_ref[...] = jnp.dot(h_ref[...], w_ref[...], preferred_element_type=F32).astype(o_ref.dtype)


def _in_proj(cfg, x, g, mod, rows_per_mod, w):
    t, d = x.shape
    n = w.shape[1]
    tm = min(cfg.tm_mm, t)
    tn = cfg.tn_mm if n % cfg.tn_mm == 0 else 512
    return pl.pallas_call(
        _in_proj_kernel,
        grid=(t // tm, n // tn),
        in_specs=[pl.BlockSpec((tm, d), lambda i, j: (i, 0)),
                  pl.BlockSpec((1, d), lambda i, j: (0, 0)),
                  pl.BlockSpec((1, 6, d), lambda i, j: (i * tm // rows_per_mod, 0, 0)),
                  pl.BlockSpec((d, tn), lambda i, j: (0, j))],
        out_specs=pl.BlockSpec((tm, tn), lambda i, j: (i, j)),
        out_shape=jax.ShapeDtypeStruct((t, n), BF16),
        scratch_shapes=[pltpu.VMEM((tm, d), BF16)],
        compiler_params=_params("parallel", "arbitrary"), name="in_proj",
    )(x, g, mod, w)


def _rope_tables(seq):
    rows = seq // GRID_W
    r = jnp.repeat(jnp.arange(rows, dtype=F32), GRID_W)
    col = jnp.tile(jnp.arange(GRID_W, dtype=F32), rows)
    n_freq = HEAD_DIM // 4
    inv = ROPE_BASE ** (-jnp.arange(n_freq, dtype=F32) / n_freq)
    ang = jnp.concatenate([r[:, None] * inv, col[:, None] * inv], axis=-1)
    cos, sin = jnp.cos(ang), jnp.sin(ang)
    return jnp.tile(cos, (1, 4)), jnp.tile(jnp.concatenate([-sin, sin], axis=-1), (1, 2))


def _rope_kernel(p_ref, cos_ref, sin_ref, o_ref):
    scale = jnp.where(pl.program_id(2) == 0, Q_SCALE, 1.0).astype(F32)
    cos = cos_ref[...] * scale
    sin = sin_ref[...] * scale
    lane = lax.broadcasted_iota(jnp.int32, cos.shape, 1)
    first_half = (lane % HEAD_DIM) < (HEAD_DIM // 2)
    for c in range(QK_DIM // PAIR):
        t = p_ref[0, :, c * PAIR:(c + 1) * PAIR].astype(F32)
        partner = jnp.where(first_half, pltpu.roll(t, PAIR - HEAD_DIM // 2, 1), pltpu.roll(t, HEAD_DIM // 2, 1))
        o_ref[0, 0, :, c * PAIR:(c + 1) * PAIR] = (t * cos + partner * sin).astype(o_ref.dtype)


def _rope(cfg, p3, cos, sin):
    b, s, _ = p3.shape
    tm = cfg.tm_rope
    qblk = cfg.off_q // QK_DIM
    return pl.pallas_call(
        _rope_kernel,
        grid=(b, s // tm, 2),
        in_specs=[pl.BlockSpec((1, tm, QK_DIM), lambda bi, i, j: (bi, i, qblk + j)),
                  pl.BlockSpec((tm, PAIR), lambda bi, i, j: (i, 0)),
                  pl.BlockSpec((tm, PAIR), lambda bi, i, j: (i, 0))],
        out_specs=pl.BlockSpec((1, 1, tm, QK_DIM), lambda bi, i, j: (j, bi, i, 0)),
        out_shape=jax.ShapeDtypeStruct((2, b, s, QK_DIM), BF16),
        compiler_params=_params("parallel", "parallel", "parallel"), name="rope",
    )(p3, cos, sin)


def _attn_kernel(*refs, tq, tk, n_lat, q_prescaled):
    if n_lat:
        q_ref, kc_ref, vc_ref, k_ref, v_ref, dl_ref, g_ref, li_ref, o_ref, q2_ref, vx_ref = refs
    else:
        q_ref, kc_ref, vc_ref, dl_ref, g_ref, li_ref, o_ref, q2_ref, vx_ref = refs
    lc = kc_ref.shape[1]

    @pl.when(pl.program_id(2) == 0)
    def _():
        vx_ref[:lc, :PAIR] = vc_ref[0]
        vx_ref[:lc, PAIR:] = jnp.ones((lc, PAIR), BF16)
        if n_lat:
            vx_ref[lc:, :PAIR] = v_ref[0]
            vx_ref[lc:, PAIR:] = jnp.ones((n_lat * tk, PAIR), BF16)

    q = q_ref[0]
    if not q_prescaled:
        q = (q.astype(F32) * Q_SCALE).astype(BF16)
    lane = lax.broadcasted_iota(jnp.int32, q.shape, 1)
    zero = jnp.zeros_like(q)
    q2_ref[:tq] = jnp.where(lane < HEAD_DIM, q, zero)
    q2_ref[tq:] = jnp.where(lane >= HEAD_DIM, q, zero)
    q2 = q2_ref[...]

    chunks = [(kc_ref[0], 0, lc)]
    for j in range(n_lat):
        chunks.append((k_ref[0, 0, j * tk:(j + 1) * tk, :], lc + j * tk, tk))
    m = jnp.full((2 * tq, 1), -jnp.inf, F32)
    acc = jnp.zeros((2 * tq, 2 * PAIR), F32)
    for k, off, n in chunks:
        s = lax.dot_general(q2, k, (((1,), (1,)), ((), ())), preferred_element_type=F32)
        m_new = jnp.maximum(m, jnp.max(s, axis=-1, keepdims=True))
        p = jnp.exp2(s - m_new).astype(BF16)
        acc = jnp.exp2(m - m_new) * acc + jnp.dot(p, vx_ref[off:off + n, :], preferred_element_type=F32)
        m = m_new

    o12 = acc[:, :PAIR] / acc[:, PAIR:]
    dl = dl_ref[...]
    lam_init = li_ref[...]
    lam = (jnp.exp(jnp.sum(dl[0:1] * dl[1:2], axis=-1, keepdims=True))
           - jnp.exp(jnp.sum(dl[2:3] * dl[3:4], axis=-1, keepdims=True)) + lam_init)
    o = o12[:tq] - lam * o12[tq:]
    o_ref[0] = (_rms(o, g_ref[...]) * (1.0 - lam_init)).astype(o_ref.dtype)


def _attention(cfg, q_arr, q_blk, kc_arr, kc_blk, vc_arr, vc_blk, dl, g_sub, lam_init, tq,
               k_lat=None, v_arr=None, v_blk=0):
    b, lq = q_arr.shape[0], q_arr.shape[1]
    lc = kc_arr.shape[1]
    in_specs = [pl.BlockSpec((1, tq, PAIR), lambda bi, h, i: (bi, i, q_blk + h)),
                pl.BlockSpec((1, lc, PAIR), lambda bi, h, i: (bi, 0, kc_blk + h)),
                pl.BlockSpec((1, lc, PAIR), lambda bi, h, i: (bi, 0, vc_blk + h))]
    args = [q_arr, kc_arr, vc_arr]
    n_lat = 0
    if k_lat is not None:
        s = k_lat.shape[2]
        n_lat = s // cfg.tk
        in_specs += [pl.BlockSpec((1, 1, s, PAIR), lambda bi, h, i: (1, bi, 0, h)),
                     pl.BlockSpec((1, s, PAIR), lambda bi, h, i: (bi, 0, v_blk + h))]
        args += [k_lat, v_arr]
    in_specs += [pl.BlockSpec((4, HEAD_DIM), lambda bi, h, i: (0, 0)),
                 pl.BlockSpec((1, PAIR), lambda bi, h, i: (0, 0)),
                 pl.BlockSpec((1, PAIR), lambda bi, h, i: (0, 0))]
    args += [dl, g_sub, jnp.full((1, PAIR), lam_init, F32)]
    lk = lc + n_lat * cfg.tk
    return pl.pallas_call(
        functools.partial(_attn_kernel, tq=tq, tk=cfg.tk, n_lat=n_lat, q_prescaled=k_lat is not None),
        grid=(b, HEADS, lq // tq),
        in_specs=in_specs,
        out_specs=pl.BlockSpec((1, tq, PAIR), lambda bi, h, i: (bi, i, h)),
        out_shape=jax.ShapeDtypeStruct((b, lq, QK_DIM), BF16),
        scratch_shapes=[pltpu.VMEM((2 * tq, PAIR), BF16), pltpu.VMEM((lk, 2 * PAIR), BF16)],
        compiler_params=_params("parallel", "parallel", "arbitrary"),
        name="diff_attention" if n_lat else "diff_attention_ctx",
    )(*args)


def _dft_cos_sin(n, scale=1.0):
    j = jnp.arange(n, dtype=jnp.int32)
    ang = ((j[:, None] * j[None, :]) % n).astype(F32) * (2.0 * math.pi / n)
    return jnp.cos(ang) * scale, jnp.sin(ang) * scale


def _fft_tables(seq):
    n1, n2 = FFT_N1, seq // FFT_N1
    c1, s1 = _dft_cos_sin(n1)
    stage1 = jnp.concatenate([c1, -s1], axis=0).astype(BF16)
    k1 = jnp.arange(n1, dtype=jnp.int32)
    m = jnp.arange(n2, dtype=jnp.int32)
    ang = (m[:, None] * k1[None, :]).astype(F32) * (2.0 * math.pi / seq)
    tw_cos, tw_sin = jnp.cos(ang)[:, :, None], jnp.sin(ang)[:, :, None]
    c2, s2 = _dft_cos_sin(n2)
    stage2 = jnp.concatenate([jnp.concatenate([c2, s2], axis=1),
                              jnp.concatenate([-s2, c2], axis=1)], axis=0).astype(BF16)
    cc, sc = _dft_cos_sin(GROUP_DIM, scale=(seq * GROUP_DIM) ** -0.5)
    return stage1, tw_cos, tw_sin, stage2, cc.astype(BF16), sc.astype(BF16)


def _fft_a_kernel(x_ref, w_ref, c_ref, s_ref, o_ref):
    for t in range(FFT_A_PER_STEP):
        x = x_ref[0, :, t * BRANCH_DIM:(t + 1) * BRANCH_DIM]
        a = jnp.dot(w_ref[...], x, preferred_element_type=F32)
        re, im = a[:FFT_N1], a[FFT_N1:]
        c, s = c_ref[t], s_ref[t]
        base = 2 * t * BRANCH_DIM
        o_ref[0, :, base:base + BRANCH_DIM] = (re * c + im * s).astype(o_ref.dtype)
        o_ref[0, :, base + BRANCH_DIM:base + 2 * BRANCH_DIM] = (im * c - re * s).astype(o_ref.dtype)


def _fft_b_kernel(x_ref, w_ref, cc_ref, sc_ref, o_ref, *, n2):
    for t in range(FFT_B_PER_STEP):
        x = x_ref[0, t]
        z = jnp.concatenate([x[:, :BRANCH_DIM], x[:, BRANCH_DIM:]], axis=0)
        y = jnp.dot(w_ref[...], z, preferred_element_type=F32).astype(BF16)
        for g in range(GROUPS):
            sl = slice(g * GROUP_DIM, (g + 1) * GROUP_DIM)
            o_ref[0, :, t * BRANCH_DIM + g * GROUP_DIM:t * BRANCH_DIM + (g + 1) * GROUP_DIM] = (
                jnp.dot(y[:n2, sl], cc_ref[...], preferred_element_type=F32)
                + jnp.dot(y[n2:, sl], sc_ref[...], preferred_element_type=F32)).astype(o_ref.dtype)


def _fourier_latent(cfg, p_lat, tables):
    b, s = cfg.batch, cfg.seq
    n1, n2 = FFT_N1, s // FFT_N1
    stage1, tw_cos, tw_sin, stage2, cc, sc = tables
    f = p_lat[:, cfg.off_f:cfg.off_f + BRANCH_DIM]
    pa, pb = FFT_A_PER_STEP, FFT_B_PER_STEP
    mid = pl.pallas_call(
        _fft_a_kernel,
        grid=(b, n2 // pa),
        in_specs=[pl.BlockSpec((1, n1, pa * BRANCH_DIM), lambda bi, m: (bi, 0, m)),
                  pl.BlockSpec((2 * n1, n1), lambda bi, m: (0, 0)),
                  pl.BlockSpec((pa, n1, 1), lambda bi, m: (m, 0, 0)),
                  pl.BlockSpec((pa, n1, 1), lambda bi, m: (m, 0, 0))],
        out_specs=pl.BlockSpec((1, n1, pa * 2 * BRANCH_DIM), lambda bi, m: (bi, 0, m)),
        out_shape=jax.ShapeDtypeStruct((b, n1, n2 * 2 * BRANCH_DIM), BF16),
        compiler_params=_params("parallel", "parallel"), name="fft_stage_a",
    )(f.reshape(b, n1, n2 * BRANCH_DIM), stage1, tw_cos, tw_sin)
    out = pl.pallas_call(
        functools.partial(_fft_b_kernel, n2=n2),
        grid=(b, n1 // pb),
        in_specs=[pl.BlockSpec((1, pb, n2, 2 * BRANCH_DIM), lambda bi, k: (bi, k, 0, 0)),
                  pl.BlockSpec((2 * n2, 2 * n2), lambda bi, k: (0, 0)),
                  pl.BlockSpec((GROUP_DIM, GROUP_DIM), lambda bi, k: (0, 0)),
                  pl.BlockSpec((GROUP_DIM, GROUP_DIM), lambda bi, k: (0, 0))],
        out_specs=pl.BlockSpec((1, n2, pb * BRANCH_DIM), lambda bi, k: (bi, 0, k)),
        out_shape=jax.ShapeDtypeStruct((b, n2, n1 * BRANCH_DIM), BF16),
        compiler_params=_params("parallel", "parallel"), name="fft_stage_b",
    )(mid.reshape(b, n1, n2, 2 * BRANCH_DIM), stage2, cc, sc)
    return out.reshape(b * s, BRANCH_DIM)


def _dft_dense_kernel(x_ref, c_ref, s_ref, cc_ref, sc_ref, o_ref):
    x = x_ref[0]
    yr = jnp.dot(c_ref[...], x, preferred_element_type=F32).astype(BF16)
    yi = (-jnp.dot(s_ref[...], x, preferred_element_type=F32)).astype(BF16)
    for g in range(GROUPS):
        sl = slice(g * GROUP_DIM, (g + 1) * GROUP_DIM)
        o_ref[0, :, sl] = (jnp.dot(yr[:, sl], cc_ref[...], preferred_element_type=F32)
                           + jnp.dot(yi[:, sl], sc_ref[...], preferred_element_type=F32)).astype(o_ref.dtype)


def _fourier_ctx(cfg, p_ctx):
    b, n = cfg.batch, cfg.ctx
    cn, sn = _dft_cos_sin(n)
    cc, sc = _dft_cos_sin(GROUP_DIM, scale=(n * GROUP_DIM) ** -0.5)
    fblk = cfg.off_f // BRANCH_DIM
    full = lambda bi: (0, 0)
    out = pl.pallas_call(
        _dft_dense_kernel,
        grid=(b,),
        in_specs=[pl.BlockSpec((1, n, BRANCH_DIM), lambda bi: (bi, 0, fblk)),
                  pl.BlockSpec((n, n), full), pl.BlockSpec((n, n), full),
                  pl.BlockSpec((GROUP_DIM, GROUP_DIM), full), pl.BlockSpec((GROUP_DIM, GROUP_DIM), full)],
        out_specs=pl.BlockSpec((1, n, BRANCH_DIM), lambda bi: (bi, 0, 0)),
        out_shape=jax.ShapeDtypeStruct((b, n, BRANCH_DIM), BF16),
        compiler_params=_params("parallel"), name="dft_ctx",
    )(p_ctx.reshape(b, n, cfg.n_in), cn.astype(BF16), sn.astype(BF16), cc.astype(BF16), sc.astype(BF16))
    return out.reshape(b * n, BRANCH_DIM)


def _sgu_kernel(u_ref, v_ref, g_ref, w_ref, b_ref, o_ref, *, n_chunks):
    gv = g_ref[...]
    for c in range(n_chunks):
        rows = slice(c * CHUNK, (c + 1) * CHUNK)
        u = jax.nn.gelu(u_ref[rows, :].astype(F32))
        v = jax.nn.gelu(v_ref[rows, :].astype(F32))
        vc = v - jnp.mean(v, axis=-1, keepdims=True)
        vn = (vc * lax.rsqrt(jnp.mean(vc * vc, axis=-1, keepdims=True) + EPS) * gv).astype(BF16)
        for g in range(GROUPS):
            sl = slice(g * GROUP_DIM, (g + 1) * GROUP_DIM)
            sv = jnp.dot(w_ref[g], vn[:, sl], preferred_element_type=F32) + b_ref[:, g:g + 1]
            o_ref[rows, sl] = (u[:, sl] * sv).astype(o_ref.dtype)


def _sgu(cfg, p2, sgu_w, sgu_bt, sgu_g):
    t = p2.shape[0]
    tm = min(cfg.tm_sgu, t)
    ublk, vblk = cfg.off_u // BRANCH_DIM, cfg.off_v // BRANCH_DIM
    return pl.pallas_call(
        functools.partial(_sgu_kernel, n_chunks=tm // CHUNK),
        grid=(t // tm,),
        in_specs=[pl.BlockSpec((tm, BRANCH_DIM), lambda i: (i, ublk)),
                  pl.BlockSpec((tm, BRANCH_DIM), lambda i: (i, vblk)),
                  pl.BlockSpec((1, BRANCH_DIM), lambda i: (0, 0)),
                  pl.BlockSpec((GROUPS, CHUNK, CHUNK), lambda i: (0, 0, 0)),
                  pl.BlockSpec((CHUNK, GROUPS), lambda i: (0, 0))],
        out_specs=pl.BlockSpec((tm, BRANCH_DIM), lambda i: (i, 0)),
        out_shape=jax.ShapeDtypeStruct((t, BRANCH_DIM), BF16),
        compiler_params=_params("parallel"), name="spatial_gating",
    )(p2, p2, sgu_g, sgu_w, sgu_bt)


def _merge_kernel(pg0_ref, pg1_ref, pg2_ref, f_ref, s_ref, a_ref, x_ref, bg_ref,
                  wf_ref, ws_ref, wa_ref, wo_ref, g_ref, mod_ref, o_ref, *, d):
    def gate(pg_ref, k):
        return jax.nn.sigmoid(pg_ref[...].astype(F32) + bg_ref[:, k * d:(k + 1) * d])

    merged = gate(pg0_ref, 0) * jnp.dot(f_ref[...], wf_ref[...], preferred_element_type=F32)
    merged += gate(pg1_ref, 1) * jnp.dot(s_ref[...], ws_ref[...], preferred_element_type=F32)
    merged += gate(pg2_ref, 2) * jnp.dot(a_ref[...], wa_ref[...], preferred_element_type=F32)
    y = jnp.dot(merged.astype(BF16), wo_ref[...], preferred_element_type=F32)
    o_ref[...] = x_ref[...] + mod_ref[0][2:3] * _rms(y, g_ref[...])


def _resident(shape):
    return pl.BlockSpec(shape, lambda *_: (0,) * len(shape), pipeline_mode=pl.Buffered(1))


def _merge(cfg, p2, four, sgu, attn, x, b_gate, wf, ws, wa, wo, g, mod, rows_per_mod):
    t, d = x.shape
    tm = cfg.tm_merge
    row = lambda i: (i, 0)
    return pl.pallas_call(
        functools.partial(_merge_kernel, d=d),
        grid=(t // tm,),
        in_specs=[pl.BlockSpec((tm, d), lambda i: (i, 0)),
                  pl.BlockSpec((tm, d), lambda i: (i, 1)),
                  pl.BlockSpec((tm, d), lambda i: (i, 2)),
                  pl.BlockSpec((tm, BRANCH_DIM), row), pl.BlockSpec((tm, BRANCH_DIM), row),
                  pl.BlockSpec((tm, QK_DIM), row), pl.BlockSpec((tm, d), row),
                  _resident((1, 3 * d)), _resident((BRANCH_DIM, d)), _resident((BRANCH_DIM, d)),
                  _resident((QK_DIM, d)), _resident((d, d)), _resident((1, d)),
                  pl.BlockSpec((1, 6, d), lambda i: (i * tm // rows_per_mod, 0, 0))],
        out_specs=pl.BlockSpec((tm, d), row),
        out_shape=jax.ShapeDtypeStruct((t, d), F32),
        compiler_params=_params("parallel"), name="merge",
    )(p2, p2, p2, four, sgu, attn, x, b_gate, wf, ws, wa, wo, g, mod)


def _ffn_up_kernel(eid_ref, nact_ref, a_ref, w1_ref, w3_ref, o_ref, w1b_ref, w3b_ref):
    i = pl.program_id(1)
    active = i < nact_ref[0]
    fresh = jnp.logical_or(i == 0, eid_ref[i] != eid_ref[jnp.maximum(i - 1, 0)])

    @pl.when(jnp.logical_and(active, fresh))
    def _():
        w1b_ref[...] = w1_ref[0].astype(BF16)
        w3b_ref[...] = w3_ref[0].astype(BF16)

    @pl.when(active)
    def _():
        a = a_ref[...]
        h1 = jnp.dot(a, w1b_ref[...], preferred_element_type=F32)
        h3 = jnp.dot(a, w3b_ref[...], preferred_element_type=F32)
        o_ref[...] = (jax.nn.silu(h1) * h3).astype(o_ref.dtype)

    @pl.when(jnp.logical_not(active))
    def _():
        o_ref[...] = jnp.zeros(o_ref.shape, o_ref.dtype)


def _ffn_up(cfg, a, w1, w3, tile_eid, n_active, tm):
    r, d = a.shape
    dff = w1.shape[2]
    tn = cfg.tn_up
    grid_spec = pltpu.PrefetchScalarGridSpec(
        num_scalar_prefetch=2,
        grid=(dff // tn, r // tm),
        in_specs=[pl.BlockSpec((tm, d), lambda j, i, eid, na: (jnp.minimum(i, na[0] - 1), 0)),
                  pl.BlockSpec((1, d, tn), lambda j, i, eid, na: (eid[i], 0, j)),
                  pl.BlockSpec((1, d, tn), lambda j, i, eid, na: (eid[i], 0, j))],
        out_specs=pl.BlockSpec((tm, tn), lambda j, i, eid, na: (i, j)),
        scratch_shapes=[pltpu.VMEM((d, tn), BF16), pltpu.VMEM((d, tn), BF16)],
    )
    return pl.pallas_call(
        _ffn_up_kernel, grid_spec=grid_spec,
        out_shape=jax.ShapeDtypeStruct((r, dff), BF16),
        compiler_params=_params("arbitrary", "arbitrary"), name="swiglu_up",
    )(tile_eid, n_active, a, w1, w3)


def _ffn_down_kernel(eid_ref, nact_ref, *refs, residual):
    if residual:
        h_ref, w_ref, x_ref, g_ref, mod_ref, o_ref, acc_ref = refs
    else:
        h_ref, w_ref, o_ref, acc_ref = refs
    k = pl.program_id(1)

    @pl.when(k == 0)
    def _():
        acc_ref[...] = jnp.zeros(acc_ref.shape, F32)

    @pl.when(pl.program_id(0) < nact_ref[0])
    def _():
        acc_ref[...] += jnp.dot(h_ref[...], w_ref[0], preferred_element_type=F32)

    @pl.when(k == pl.num_programs(1) - 1)
    def _():
        if residual:
            o_ref[...] = x_ref[...] + mod_ref[0][5:6] * _rms(acc_ref[...], g_ref[...])
        else:
            o_ref[...] = acc_ref[...].astype(o_ref.dtype)


def _ffn_down(cfg, h, w2, tile_eid, n_active, tm, tk, x=None, g=None, mod=None, rows_per_mod=None):
    r, dff = h.shape
    d = w2.shape[2]
    residual = x is not None
    in_specs = [pl.BlockSpec((tm, tk), lambda i, k, eid, na: (jnp.minimum(i, na[0] - 1), k)),
                pl.BlockSpec((1, tk, d), lambda i, k, eid, na: (eid[i], k, 0))]
    args = [h, w2]
    if residual:
        in_specs += [pl.BlockSpec((tm, d), lambda i, k, eid, na: (i, 0)),
                     pl.BlockSpec((1, d), lambda i, k, eid, na: (0, 0)),
                     pl.BlockSpec((1, 6, d), lambda i, k, eid, na: (i * tm // rows_per_mod, 0, 0))]
        args += [x, g, mod]
    grid_spec = pltpu.PrefetchScalarGridSpec(
        num_scalar_prefetch=2,
        grid=(r // tm, dff // tk),
        in_specs=in_specs,
        out_specs=pl.BlockSpec((tm, d), lambda i, k, eid, na: (i, 0)),
        scratch_shapes=[pltpu.VMEM((tm, d), F32)],
    )
    return pl.pallas_call(
        functools.partial(_ffn_down_kernel, residual=residual), grid_spec=grid_spec,
        out_shape=jax.ShapeDtypeStruct((r, d), F32 if residual else BF16),
        compiler_params=_params("parallel", "arbitrary"),
        name="swiglu_down_residual" if residual else "swiglu_down",
    )(tile_eid, n_active, *args)


def _ffn_down_cols_kernel(eid_ref, nact_ref, h_ref, w_ref, o_ref, wb_ref):
    i = pl.program_id(1)
    active = i < nact_ref[0]
    fresh = jnp.logical_or(i == 0, eid_ref[i] != eid_ref[jnp.maximum(i - 1, 0)])

    @pl.when(jnp.logical_and(active, fresh))
    def _():
        wb_ref[...] = w_ref[0].astype(BF16)

    @pl.when(active)
    def _():
        o_ref[...] = jnp.dot(h_ref[...], wb_ref[...], preferred_element_type=F32).astype(o_ref.dtype)

    @pl.when(jnp.logical_not(active))
    def _():
        o_ref[...] = jnp.zeros(o_ref.shape, o_ref.dtype)


def _ffn_down_cols(cfg, h, w2, tile_eid, n_active, tm):
    r, dff = h.shape
    d = w2.shape[2]
    tn = cfg.tn_down_cols
    grid_spec = pltpu.PrefetchScalarGridSpec(
        num_scalar_prefetch=2,
        grid=(d // tn, r // tm),
        in_specs=[pl.BlockSpec((tm, dff), lambda n, i, eid, na: (jnp.minimum(i, na[0] - 1), 0)),
                  pl.BlockSpec((1, dff, tn), lambda n, i, eid, na: (eid[i], 0, n))],
        out_specs=pl.BlockSpec((tm, tn), lambda n, i, eid, na: (i, n)),
        scratch_shapes=[pltpu.VMEM((dff, tn), BF16)],
    )
    return pl.pallas_call(
        _ffn_down_cols_kernel, grid_spec=grid_spec,
        out_shape=jax.ShapeDtypeStruct((r, d), BF16),
        compiler_params=_params("arbitrary", "arbitrary"), name="swiglu_down_cols",
    )(tile_eid, n_active, h, w2)


def _combine_kernel(y1_ref, y2_ref, wt_ref, x_ref, g_ref, mod_ref, o_ref):
    wt = wt_ref[...]
    y = wt[:, 0:1] * y1_ref[...].astype(F32) + wt[:, 1:2] * y2_ref[...].astype(F32)
    o_ref[...] = x_ref[...] + mod_ref[0][5:6] * _rms(y, g_ref[...])


def _combine(cfg, y1, y2, wt, x, g, mod, rows_per_mod):
    t, d = x.shape
    tm = min(cfg.tm_norm, t)
    row = lambda i: (i, 0)
    return pl.pallas_call(
        _combine_kernel,
        grid=(t // tm,),
        in_specs=[pl.BlockSpec((tm, d), row), pl.BlockSpec((tm, d), row),
                  pl.BlockSpec((tm, TOP_K), row), pl.BlockSpec((tm, d), row),
                  pl.BlockSpec((1, d), lambda i: (0, 0)),
                  pl.BlockSpec((1, 6, d), lambda i: (i * tm // rows_per_mod, 0, 0))],
        out_specs=pl.BlockSpec((tm, d), row),
        out_shape=jax.ShapeDtypeStruct((t, d), F32),
        compiler_params=_params("parallel"), name="moe_combine",
    )(y1, y2, wt, x, g, mod)


def _dispatch_plan(route, counts, tm):
    t = route.shape[1]
    n_pairs = TOP_K * t
    n_rows = n_pairs + N_EXPERTS * tm
    e_flat = route[:TOP_K].reshape(n_pairs)
    rank = route[TOP_K:2 * TOP_K].reshape(n_pairs)
    padded = (counts + tm - 1) // tm * tm
    ends = jnp.cumsum(padded)
    starts = ends - padded
    onehot = e_flat[:, None] == jnp.arange(N_EXPERTS, dtype=jnp.int32)[None, :]
    dest = jnp.sum(jnp.where(onehot, starts[None, :], 0), axis=1) + rank
    tok = jnp.tile(jnp.arange(t, dtype=jnp.int32), TOP_K)
    row_tok = jnp.zeros((n_rows,), jnp.int32).at[dest].set(tok)
    tile_start = jnp.arange(n_rows // tm, dtype=jnp.int32) * tm
    tile_eid = jnp.minimum(jnp.sum((tile_start[:, None] >= ends[None, :]).astype(jnp.int32), axis=1),
                           N_EXPERTS - 1)
    n_active = (ends[-1] // tm).astype(jnp.int32).reshape(1)
    return row_tok, dest.reshape(TOP_K, t), tile_eid, n_active


def _dense_plan(t, tm):
    return jnp.zeros((t // tm,), jnp.int32), jnp.full((1,), t // tm, jnp.int32)


def _permute_w_in(w):
    b3 = 3 * BRANCH_DIM
    q3 = 3 * QK_DIM
    return jnp.concatenate([w[:, b3 + q3:], w[:, b3:b3 + q3], w[:, :b3]], axis=1).astype(BF16)


def _forward(cfg, x, c, ctx, c_ctx, w_mod, b_mod, g_norm, w_in, b_gate, w_fourier_out, w_sgu_out,
             w_attn_out, w_o, sgu_w, sgu_b, sgu_g, diff_lambda, diff_subln_g,
             ffn_w1, ffn_w3, ffn_w2, router_w, moe_w1, moe_w3, moe_w2):
    b, s, d = x.shape
    n_ctx = ctx.shape[1]
    t_lat, t_ctx = b * s, b * n_ctx
    cos, sin = _rope_tables(s)
    fft_tables = _fft_tables(s)

    c_rows = jnp.concatenate([c, c_ctx[None, :], jnp.zeros((8 - b - 1, d), F32)], axis=0)
    mod_all = _modulation(cfg, c_rows, w_mod, b_mod)

    xl = x.reshape(t_lat, d)
    xc = ctx.reshape(t_ctx, d)
    for l in range(cfg.depth):
        last = l == cfg.depth - 1
        lam_init = 0.8 - 0.6 * math.exp(-0.3 * l)
        mod_l = mod_all[l, :b].reshape(b, 6, d)
        mod_c = mod_all[l, b:b + 1].reshape(1, 6, d)
        g = g_norm[l].reshape(4, 1, d)
        w_in_l = _permute_w_in(w_in[l])
        wf, ws = w_fourier_out[l].astype(BF16), w_sgu_out[l].astype(BF16)
        wa, wo = w_attn_out[l].astype(BF16), w_o[l].astype(BF16)
        bg = b_gate[l].reshape(1, 3 * d)
        sw = sgu_w[l].astype(BF16)
        sbt = sgu_b[l].T
        sg = sgu_g[l].reshape(1, BRANCH_DIM)
        dl = diff_lambda[l]
        gsub = diff_subln_g[l].reshape(1, PAIR)
        pair_blk = lambda off: off // PAIR

        p_lat = _in_proj(cfg, xl, g[0], mod_l, s, w_in_l)
        p_lat3 = p_lat.reshape(b, s, cfg.n_in)
        qk_rot = _rope(cfg, p_lat3, cos, sin)
        if last:
            p_ctx3 = _in_proj(cfg, xc, g[0], mod_c, t_ctx,
                              w_in_l[:, cfg.off_k:cfg.off_f]).reshape(b, n_ctx, 2 * QK_DIM)
            kc_blk, vc_blk = 0, pair_blk(QK_DIM)
        else:
            p_ctx = _in_proj(cfg, xc, g[0], mod_c, t_ctx, w_in_l)
            p_ctx3 = p_ctx.reshape(b, n_ctx, cfg.n_in)
            kc_blk, vc_blk = pair_blk(cfg.off_k), pair_blk(cfg.off_va)
        al = _attention(cfg, qk_rot[0], 0, p_ctx3, kc_blk, p_ctx3, vc_blk, dl, gsub, lam_init, cfg.tq,
                        k_lat=qk_rot, v_arr=p_lat3, v_blk=pair_blk(cfg.off_va))
        four_l = _fourier_latent(cfg, p_lat, fft_tables)
        sgu_l = _sgu(cfg, p_lat, sw, sbt, sg)
        xl = _merge(cfg, p_lat, four_l, sgu_l, al.reshape(t_lat, QK_DIM), xl, bg, wf, ws, wa, wo, g[1], mod_l, s)
        if not last:
            ac = _attention(cfg, p_ctx3, pair_blk(cfg.off_q), p_ctx3, kc_blk, p_ctx3, vc_blk, dl, gsub,
                            lam_init, n_ctx)
            four_c = _fourier_ctx(cfg, p_ctx)
            sgu_c = _sgu(cfg, p_ctx, sw, sbt, sg)
            xc = _merge(cfg, p_ctx, four_c, sgu_c, ac.reshape(t_ctx, QK_DIM), xc, bg, wf, ws, wa, wo, g[1],
                        mod_c, t_ctx)

        i = l // 2
        streams = [(xl, mod_l, s)] + ([] if last else [(xc, mod_c, t_ctx)])
        outs = []
        if l % 2 == 0:
            w2 = ffn_w2[i][None].astype(BF16)
            for xs, mod, rpm in streams:
                t = xs.shape[0]
                fl = _norm_mod(cfg, xs, g[2], mod, rpm, 3, 4)
                tm_up = min(cfg.tm_up, t)
                eid_up, n_up = _dense_plan(t, tm_up)
                h = _ffn_up(cfg, fl, ffn_w1, ffn_w3, eid_up + i, n_up, tm_up)
                tm_dn = min(cfg.tm_down_dense, t)
                outs.append(_ffn_down(cfg, h, w2, *_dense_plan(t, tm_dn), tm_dn, cfg.tk_down_dense, x=xs,
                                      g=g[3], mod=mod, rows_per_mod=rpm))
        else:
            w1 = moe_w1.reshape(-1, d, cfg.dff)
            w3 = moe_w3.reshape(-1, d, cfg.dff)
            w2 = moe_w2.reshape(-1, cfg.dff, d)
            rwt = router_w[i].T
            for xs, mod, rpm in streams:
                tm = cfg.tm_down if xs.shape[0] >= N_EXPERTS * cfg.tm_down else cfg.tm_down_ctx
                fl, route, top_w, counts = _norm_mod(cfg, xs, g[2], mod, rpm, 3, 4, router_wt=rwt)
                row_tok, pos, tile_eid, n_active = _dispatch_plan(
                    route, counts[:N_EXPERTS, 0].astype(jnp.int32), tm)
                a = jnp.take(fl, row_tok, axis=0)
                h = _ffn_up(cfg, a, w1, w3, tile_eid + i * N_EXPERTS, n_active, tm)
                tm_dn = min(cfg.tm_down_cols, tm)
                y = _ffn_down_cols(cfg, h, w2, jnp.repeat(tile_eid, tm // tm_dn) + i * N_EXPERTS,
                                   n_active * (tm // tm_dn), tm_dn)
                outs.append(_combine(cfg, jnp.take(y, pos[0], axis=0), jnp.take(y, pos[1], axis=0),
                                     top_w[:TOP_K].T, xs, g[3], mod, rpm))
        xl = outs[0]
        if not last:
            xc = outs[1]
    return xl.reshape(b, s, d)


def kernel(x, c, ctx, c_ctx, w_mod, b_mod, g_norm, w_in, b_gate, w_fourier_out, w_sgu_out, w_attn_out, w_o,
           sgu_w, sgu_b, sgu_g, diff_lambda, diff_subln_g, ffn_w1, ffn_w3, ffn_w2, router_w,
           moe_w1, moe_w3, moe_w2):
    return _forward(PROD, x, c, ctx, c_ctx, w_mod, b_mod, g_norm, w_in, b_gate, w_fourier_out, w_sgu_out,
                    w_attn_out, w_o, sgu_w, sgu_b, sgu_g, diff_lambda, diff_subln_g,
                    ffn_w1, ffn_w3, ffn_w2, router_w, moe_w1, moe_w3, moe_w2)
```

```python
import functools
import math
from typing import NamedTuple

import jax
import jax.numpy as jnp
from jax import lax
from jax.experimental import pallas as pl
from jax.experimental.pallas import tpu as pltpu

F32 = jnp.float32
BF16 = jnp.bfloat16

EPS = 1e-6
GRID_W = 64
ROPE_BASE = 10000.0
HEAD_DIM = 64
HEADS = 8
PAIR = 2 * HEAD_DIM
QK_DIM = HEADS * PAIR
GROUP_DIM = 128
GROUPS = 4
BRANCH_DIM = GROUPS * GROUP_DIM
CHUNK = 128
FFT_N1 = 128
FFT_A_PER_STEP = 4
FFT_B_PER_STEP = 8
N_EXPERTS = 8
TOP_K = 2
EXPERT_ROWS = 8
ATTN_SCALE = HEAD_DIM ** -0.5
Q_SCALE = ATTN_SCALE * math.log2(math.e)
VMEM_LIMIT = 56 * 1024 * 1024


class Cfg(NamedTuple):
    d: int
    batch: int
    seq: int
    ctx: int
    dff: int
    depth: int
    tm_norm: int
    tm_mm: int
    tn_mm: int
    tm_rope: int
    tq: int
    tk: int
    tm_sgu: int
    tm_merge: int
    tm_up: int
    tn_up: int
    tm_down: int
    tm_down_cols: int
    tn_down_cols: int
    tm_down_ctx: int
    moe_chunks: int
    tm_down_dense: int
    tk_down_dense: int
    tn_mod: int

    @property
    def n_in(self):
        return 3 * self.d + 3 * QK_DIM + 3 * BRANCH_DIM

    @property
    def off_q(self):
        return 3 * self.d

    @property
    def off_k(self):
        return self.off_q + QK_DIM

    @property
    def off_va(self):
        return self.off_k + QK_DIM

    @property
    def off_f(self):
        return self.off_va + QK_DIM

    @property
    def off_u(self):
        return self.off_f + BRANCH_DIM

    @property
    def off_v(self):
        return self.off_u + BRANCH_DIM


PROD = Cfg(d=2048, batch=4, seq=8192, ctx=256, dff=5632, depth=4,
           tm_norm=512, tm_mm=1024, tn_mm=1536, tm_rope=512, tq=512, tk=256,
           tm_sgu=1024, tm_merge=256, tm_up=1024, tn_up=512, tm_down=1024, tm_down_cols=512, tn_down_cols=512,
           tm_down_ctx=256, moe_chunks=4, tm_down_dense=1024, tk_down_dense=512, tn_mod=1024)


def _params(*sem):
    return pltpu.CompilerParams(dimension_semantics=sem, vmem_limit_bytes=VMEM_LIMIT)


def _rms(y, g):
    return y * lax.rsqrt(jnp.mean(y * y, axis=-1, keepdims=True) + EPS) * g


def _mod_kernel(c_ref, w_ref, b_ref, o_ref):
    sc = jax.nn.silu(c_ref[...])
    o_ref[0] = jnp.dot(sc.astype(BF16), w_ref[0].astype(BF16), preferred_element_type=F32) + b_ref[0]


def _modulation(cfg, c_rows, w_mod, b_mod):
    depth, d, n6 = w_mod.shape
    tn = cfg.tn_mod
    return pl.pallas_call(
        _mod_kernel,
        grid=(depth, n6 // tn),
        in_specs=[pl.BlockSpec((8, d), lambda l, j: (0, 0)),
                  pl.BlockSpec((1, d, tn), lambda l, j: (l, 0, j)),
                  pl.BlockSpec((1, 1, tn), lambda l, j: (l, 0, j))],
        out_specs=pl.BlockSpec((1, 8, tn), lambda l, j: (l, 0, j)),
        out_shape=jax.ShapeDtypeStruct((depth, 8, n6), F32),
        compiler_params=_params("parallel", "parallel"),
        name="modulation",
    )(c_rows, w_mod, b_mod.reshape(depth, 1, n6))


def _norm_mod_kernel(x_ref, g_ref, mod_ref, o_ref, *, shift_idx, scale_idx):
    y = _rms(x_ref[...], g_ref[...])
    m = mod_ref[0]
    o_ref[...] = (y * (1.0 + m[scale_idx:scale_idx + 1]) + m[shift_idx:shift_idx + 1]).astype(o_ref.dtype)


def _norm_mod_router_kernel(x_ref, g_ref, mod_ref, rw_ref, tri_ref, o_ref, idx_ref, wt_ref, cnt_ref, *,
                            shift_idx, scale_idx):
    @pl.when(pl.program_id(0) == 0)
    def _():
        cnt_ref[...] = jnp.zeros(cnt_ref.shape, F32)

    y = _rms(x_ref[...], g_ref[...])
    m = mod_ref[0]
    h = y * (1.0 + m[scale_idx:scale_idx + 1]) + m[shift_idx:shift_idx + 1]
    o_ref[...] = h.astype(o_ref.dtype)
    logits = lax.dot_general(rw_ref[...], h, (((1,), (1,)), ((), ())),
                             precision=lax.Precision.HIGHEST, preferred_element_type=F32)
    row = lax.broadcasted_iota(jnp.int32, logits.shape, 0).astype(F32)
    m1 = jnp.max(logits, axis=0, keepdims=True)
    i1 = jnp.min(jnp.where(logits == m1, row, float(N_EXPERTS)), axis=0, keepdims=True)
    rest = jnp.where(row == i1, -jnp.inf, logits)
    m2 = jnp.max(rest, axis=0, keepdims=True)
    i2 = jnp.min(jnp.where(rest == m2, row, float(N_EXPERTS)), axis=0, keepdims=True)
    e2 = jnp.exp(m2 - m1)
    w1 = 1.0 / (1.0 + e2)
    w2 = e2 / (1.0 + e2)
    wt_ref[...] = jnp.where(row == 0.0, w1, jnp.where(row == 1.0, w2, 0.0))
    pick1 = jnp.where(row == i1, 1.0, 0.0)
    pick2 = jnp.where(row == i2, 1.0, 0.0)
    pre1 = jnp.dot(pick1.astype(BF16), tri_ref[...], preferred_element_type=F32)
    pre2 = jnp.dot(pick2.astype(BF16), tri_ref[...], preferred_element_type=F32)
    n1 = jnp.sum(pick1, axis=1, keepdims=True)
    n2 = jnp.sum(pick2, axis=1, keepdims=True)
    seen = cnt_ref[:, 0:1]
    r1 = jnp.sum(pick1 * (pre1 - 1.0 + seen), axis=0, keepdims=True)
    r2 = jnp.sum(pick2 * (pre2 - 1.0 + seen + n1), axis=0, keepdims=True)
    cnt_ref[...] = cnt_ref[...] + (n1 + n2)
    idx_ref[...] = jnp.where(row == 0.0, i1, jnp.where(row == 1.0, i2, jnp.where(row == 2.0, r1, jnp.where(
        row == 3.0, r2, 0.0)))).astype(jnp.int32)


def _norm_mod(cfg, x, g, mod, rows_per_mod, shift_idx, scale_idx, router_wt=None):
    t, d = x.shape
    tm = min(cfg.tm_norm, t)
    in_specs = [pl.BlockSpec((tm, d), lambda i: (i, 0)),
                pl.BlockSpec((1, d), lambda i: (0, 0)),
                pl.BlockSpec((1, 6, d), lambda i: (i * tm // rows_per_mod, 0, 0))]
    h_spec = pl.BlockSpec((tm, d), lambda i: (i, 0))
    h_shape = jax.ShapeDtypeStruct((t, d), BF16)
    if router_wt is None:
        return pl.pallas_call(
            functools.partial(_norm_mod_kernel, shift_idx=shift_idx, scale_idx=scale_idx),
            grid=(t // tm,), in_specs=in_specs, out_specs=h_spec, out_shape=h_shape,
            compiler_params=_params("parallel"), name="norm_mod",
        )(x, g, mod)
    r_spec = pl.BlockSpec((EXPERT_ROWS, tm), lambda i: (0, i))
    tri = jnp.triu(jnp.ones((tm, tm), BF16))
    return pl.pallas_call(
        functools.partial(_norm_mod_router_kernel, shift_idx=shift_idx, scale_idx=scale_idx),
        grid=(t // tm,),
        in_specs=in_specs + [pl.BlockSpec((N_EXPERTS, d), lambda i: (0, 0)),
                             pl.BlockSpec((tm, tm), lambda i: (0, 0))],
        out_specs=[h_spec, r_spec, r_spec, pl.BlockSpec((EXPERT_ROWS, 128), lambda i: (0, 0))],
        out_shape=[h_shape, jax.ShapeDtypeStruct((EXPERT_ROWS, t), jnp.int32),
                   jax.ShapeDtypeStruct((EXPERT_ROWS, t), F32),
                   jax.ShapeDtypeStruct((EXPERT_ROWS, 128), F32)],
        compiler_params=_params("arbitrary"), name="norm_mod_router",
    )(x, g, mod, router_wt, tri)


def _in_proj_kernel(x_ref, g_ref, mod_ref, w_ref, o_ref, h_ref):
    @pl.when(pl.program_id(1) == 0)
    def _():
        m = mod_ref[0]
        h_ref[...] = (_rms(x_ref[...], g_ref[...]) * (1.0 + m[1:2]) + m[0:1]).astype(h_ref.dtype)

    o_ref[...] = jnp.dot(h_ref[...], w_ref[...], preferred_element_type=F32).astype(o_ref.dtype)


def _in_proj(cfg, x, g, mod, rows_per_mod, w):
    t, d = x.shape
    n = w.shape[1]
    tm = min(cfg.tm_mm, t)
    tn = cfg.tn_mm if n % cfg.tn_mm == 0 else 512
    return pl.pallas_call(
        _in_proj_kernel,
        grid=(t // tm, n // tn),
        in_specs=[pl.BlockSpec((tm, d), lambda i, j: (i, 0)),
                  pl.BlockSpec((1, d), lambda i, j: (0, 0)),
                  pl.BlockSpec((1, 6, d), lambda i, j: (i * tm // rows_per_mod, 0, 0)),
                  pl.BlockSpec((d, tn), lambda i, j: (0, j))],
        out_specs=pl.BlockSpec((tm, tn), lambda i, j: (i, j)),
        out_shape=jax.ShapeDtypeStruct((t, n), BF16),
        scratch_shapes=[pltpu.VMEM((tm, d), BF16)],
        compiler_params=_params("parallel", "arbitrary"), name="in_proj",
    )(x, g, mod, w)


def _rope_tables(seq):
    rows = seq // GRID_W
    r = jnp.repeat(jnp.arange(rows, dtype=F32), GRID_W)
    col = jnp.tile(jnp.arange(GRID_W, dtype=F32), rows)
    n_freq = HEAD_DIM // 4
    inv = ROPE_BASE ** (-jnp.arange(n_freq, dtype=F32) / n_freq)
    ang = jnp.concatenate([r[:, None] * inv, col[:, None] * inv], axis=-1)
    cos, sin = jnp.cos(ang), jnp.sin(ang)
    return jnp.tile(cos, (1, 4)), jnp.tile(jnp.concatenate([-sin, sin], axis=-1), (1, 2))


def _rope_kernel(p_ref, cos_ref, sin_ref, o_ref):
    scale = jnp.where(pl.program_id(2) == 0, Q_SCALE, 1.0).astype(F32)
    cos = cos_ref[...] * scale
    sin = sin_ref[...] * scale
    lane = lax.broadcasted_iota(jnp.int32, cos.shape, 1)
    first_half = (lane % HEAD_DIM) < (HEAD_DIM // 2)
    for c in range(QK_DIM // PAIR):
        t = p_ref[0, :, c * PAIR:(c + 1) * PAIR].astype(F32)
        partner = jnp.where(first_half, pltpu.roll(t, PAIR - HEAD_DIM // 2, 1), pltpu.roll(t, HEAD_DIM // 2, 1))
        o_ref[0, 0, :, c * PAIR:(c + 1) * PAIR] = (t * cos + partner * sin).astype(o_ref.dtype)


def _rope(cfg, p3, cos, sin):
    b, s, _ = p3.shape
    tm = cfg.tm_rope
    qblk = cfg.off_q // QK_DIM
    return pl.pallas_call(
        _rope_kernel,
        grid=(b, s // tm, 2),
        in_specs=[pl.BlockSpec((1, tm, QK_DIM), lambda bi, i, j: (bi, i, qblk + j)),
                  pl.BlockSpec((tm, PAIR), lambda bi, i, j: (i, 0)),
                  pl.BlockSpec((tm, PAIR), lambda bi, i, j: (i, 0))],
        out_specs=pl.BlockSpec((1, 1, tm, QK_DIM), lambda bi, i, j: (j, bi, i, 0)),
        out_shape=jax.ShapeDtypeStruct((2, b, s, QK_DIM), BF16),
        compiler_params=_params("parallel", "parallel", "parallel"), name="rope",
    )(p3, cos, sin)


def _attn_kernel(*refs, tq, tk, n_lat, q_prescaled):
    if n_lat:
        q_ref, kc_ref, vc_ref, k_ref, v_ref, dl_ref, g_ref, li_ref, o_ref, q2_ref, vx_ref = refs
    else:
        q_ref, kc_ref, vc_ref, dl_ref, g_ref, li_ref, o_ref, q2_ref, vx_ref = refs
    lc = kc_ref.shape[1]

    @pl.when(pl.program_id(2) == 0)
    def _():
        vx_ref[:lc, :PAIR] = vc_ref[0]
        vx_ref[:lc, PAIR:] = jnp.ones((lc, PAIR), BF16)
        if n_lat:
            vx_ref[lc:, :PAIR] = v_ref[0]
            vx_ref[lc:, PAIR:] = jnp.ones((n_lat * tk, PAIR), BF16)

    q = q_ref[0]
    if not q_prescaled:
        q = (q.astype(F32) * Q_SCALE).astype(BF16)
    lane = lax.broadcasted_iota(jnp.int32, q.shape, 1)
    zero = jnp.zeros_like(q)
    q2_ref[:tq] = jnp.where(lane < HEAD_DIM, q, zero)
    q2_ref[tq:] = jnp.where(lane >= HEAD_DIM, q, zero)
    q2 = q2_ref[...]

    chunks = [(kc_ref[0], 0, lc)]
    for j in range(n_lat):
        chunks.append((k_ref[0, 0, j * tk:(j + 1) * tk, :], lc + j * tk, tk))
    m = jnp.full((2 * tq, 1), -jnp.inf, F32)
    acc = jnp.zeros((2 * tq, 2 * PAIR), F32)
    for k, off, n in chunks:
        s = lax.dot_general(q2, k, (((1,), (1,)), ((), ())), preferred_element_type=F32)
        m_new = jnp.maximum(m, jnp.max(s, axis=-1, keepdims=True))
        p = jnp.exp2(s - m_new).astype(BF16)
        acc = jnp.exp2(m - m_new) * acc + jnp.dot(p, vx_ref[off:off + n, :], preferred_element_type=F32)
        m = m_new

    o12 = acc[:, :PAIR] / acc[:, PAIR:]
    dl = dl_ref[...]
    lam_init = li_ref[...]
    lam = (jnp.exp(jnp.sum(dl[0:1] * dl[1:2], axis=-1, keepdims=True))
           - jnp.exp(jnp.sum(dl[2:3] * dl[3:4], axis=-1, keepdims=True)) + lam_init)
    o = o12[:tq] - lam * o12[tq:]
    o_ref[0] = (_rms(o, g_ref[...]) * (1.0 - lam_init)).astype(o_ref.dtype)


def _attention(cfg, q_arr, q_blk, kc_arr, kc_blk, vc_arr, vc_blk, dl, g_sub, lam_init, tq,
               k_lat=None, v_arr=None, v_blk=0):
    b, lq = q_arr.shape[0], q_arr.shape[1]
    lc = kc_arr.shape[1]
    in_specs = [pl.BlockSpec((1, tq, PAIR), lambda bi, h, i: (bi, i, q_blk + h)),
                pl.BlockSpec((1, lc, PAIR), lambda bi, h, i: (bi, 0, kc_blk + h)),
                pl.BlockSpec((1, lc, PAIR), lambda bi, h, i: (bi, 0, vc_blk + h))]
    args = [q_arr, kc_arr, vc_arr]
    n_lat = 0
    if k_lat is not None:
        s = k_lat.shape[2]
        n_lat = s // cfg.tk
        in_specs += [pl.BlockSpec((1, 1, s, PAIR), lambda bi, h, i: (1, bi, 0, h)),
                     pl.BlockSpec((1, s, PAIR), lambda bi, h, i: (bi, 0, v_blk + h))]
        args += [k_lat, v_arr]
    in_specs += [pl.BlockSpec((4, HEAD_DIM), lambda bi, h, i: (0, 0)),
                 pl.BlockSpec((1, PAIR), lambda bi, h, i: (0, 0)),
                 pl.BlockSpec((1, PAIR), lambda bi, h, i: (0, 0))]
    args += [dl, g_sub, jnp.full((1, PAIR), lam_init, F32)]
    lk = lc + n_lat * cfg.tk
    return pl.pallas_call(
        functools.partial(_attn_kernel, tq=tq, tk=cfg.tk, n_lat=n_lat, q_prescaled=k_lat is not None),
        grid=(b, HEADS, lq // tq),
        in_specs=in_specs,
        out_specs=pl.BlockSpec((1, tq, PAIR), lambda bi, h, i: (bi, i, h)),
        out_shape=jax.ShapeDtypeStruct((b, lq, QK_DIM), BF16),
        scratch_shapes=[pltpu.VMEM((2 * tq, PAIR), BF16), pltpu.VMEM((lk, 2 * PAIR), BF16)],
        compiler_params=_params("parallel", "parallel", "arbitrary"),
        name="diff_attention" if n_lat else "diff_attention_ctx",
    )(*args)


def _dft_cos_sin(n, scale=1.0):
    j = jnp.arange(n, dtype=jnp.int32)
    ang = ((j[:, None] * j[None, :]) % n).astype(F32) * (2.0 * math.pi / n)
    return jnp.cos(ang) * scale, jnp.sin(ang) * scale


def _fft_tables(seq):
    n1, n2 = FFT_N1, seq // FFT_N1
    c1, s1 = _dft_cos_sin(n1)
    stage1 = jnp.concatenate([c1, -s1], axis=0).astype(BF16)
    k1 = jnp.arange(n1, dtype=jnp.int32)
    m = jnp.arange(n2, dtype=jnp.int32)
    ang = (m[:, None] * k1[None, :]).astype(F32) * (2.0 * math.pi / seq)
    tw_cos, tw_sin = jnp.cos(ang)[:, :, None], jnp.sin(ang)[:, :, None]
    c2, s2 = _dft_cos_sin(n2)
    stage2 = jnp.concatenate([jnp.concatenate([c2, s2], axis=1),
                              jnp.concatenate([-s2, c2], axis=1)], axis=0).astype(BF16)
    cc, sc = _dft_cos_sin(GROUP_DIM, scale=(seq * GROUP_DIM) ** -0.5)
    return stage1, tw_cos, tw_sin, stage2, cc.astype(BF16), sc.astype(BF16)


def _fft_a_kernel(x_ref, w_ref, c_ref, s_ref, o_ref):
    for t in range(FFT_A_PER_STEP):
        x = x_ref[0, :, t * BRANCH_DIM:(t + 1) * BRANCH_DIM]
        a = jnp.dot(w_ref[...], x, preferred_element_type=F32)
        re, im = a[:FFT_N1], a[FFT_N1:]
        c, s = c_ref[t], s_ref[t]
        base = 2 * t * BRANCH_DIM
        o_ref[0, :, base:base + BRANCH_DIM] = (re * c + im * s).astype(o_ref.dtype)
        o_ref[0, :, base + BRANCH_DIM:base + 2 * BRANCH_DIM] = (im * c - re * s).astype(o_ref.dtype)


def _fft_b_kernel(x_ref, w_ref, cc_ref, sc_ref, o_ref, *, n2):
    for t in range(FFT_B_PER_STEP):
        x = x_ref[0, t]
        z = jnp.concatenate([x[:, :BRANCH_DIM], x[:, BRANCH_DIM:]], axis=0)
        y = jnp.dot(w_ref[...], z, preferred_element_type=F32).astype(BF16)
        for g in range(GROUPS):
            sl = slice(g * GROUP_DIM, (g + 1) * GROUP_DIM)
            o_ref[0, :, t * BRANCH_DIM + g * GROUP_DIM:t * BRANCH_DIM + (g + 1) * GROUP_DIM] = (
                jnp.dot(y[:n2, sl], cc_ref[...], preferred_element_type=F32)
                + jnp.dot(y[n2:, sl], sc_ref[...], preferred_element_type=F32)).astype(o_ref.dtype)


def _fourier_latent(cfg, p_lat, tables):
    b, s = cfg.batch, cfg.seq
    n1, n2 = FFT_N1, s // FFT_N1
    stage1, tw_cos, tw_sin, stage2, cc, sc = tables
    f = p_lat[:, cfg.off_f:cfg.off_f + BRANCH_DIM]
    pa, pb = FFT_A_PER_STEP, FFT_B_PER_STEP
    mid = pl.pallas_call(
        _fft_a_kernel,
        grid=(b, n2 // pa),
        in_specs=[pl.BlockSpec((1, n1, pa * BRANCH_DIM), lambda bi, m: (bi, 0, m)),
                  pl.BlockSpec((2 * n1, n1), lambda bi, m: (0, 0)),
                  pl.BlockSpec((pa, n1, 1), lambda bi, m: (m, 0, 0)),
                  pl.BlockSpec((pa, n1, 1), lambda bi, m: (m, 0, 0))],
        out_specs=pl.BlockSpec((1, n1, pa * 2 * BRANCH_DIM), lambda bi, m: (bi, 0, m)),
        out_shape=jax.ShapeDtypeStruct((b, n1, n2 * 2 * BRANCH_DIM), BF16),
        compiler_params=_params("parallel", "parallel"), name="fft_stage_a",
    )(f.reshape(b, n1, n2 * BRANCH_DIM), stage1, tw_cos, tw_sin)
    out = pl.pallas_call(
        functools.partial(_fft_b_kernel, n2=n2),
        grid=(b, n1 // pb),
        in_specs=[pl.BlockSpec((1, pb, n2, 2 * BRANCH_DIM), lambda bi, k: (bi, k, 0, 0)),
                  pl.BlockSpec((2 * n2, 2 * n2), lambda bi, k: (0, 0)),
                  pl.BlockSpec((GROUP_DIM, GROUP_DIM), lambda bi, k: (0, 0)),
                  pl.BlockSpec((GROUP_DIM, GROUP_DIM), lambda bi, k: (0, 0))],
        out_specs=pl.BlockSpec((1, n2, pb * BRANCH_DIM), lambda bi, k: (bi, 0, k)),
        out_shape=jax.ShapeDtypeStruct((b, n2, n1 * BRANCH_DIM), BF16),
        compiler_params=_params("parallel", "parallel"), name="fft_stage_b",
    )(mid.reshape(b, n1, n2, 2 * BRANCH_DIM), stage2, cc, sc)
    return out.reshape(b * s, BRANCH_DIM)


def _dft_dense_kernel(x_ref, c_ref, s_ref, cc_ref, sc_ref, o_ref):
    x = x_ref[0]
    yr = jnp.dot(c_ref[...], x, preferred_element_type=F32).astype(BF16)
    yi = (-jnp.dot(s_ref[...], x, preferred_element_type=F32)).astype(BF16)
    for g in range(GROUPS):
        sl = slice(g * GROUP_DIM, (g + 1) * GROUP_DIM)
        o_ref[0, :, sl] = (jnp.dot(yr[:, sl], cc_ref[...], preferred_element_type=F32)
                           + jnp.dot(yi[:, sl], sc_ref[...], preferred_element_type=F32)).astype(o_ref.dtype)


def _fourier_ctx(cfg, p_ctx):
    b, n = cfg.batch, cfg.ctx
    cn, sn = _dft_cos_sin(n)
    cc, sc = _dft_cos_sin(GROUP_DIM, scale=(n * GROUP_DIM) ** -0.5)
    fblk = cfg.off_f // BRANCH_DIM
    full = lambda bi: (0, 0)
    out = pl.pallas_call(
        _dft_dense_kernel,
        grid=(b,),
        in_specs=[pl.BlockSpec((1, n, BRANCH_DIM), lambda bi: (bi, 0, fblk)),
                  pl.BlockSpec((n, n), full), pl.BlockSpec((n, n), full),
                  pl.BlockSpec((GROUP_DIM, GROUP_DIM), full), pl.BlockSpec((GROUP_DIM, GROUP_DIM), full)],
        out_specs=pl.BlockSpec((1, n, BRANCH_DIM), lambda bi: (bi, 0, 0)),
        out_shape=jax.ShapeDtypeStruct((b, n, BRANCH_DIM), BF16),
        compiler_params=_params("parallel"), name="dft_ctx",
    )(p_ctx.reshape(b, n, cfg.n_in), cn.astype(BF16), sn.astype(BF16), cc.astype(BF16), sc.astype(BF16))
    return out.reshape(b * n, BRANCH_DIM)


def _sgu_kernel(u_ref, v_ref, g_ref, w_ref, b_ref, o_ref, *, n_chunks):
    gv = g_ref[...]
    for c in range(n_chunks):
        rows = slice(c * CHUNK, (c + 1) * CHUNK)
        u = jax.nn.gelu(u_ref[rows, :].astype(F32))
        v = jax.nn.gelu(v_ref[rows, :].astype(F32))
        vc = v - jnp.mean(v, axis=-1, keepdims=True)
        vn = (vc * lax.rsqrt(jnp.mean(vc * vc, axis=-1, keepdims=True) + EPS) * gv).astype(BF16)
        for g in range(GROUPS):
            sl = slice(g * GROUP_DIM, (g + 1) * GROUP_DIM)
            sv = jnp.dot(w_ref[g], vn[:, sl], preferred_element_type=F32) + b_ref[:, g:g + 1]
            o_ref[rows, sl] = (u[:, sl] * sv).astype(o_ref.dtype)


def _sgu(cfg, p2, sgu_w, sgu_bt, sgu_g):
    t = p2.shape[0]
    tm = min(cfg.tm_sgu, t)
    ublk, vblk = cfg.off_u // BRANCH_DIM, cfg.off_v // BRANCH_DIM
    return pl.pallas_call(
        functools.partial(_sgu_kernel, n_chunks=tm // CHUNK),
        grid=(t // tm,),
        in_specs=[pl.BlockSpec((tm, BRANCH_DIM), lambda i: (i, ublk)),
                  pl.BlockSpec((tm, BRANCH_DIM), lambda i: (i, vblk)),
                  pl.BlockSpec((1, BRANCH_DIM), lambda i: (0, 0)),
                  pl.BlockSpec((GROUPS, CHUNK, CHUNK), lambda i: (0, 0, 0)),
                  pl.BlockSpec((CHUNK, GROUPS), lambda i: (0, 0))],
        out_specs=pl.BlockSpec((tm, BRANCH_DIM), lambda i: (i, 0)),
        out_shape=jax.ShapeDtypeStruct((t, BRANCH_DIM), BF16),
        compiler_params=_params("parallel"), name="spatial_gating",
    )(p2, p2, sgu_g, sgu_w, sgu_bt)


def _merge_kernel(pg0_ref, pg1_ref, pg2_ref, f_ref, s_ref, a_ref, x_ref, bg_ref,
                  wf_ref, ws_ref, wa_ref, wo_ref, g_ref, mod_ref, o_ref, *, d):
    def gate(pg_ref, k):
        return jax.nn.sigmoid(pg_ref[...].astype(F32) + bg_ref[:, k * d:(k + 1) * d])

    merged = gate(pg0_ref, 0) * jnp.dot(f_ref[...], wf_ref[...], preferred_element_type=F32)
    merged += gate(pg1_ref, 1) * jnp.dot(s_ref[...], ws_ref[...], preferred_element_type=F32)
    merged += gate(pg2_ref, 2) * jnp.dot(a_ref[...], wa_ref[...], preferred_element_type=F32)
    y = jnp.dot(merged.astype(BF16), wo_ref[...], preferred_element_type=F32)
    o_ref[...] = x_ref[...] + mod_ref[0][2:3] * _rms(y, g_ref[...])


def _resident(shape):
    return pl.BlockSpec(shape, lambda *_: (0,) * len(shape), pipeline_mode=pl.Buffered(1))


def _merge(cfg, p2, four, sgu, attn, x, b_gate, wf, ws, wa, wo, g, mod, rows_per_mod):
    t, d = x.shape
    tm = cfg.tm_merge
    row = lambda i: (i, 0)
    return pl.pallas_call(
        functools.partial(_merge_kernel, d=d),
        grid=(t // tm,),
        in_specs=[pl.BlockSpec((tm, d), lambda i: (i, 0)),
                  pl.BlockSpec((tm, d), lambda i: (i, 1)),
                  pl.BlockSpec((tm, d), lambda i: (i, 2)),
                  pl.BlockSpec((tm, BRANCH_DIM), row), pl.BlockSpec((tm, BRANCH_DIM), row),
                  pl.BlockSpec((tm, QK_DIM), row), pl.BlockSpec((tm, d), row),
                  _resident((1, 3 * d)), _resident((BRANCH_DIM, d)), _resident((BRANCH_DIM, d)),
                  _resident((QK_DIM, d)), _resident((d, d)), _resident((1, d)),
                  pl.BlockSpec((1, 6, d), lambda i: (i * tm // rows_per_mod, 0, 0))],
        out_specs=pl.BlockSpec((tm, d), row),
        out_shape=jax.ShapeDtypeStruct((t, d), F32),
        compiler_params=_params("parallel"), name="merge",
    )(p2, p2, p2, four, sgu, attn, x, b_gate, wf, ws, wa, wo, g, mod)


def _last_active(i, n_active_ref):
    return jnp.minimum(i, jnp.maximum(n_active_ref[0] - 1, 0))


def _ffn_up_kernel(eid_ref, nact_ref, a_ref, w1_ref, w3_ref, o_ref, w1b_ref, w3b_ref):
    i = pl.program_id(1)
    active = i < nact_ref[0]
    fresh = jnp.logical_or(i == 0, eid_ref[i] != eid_ref[jnp.maximum(i - 1, 0)])

    @pl.when(jnp.logical_and(active, fresh))
    def _():
        w1b_ref[...] = w1_ref[0].astype(BF16)
        w3b_ref[...] = w3_ref[0].astype(BF16)

    @pl.when(active)
    def _():
        a = a_ref[...]
        h1 = jnp.dot(a, w1b_ref[...], preferred_element_type=F32)
        h3 = jnp.dot(a, w3b_ref[...], preferred_element_type=F32)
        o_ref[...] = (jax.nn.silu(h1) * h3).astype(o_ref.dtype)

    @pl.when(jnp.logical_not(active))
    def _():
        o_ref[...] = jnp.zeros(o_ref.shape, o_ref.dtype)


def _ffn_up(cfg, a, w1, w3, tile_eid, n_active, tm):
    r, d = a.shape
    dff = w1.shape[2]
    tn = cfg.tn_up
    grid_spec = pltpu.PrefetchScalarGridSpec(
        num_scalar_prefetch=2,
        grid=(dff // tn, r // tm),
        in_specs=[pl.BlockSpec((tm, d), lambda j, i, eid, na: (_last_active(i, na), 0)),
                  pl.BlockSpec((1, d, tn), lambda j, i, eid, na: (eid[i], 0, j)),
                  pl.BlockSpec((1, d, tn), lambda j, i, eid, na: (eid[i], 0, j))],
        out_specs=pl.BlockSpec((tm, tn), lambda j, i, eid, na: (i, j)),
        scratch_shapes=[pltpu.VMEM((d, tn), BF16), pltpu.VMEM((d, tn), BF16)],
    )
    return pl.pallas_call(
        _ffn_up_kernel, grid_spec=grid_spec,
        out_shape=jax.ShapeDtypeStruct((r, dff), BF16),
        compiler_params=_params("arbitrary", "arbitrary"), name="swiglu_up",
    )(tile_eid, n_active, a, w1, w3)


def _ffn_down_kernel(eid_ref, nact_ref, *refs, residual):
    if residual:
        h_ref, w_ref, x_ref, g_ref, mod_ref, o_ref, acc_ref = refs
    else:
        h_ref, w_ref, o_ref, acc_ref = refs
    k = pl.program_id(1)

    @pl.when(k == 0)
    def _():
        acc_ref[...] = jnp.zeros(acc_ref.shape, F32)

    @pl.when(pl.program_id(0) < nact_ref[0])
    def _():
        acc_ref[...] += jnp.dot(h_ref[...], w_ref[0], preferred_element_type=F32)

    @pl.when(k == pl.num_programs(1) - 1)
    def _():
        if residual:
            o_ref[...] = x_ref[...] + mod_ref[0][5:6] * _rms(acc_ref[...], g_ref[...])
        else:
            o_ref[...] = acc_ref[...].astype(o_ref.dtype)


def _ffn_down(cfg, h, w2, tile_eid, n_active, tm, tk, x=None, g=None, mod=None, rows_per_mod=None):
    r, dff = h.shape
    d = w2.shape[2]
    residual = x is not None
    in_specs = [pl.BlockSpec((tm, tk), lambda i, k, eid, na: (_last_active(i, na), k)),
                pl.BlockSpec((1, tk, d), lambda i, k, eid, na: (eid[i], k, 0))]
    args = [h, w2]
    if residual:
        in_specs += [pl.BlockSpec((tm, d), lambda i, k, eid, na: (i, 0)),
                     pl.BlockSpec((1, d), lambda i, k, eid, na: (0, 0)),
                     pl.BlockSpec((1, 6, d), lambda i, k, eid, na: (i * tm // rows_per_mod, 0, 0))]
        args += [x, g, mod]
    grid_spec = pltpu.PrefetchScalarGridSpec(
        num_scalar_prefetch=2,
        grid=(r // tm, dff // tk),
        in_specs=in_specs,
        out_specs=pl.BlockSpec((tm, d), lambda i, k, eid, na: (i, 0)),
        scratch_shapes=[pltpu.VMEM((tm, d), F32)],
    )
    return pl.pallas_call(
        functools.partial(_ffn_down_kernel, residual=residual), grid_spec=grid_spec,
        out_shape=jax.ShapeDtypeStruct((r, d), F32 if residual else BF16),
        compiler_params=_params("parallel", "arbitrary"),
        name="swiglu_down_residual" if residual else "swiglu_down",
    )(tile_eid, n_active, *args)


def _ffn_down_cols_kernel(eid_ref, nact_ref, h_ref, w_ref, *refs):
    o_ref, wb_ref = refs[-2:]
    i = pl.program_id(1)
    active = i < nact_ref[0]
    fresh = jnp.logical_or(i == 0, eid_ref[i] != eid_ref[jnp.maximum(i - 1, 0)])

    @pl.when(jnp.logical_and(active, fresh))
    def _():
        wb_ref[...] = w_ref[0].astype(BF16)

    @pl.when(active)
    def _():
        o_ref[...] = jnp.dot(h_ref[...], wb_ref[...], preferred_element_type=F32).astype(o_ref.dtype)

    @pl.when(jnp.logical_not(active))
    def _():
        o_ref[...] = jnp.zeros(o_ref.shape, o_ref.dtype)


def _ffn_down_cols(cfg, h, w2, tile_eid, n_active, tm, y_prev=None, tile_off=0, total_rows=None):
    r, dff = h.shape
    d = w2.shape[2]
    tn = cfg.tn_down_cols
    in_specs = [pl.BlockSpec((tm, dff), lambda n, i, eid, na: (_last_active(i, na), 0)),
                pl.BlockSpec((1, dff, tn), lambda n, i, eid, na: (eid[i], 0, n))]
    args = [tile_eid, n_active, h, w2]
    aliases = {}
    if y_prev is not None:
        in_specs.append(pl.BlockSpec(memory_space=pl.ANY))
        args.append(y_prev)
        aliases = {len(args) - 1: 0}
    grid_spec = pltpu.PrefetchScalarGridSpec(
        num_scalar_prefetch=2,
        grid=(d // tn, r // tm),
        in_specs=in_specs,
        out_specs=pl.BlockSpec((tm, tn), lambda n, i, eid, na: (i + tile_off, n)),
        scratch_shapes=[pltpu.VMEM((dff, tn), BF16)],
    )
    return pl.pallas_call(
        _ffn_down_cols_kernel, grid_spec=grid_spec,
        out_shape=jax.ShapeDtypeStruct((total_rows or r, d), BF16),
        input_output_aliases=aliases,
        compiler_params=_params("arbitrary", "arbitrary"), name="swiglu_down_cols",
    )(*args)


def _combine_kernel(y1_ref, y2_ref, wt_ref, x_ref, g_ref, mod_ref, o_ref):
    wt = wt_ref[...]
    y = wt[:, 0:1] * y1_ref[...].astype(F32) + wt[:, 1:2] * y2_ref[...].astype(F32)
    o_ref[...] = x_ref[...] + mod_ref[0][5:6] * _rms(y, g_ref[...])


def _combine(cfg, y1, y2, wt, x, g, mod, rows_per_mod):
    t, d = x.shape
    tm = min(cfg.tm_norm, t)
    row = lambda i: (i, 0)
    return pl.pallas_call(
        _combine_kernel,
        grid=(t // tm,),
        in_specs=[pl.BlockSpec((tm, d), row), pl.BlockSpec((tm, d), row),
                  pl.BlockSpec((tm, TOP_K), row), pl.BlockSpec((tm, d), row),
                  pl.BlockSpec((1, d), lambda i: (0, 0)),
                  pl.BlockSpec((1, 6, d), lambda i: (i * tm // rows_per_mod, 0, 0))],
        out_specs=pl.BlockSpec((tm, d), row),
        out_shape=jax.ShapeDtypeStruct((t, d), F32),
        compiler_params=_params("parallel"), name="moe_combine",
    )(y1, y2, wt, x, g, mod)


def _dispatch_plan(route, counts, tm):
    t = route.shape[1]
    n_pairs = TOP_K * t
    n_rows = n_pairs + N_EXPERTS * tm
    e_flat = route[:TOP_K].reshape(n_pairs)
    rank = route[TOP_K:2 * TOP_K].reshape(n_pairs)
    padded = (counts + tm - 1) // tm * tm
    ends = jnp.cumsum(padded)
    starts = ends - padded
    onehot = e_flat[:, None] == jnp.arange(N_EXPERTS, dtype=jnp.int32)[None, :]
    dest = jnp.sum(jnp.where(onehot, starts[None, :], 0), axis=1) + rank
    tok = jnp.tile(jnp.arange(t, dtype=jnp.int32), TOP_K)
    row_tok = jnp.zeros((n_rows,), jnp.int32).at[dest].set(tok)
    tile_start = jnp.arange(n_rows // tm, dtype=jnp.int32) * tm
    tile_eid = jnp.minimum(jnp.sum((tile_start[:, None] >= ends[None, :]).astype(jnp.int32), axis=1),
                           N_EXPERTS - 1)
    n_active = (ends[-1] // tm).astype(jnp.int32).reshape(1)
    return row_tok, dest.reshape(TOP_K, t), tile_eid, n_active


def _dense_plan(t, tm):
    return jnp.zeros((t // tm,), jnp.int32), jnp.full((1,), t // tm, jnp.int32)


def _permute_w_in(w):
    b3 = 3 * BRANCH_DIM
    q3 = 3 * QK_DIM
    return jnp.concatenate([w[:, b3 + q3:], w[:, b3:b3 + q3], w[:, :b3]], axis=1).astype(BF16)


def _forward(cfg, x, c, ctx, c_ctx, w_mod, b_mod, g_norm, w_in, b_gate, w_fourier_out, w_sgu_out,
             w_attn_out, w_o, sgu_w, sgu_b, sgu_g, diff_lambda, diff_subln_g,
             ffn_w1, ffn_w3, ffn_w2, router_w, moe_w1, moe_w3, moe_w2):
    b, s, d = x.shape
    n_ctx = ctx.shape[1]
    t_lat, t_ctx = b * s, b * n_ctx
    cos, sin = _rope_tables(s)
    fft_tables = _fft_tables(s)

    c_rows = jnp.concatenate([c, c_ctx[None, :], jnp.zeros((8 - b - 1, d), F32)], axis=0)
    mod_all = _modulation(cfg, c_rows, w_mod, b_mod)

    xl = x.reshape(t_lat, d)
    xc = ctx.reshape(t_ctx, d)
    for l in range(cfg.depth):
        last = l == cfg.depth - 1
        lam_init = 0.8 - 0.6 * math.exp(-0.3 * l)
        mod_l = mod_all[l, :b].reshape(b, 6, d)
        mod_c = mod_all[l, b:b + 1].reshape(1, 6, d)
        g = g_norm[l].reshape(4, 1, d)
        w_in_l = _permute_w_in(w_in[l])
        wf, ws = w_fourier_out[l].astype(BF16), w_sgu_out[l].astype(BF16)
        wa, wo = w_attn_out[l].astype(BF16), w_o[l].astype(BF16)
        bg = b_gate[l].reshape(1, 3 * d)
        sw = sgu_w[l].astype(BF16)
        sbt = sgu_b[l].T
        sg = sgu_g[l].reshape(1, BRANCH_DIM)
        dl = diff_lambda[l]
        gsub = diff_subln_g[l].reshape(1, PAIR)
        pair_blk = lambda off: off // PAIR

        p_lat = _in_proj(cfg, xl, g[0], mod_l, s, w_in_l)
        p_lat3 = p_lat.reshape(b, s, cfg.n_in)
        qk_rot = _rope(cfg, p_lat3, cos, sin)
        if last:
            p_ctx3 = _in_proj(cfg, xc, g[0], mod_c, t_ctx,
                              w_in_l[:, cfg.off_k:cfg.off_f]).reshape(b, n_ctx, 2 * QK_DIM)
            kc_blk, vc_blk = 0, pair_blk(QK_DIM)
        else:
            p_ctx = _in_proj(cfg, xc, g[0], mod_c, t_ctx, w_in_l)
            p_ctx3 = p_ctx.reshape(b, n_ctx, cfg.n_in)
            kc_blk, vc_blk = pair_blk(cfg.off_k), pair_blk(cfg.off_va)
        al = _attention(cfg, qk_rot[0], 0, p_ctx3, kc_blk, p_ctx3, vc_blk, dl, gsub, lam_init, cfg.tq,
                        k_lat=qk_rot, v_arr=p_lat3, v_blk=pair_blk(cfg.off_va))
        four_l = _fourier_latent(cfg, p_lat, fft_tables)
        sgu_l = _sgu(cfg, p_lat, sw, sbt, sg)
        xl = _merge(cfg, p_lat, four_l, sgu_l, al.reshape(t_lat, QK_DIM), xl, bg, wf, ws, wa, wo, g[1], mod_l, s)
        if not last:
            ac = _attention(cfg, p_ctx3, pair_blk(cfg.off_q), p_ctx3, kc_blk, p_ctx3, vc_blk, dl, gsub,
                            lam_init, n_ctx)
            four_c = _fourier_ctx(cfg, p_ctx)
            sgu_c = _sgu(cfg, p_ctx, sw, sbt, sg)
            xc = _merge(cfg, p_ctx, four_c, sgu_c, ac.reshape(t_ctx, QK_DIM), xc, bg, wf, ws, wa, wo, g[1],
                        mod_c, t_ctx)

        i = l // 2
        streams = [(xl, mod_l, s)] + ([] if last else [(xc, mod_c, t_ctx)])
        outs = []
        if l % 2 == 0:
            w2 = ffn_w2[i][None].astype(BF16)
            for xs, mod, rpm in streams:
                t = xs.shape[0]
                fl = _norm_mod(cfg, xs, g[2], mod, rpm, 3, 4)
                tm_up = min(cfg.tm_up, t)
                eid_up, n_up = _dense_plan(t, tm_up)
                h = _ffn_up(cfg, fl, ffn_w1, ffn_w3, eid_up + i, n_up, tm_up)
                tm_dn = min(cfg.tm_down_dense, t)
                outs.append(_ffn_down(cfg, h, w2, *_dense_plan(t, tm_dn), tm_dn, cfg.tk_down_dense, x=xs,
                                      g=g[3], mod=mod, rows_per_mod=rpm))
        else:
            w1 = moe_w1.reshape(-1, d, cfg.dff)
            w3 = moe_w3.reshape(-1, d, cfg.dff)
            w2 = moe_w2.reshape(-1, cfg.dff, d)
            rwt = router_w[i].T
            for xs, mod, rpm in streams:
                tm = cfg.tm_down if xs.shape[0] >= N_EXPERTS * cfg.tm_down else cfg.tm_down_ctx
                fl, route, top_w, counts = _norm_mod(cfg, xs, g[2], mod, rpm, 3, 4, router_wt=rwt)
                row_tok, pos, tile_eid, n_active = _dispatch_plan(
                    route, counts[:N_EXPERTS, 0].astype(jnp.int32), tm)
                n_tiles = tile_eid.shape[0]
                n_chunks = cfg.moe_chunks if tm == cfg.tm_down and n_tiles % cfg.moe_chunks == 0 else 1
                tpc = n_tiles // n_chunks
                tm_dn = min(cfg.tm_down_cols, tm)
                rep = tm // tm_dn
                eid = tile_eid + i * N_EXPERTS
                y = None
                for c in range(n_chunks):
                    na_c = jnp.clip(n_active - c * tpc, 0, tpc)
                    eid_c = eid[c * tpc:(c + 1) * tpc]
                    a = jnp.take(fl, row_tok[c * tpc * tm:(c + 1) * tpc * tm], axis=0)
                    h = _ffn_up(cfg, a, w1, w3, eid_c, na_c, tm)
                    y = _ffn_down_cols(cfg, h, w2, jnp.repeat(eid_c, rep), na_c * rep, tm_dn, y_prev=y,
                                       tile_off=c * tpc * rep, total_rows=n_tiles * tm)
                outs.append(_combine(cfg, jnp.take(y, pos[0], axis=0), jnp.take(y, pos[1], axis=0),
                                     top_w[:TOP_K].T, xs, g[3], mod, rpm))
        xl = outs[0]
        if not last:
            xc = outs[1]
    return xl.reshape(b, s, d)


def kernel(x, c, ctx, c_ctx, w_mod, b_mod, g_norm, w_in, b_gate, w_fourier_out, w_sgu_out, w_attn_out, w_o,
           sgu_w, sgu_b, sgu_g, diff_lambda, diff_subln_g, ffn_w1, ffn_w3, ffn_w2, router_w,
           moe_w1, moe_w3, moe_w2):
    return _forward(PROD, x, c, ctx, c_ctx, w_mod, b_mod, g_norm, w_in, b_gate, w_fourier_out, w_sgu_out,
                    w_attn_out, w_o, sgu_w, sgu_b, sgu_g, diff_lambda, diff_subln_g,
                    ffn_w1, ffn_w3, ffn_w2, router_w, moe_w1, moe_w3, moe_w2)
```

```python
import functools
import math
from typing import NamedTuple

import jax
import jax.numpy as jnp
from jax import lax
from jax.experimental import pallas as pl
from jax.experimental.pallas import tpu as pltpu

F32 = jnp.float32
BF16 = jnp.bfloat16

EPS = 1e-6
GRID_W = 64
ROPE_BASE = 10000.0
HEAD_DIM = 64
HEADS = 8
PAIR = 2 * HEAD_DIM
QK_DIM = HEADS * PAIR
GROUP_DIM = 128
GROUPS = 4
BRANCH_DIM = GROUPS * GROUP_DIM
CHUNK = 128
FFT_N1 = 128
FFT_A_PER_STEP = 4
FFT_B_PER_STEP = 8
N_EXPERTS = 8
TOP_K = 2
EXPERT_ROWS = 8
ATTN_SCALE = HEAD_DIM ** -0.5
Q_SCALE = ATTN_SCALE * math.log2(math.e)
VMEM_LIMIT = 56 * 1024 * 1024


class Cfg(NamedTuple):
    d: int
    batch: int
    seq: int
    ctx: int
    dff: int
    depth: int
    tm_norm: int
    tm_mm: int
    tn_mm: int
    tm_rope: int
    tq: int
    tk: int
    tm_sgu: int
    tm_merge: int
    tm_up: int
    tn_up: int
    tm_down: int
    tm_down_cols: int
    tn_down_cols: int
    tm_down_ctx: int
    moe_chunks: int
    tm_down_dense: int
    tk_down_dense: int
    tn_mod: int

    @property
    def n_in(self):
        return 3 * self.d + 3 * QK_DIM + 3 * BRANCH_DIM

    @property
    def off_q(self):
        return 3 * self.d

    @property
    def off_k(self):
        return self.off_q + QK_DIM

    @property
    def off_va(self):
        return self.off_k + QK_DIM

    @property
    def off_f(self):
        return self.off_va + QK_DIM

    @property
    def off_u(self):
        return self.off_f + BRANCH_DIM

    @property
    def off_v(self):
        return self.off_u + BRANCH_DIM


PROD = Cfg(d=2048, batch=4, seq=8192, ctx=256, dff=5632, depth=4,
           tm_norm=512, tm_mm=1024, tn_mm=1536, tm_rope=512, tq=1024, tk=256,
           tm_sgu=1024, tm_merge=512, tm_up=1024, tn_up=512, tm_down=1024, tm_down_cols=512, tn_down_cols=512,
           tm_down_ctx=256, moe_chunks=4, tm_down_dense=1024, tk_down_dense=512, tn_mod=1024)


def _params(*sem):
    return pltpu.CompilerParams(dimension_semantics=sem, vmem_limit_bytes=VMEM_LIMIT)


def _rms(y, g):
    return y * lax.rsqrt(jnp.mean(y * y, axis=-1, keepdims=True) + EPS) * g


def _mod_kernel(c_ref, w_ref, b_ref, o_ref):
    sc = jax.nn.silu(c_ref[...])
    o_ref[0] = jnp.dot(sc.astype(BF16), w_ref[0].astype(BF16), preferred_element_type=F32) + b_ref[0]


def _modulation(cfg, c_rows, w_mod, b_mod):
    depth, d, n6 = w_mod.shape
    tn = cfg.tn_mod
    return pl.pallas_call(
        _mod_kernel,
        grid=(depth, n6 // tn),
        in_specs=[pl.BlockSpec((8, d), lambda l, j: (0, 0)),
                  pl.BlockSpec((1, d, tn), lambda l, j: (l, 0, j)),
                  pl.BlockSpec((1, 1, tn), lambda l, j: (l, 0, j))],
        out_specs=pl.BlockSpec((1, 8, tn), lambda l, j: (l, 0, j)),
        out_shape=jax.ShapeDtypeStruct((depth, 8, n6), F32),
        compiler_params=_params("parallel", "parallel"),
        name="modulation",
    )(c_rows, w_mod, b_mod.reshape(depth, 1, n6))


def _norm_mod_kernel(x_ref, g_ref, mod_ref, o_ref, *, shift_idx, scale_idx):
    y = _rms(x_ref[...], g_ref[...])
    m = mod_ref[0]
    o_ref[...] = (y * (1.0 + m[scale_idx:scale_idx + 1]) + m[shift_idx:shift_idx + 1]).astype(o_ref.dtype)


def _norm_mod_router_kernel(x_ref, g_ref, mod_ref, rw_ref, tri_ref, o_ref, idx_ref, wt_ref, cnt_ref, *,
                            shift_idx, scale_idx):
    @pl.when(pl.program_id(0) == 0)
    def _():
        cnt_ref[...] = jnp.zeros(cnt_ref.shape, F32)

    y = _rms(x_ref[...], g_ref[...])
    m = mod_ref[0]
    h = y * (1.0 + m[scale_idx:scale_idx + 1]) + m[shift_idx:shift_idx + 1]
    o_ref[...] = h.astype(o_ref.dtype)
    logits = lax.dot_general(rw_ref[...], h, (((1,), (1,)), ((), ())),
                             precision=lax.Precision.HIGHEST, preferred_element_type=F32)
    row = lax.broadcasted_iota(jnp.int32, logits.shape, 0).astype(F32)
    m1 = jnp.max(logits, axis=0, keepdims=True)
    i1 = jnp.min(jnp.where(logits == m1, row, float(N_EXPERTS)), axis=0, keepdims=True)
    rest = jnp.where(row == i1, -jnp.inf, logits)
    m2 = jnp.max(rest, axis=0, keepdims=True)
    i2 = jnp.min(jnp.where(rest == m2, row, float(N_EXPERTS)), axis=0, keepdims=True)
    e2 = jnp.exp(m2 - m1)
    w1 = 1.0 / (1.0 + e2)
    w2 = e2 / (1.0 + e2)
    wt_ref[...] = jnp.where(row == 0.0, w1, jnp.where(row == 1.0, w2, 0.0))
    pick1 = jnp.where(row == i1, 1.0, 0.0)
    pick2 = jnp.where(row == i2, 1.0, 0.0)
    pre1 = jnp.dot(pick1.astype(BF16), tri_ref[...], preferred_element_type=F32)
    pre2 = jnp.dot(pick2.astype(BF16), tri_ref[...], preferred_element_type=F32)
    n1 = jnp.sum(pick1, axis=1, keepdims=True)
    n2 = jnp.sum(pick2, axis=1, keepdims=True)
    seen = cnt_ref[:, 0:1]
    r1 = jnp.sum(pick1 * (pre1 - 1.0 + seen), axis=0, keepdims=True)
    r2 = jnp.sum(pick2 * (pre2 - 1.0 + seen + n1), axis=0, keepdims=True)
    cnt_ref[...] = cnt_ref[...] + (n1 + n2)
    idx_ref[...] = jnp.where(row == 0.0, i1, jnp.where(row == 1.0, i2, jnp.where(row == 2.0, r1, jnp.where(
        row == 3.0, r2, 0.0)))).astype(jnp.int32)


def _norm_mod(cfg, x, g, mod, rows_per_mod, shift_idx, scale_idx, router_wt=None):
    t, d = x.shape
    tm = min(cfg.tm_norm, t)
    in_specs = [pl.BlockSpec((tm, d), lambda i: (i, 0)),
                pl.BlockSpec((1, d), lambda i: (0, 0)),
                pl.BlockSpec((1, 6, d), lambda i: (i * tm // rows_per_mod, 0, 0))]
    h_spec = pl.BlockSpec((tm, d), lambda i: (i, 0))
    h_shape = jax.ShapeDtypeStruct((t, d), BF16)
    if router_wt is None:
        return pl.pallas_call(
            functools.partial(_norm_mod_kernel, shift_idx=shift_idx, scale_idx=scale_idx),
            grid=(t // tm,), in_specs=in_specs, out_specs=h_spec, out_shape=h_shape,
            compiler_params=_params("parallel"), name="norm_mod",
        )(x, g, mod)
    r_spec = pl.BlockSpec((EXPERT_ROWS, tm), lambda i: (0, i))
    tri = jnp.triu(jnp.ones((tm, tm), BF16))
    return pl.pallas_call(
        functools.partial(_norm_mod_router_kernel, shift_idx=shift_idx, scale_idx=scale_idx),
        grid=(t // tm,),
        in_specs=in_specs + [pl.BlockSpec((N_EXPERTS, d), lambda i: (0, 0)),
                             pl.BlockSpec((tm, tm), lambda i: (0, 0))],
        out_specs=[h_spec, r_spec, r_spec, pl.BlockSpec((EXPERT_ROWS, 128), lambda i: (0, 0))],
        out_shape=[h_shape, jax.ShapeDtypeStruct((EXPERT_ROWS, t), jnp.int32),
                   jax.ShapeDtypeStruct((EXPERT_ROWS, t), F32),
                   jax.ShapeDtypeStruct((EXPERT_ROWS, 128), F32)],
        compiler_params=_params("arbitrary"), name="norm_mod_router",
    )(x, g, mod, router_wt, tri)


def _in_proj_kernel(x_ref, g_ref, mod_ref, w_ref, o_ref, h_ref):
    @pl.when(pl.program_id(1) == 0)
    def _():
        m = mod_ref[0]
        h_ref[...] = (_rms(x_ref[...], g_ref[...]) * (1.0 + m[1:2]) + m[0:1]).astype(h_ref.dtype)

    o_ref[...] = jnp.dot(h_ref[...], w_ref[...], preferred_element_type=F32).astype(o_ref.dtype)


def _in_proj(cfg, x, g, mod, rows_per_mod, w):
    t, d = x.shape
    n = w.shape[1]
    tm = min(cfg.tm_mm, t)
    tn = cfg.tn_mm if n % cfg.tn_mm == 0 else 512
    return pl.pallas_call(
        _in_proj_kernel,
        grid=(t // tm, n // tn),
        in_specs=[pl.BlockSpec((tm, d), lambda i, j: (i, 0)),
                  pl.BlockSpec((1, d), lambda i, j: (0, 0)),
                  pl.BlockSpec((1, 6, d), lambda i, j: (i * tm // rows_per_mod, 0, 0)),
                  pl.BlockSpec((d, tn), lambda i, j: (0, j))],
        out_specs=pl.BlockSpec((tm, tn), lambda i, j: (i, j)),
        out_shape=jax.ShapeDtypeStruct((t, n), BF16),
        scratch_shapes=[pltpu.VMEM((tm, d), BF16)],
        compiler_params=_params("parallel", "arbitrary"), name="in_proj",
    )(x, g, mod, w)


def _rope_tables(seq):
    rows = seq // GRID_W
    r = jnp.repeat(jnp.arange(rows, dtype=F32), GRID_W)
    col = jnp.tile(jnp.arange(GRID_W, dtype=F32), rows)
    n_freq = HEAD_DIM // 4
    inv = ROPE_BASE ** (-jnp.arange(n_freq, dtype=F32) / n_freq)
    ang = jnp.concatenate([r[:, None] * inv, col[:, None] * inv], axis=-1)
    cos, sin = jnp.cos(ang), jnp.sin(ang)
    return jnp.tile(cos, (1, 4)), jnp.tile(jnp.concatenate([-sin, sin], axis=-1), (1, 2))


def _rope_kernel(p_ref, cos_ref, sin_ref, o_ref):
    scale = jnp.where(pl.program_id(2) == 0, Q_SCALE, 1.0).astype(F32)
    cos = cos_ref[...] * scale
    sin = sin_ref[...] * scale
    lane = lax.broadcasted_iota(jnp.int32, cos.shape, 1)
    first_half = (lane % HEAD_DIM) < (HEAD_DIM // 2)
    for c in range(QK_DIM // PAIR):
        t = p_ref[0, :, c * PAIR:(c + 1) * PAIR].astype(F32)
        partner = jnp.where(first_half, pltpu.roll(t, PAIR - HEAD_DIM // 2, 1), pltpu.roll(t, HEAD_DIM // 2, 1))
        o_ref[0, 0, :, c * PAIR:(c + 1) * PAIR] = (t * cos + partner * sin).astype(o_ref.dtype)


def _rope(cfg, p3, cos, sin):
    b, s, _ = p3.shape
    tm = cfg.tm_rope
    qblk = cfg.off_q // QK_DIM
    return pl.pallas_call(
        _rope_kernel,
        grid=(b, s // tm, 2),
        in_specs=[pl.BlockSpec((1, tm, QK_DIM), lambda bi, i, j: (bi, i, qblk + j)),
                  pl.BlockSpec((tm, PAIR), lambda bi, i, j: (i, 0)),
                  pl.BlockSpec((tm, PAIR), lambda bi, i, j: (i, 0))],
        out_specs=pl.BlockSpec((1, 1, tm, QK_DIM), lambda bi, i, j: (j, bi, i, 0)),
        out_shape=jax.ShapeDtypeStruct((2, b, s, QK_DIM), BF16),
        compiler_params=_params("parallel", "parallel", "parallel"), name="rope",
    )(p3, cos, sin)


def _attn_kernel(*refs, tq, tk, n_lat, q_prescaled):
    if n_lat:
        q_ref, kc_ref, vc_ref, k_ref, v_ref, dl_ref, g_ref, li_ref, o_ref, q2_ref, vx_ref = refs
    else:
        q_ref, kc_ref, vc_ref, dl_ref, g_ref, li_ref, o_ref, q2_ref, vx_ref = refs
    lc = kc_ref.shape[1]

    @pl.when(pl.program_id(2) == 0)
    def _():
        vx_ref[:lc, :PAIR] = vc_ref[0]
        vx_ref[:lc, PAIR:] = jnp.ones((lc, PAIR), BF16)
        if n_lat:
            vx_ref[lc:, :PAIR] = v_ref[0]
            vx_ref[lc:, PAIR:] = jnp.ones((n_lat * tk, PAIR), BF16)

    q = q_ref[0]
    if not q_prescaled:
        q = (q.astype(F32) * Q_SCALE).astype(BF16)
    lane = lax.broadcasted_iota(jnp.int32, q.shape, 1)
    zero = jnp.zeros_like(q)
    q2_ref[:tq] = jnp.where(lane < HEAD_DIM, q, zero)
    q2_ref[tq:] = jnp.where(lane >= HEAD_DIM, q, zero)
    q2 = q2_ref[...]

    chunks = [(kc_ref[0], 0, lc)]
    for j in range(n_lat):
        chunks.append((k_ref[0, 0, j * tk:(j + 1) * tk, :], lc + j * tk, tk))
    m = jnp.full((2 * tq, 1), -jnp.inf, F32)
    acc = jnp.zeros((2 * tq, 2 * PAIR), F32)
    for k, off, n in chunks:
        s = lax.dot_general(q2, k, (((1,), (1,)), ((), ())), preferred_element_type=F32)
        m_new = jnp.maximum(m, jnp.max(s, axis=-1, keepdims=True))
        p = jnp.exp2(s - m_new).astype(BF16)
        acc = jnp.exp2(m - m_new) * acc + jnp.dot(p, vx_ref[off:off + n, :], preferred_element_type=F32)
        m = m_new

    o12 = acc[:, :PAIR] / acc[:, PAIR:]
    dl = dl_ref[...]
    lam_init = li_ref[...]
    lam = (jnp.exp(jnp.sum(dl[0:1] * dl[1:2], axis=-1, keepdims=True))
           - jnp.exp(jnp.sum(dl[2:3] * dl[3:4], axis=-1, keepdims=True)) + lam_init)
    o = o12[:tq] - lam * o12[tq:]
    o_ref[0] = (_rms(o, g_ref[...]) * (1.0 - lam_init)).astype(o_ref.dtype)


def _attention(cfg, q_arr, q_blk, kc_arr, kc_blk, vc_arr, vc_blk, dl, g_sub, lam_init, tq,
               k_lat=None, v_arr=None, v_blk=0):
    b, lq = q_arr.shape[0], q_arr.shape[1]
    lc = kc_arr.shape[1]
    in_specs = [pl.BlockSpec((1, tq, PAIR), lambda bi, h, i: (bi, i, q_blk + h)),
                pl.BlockSpec((1, lc, PAIR), lambda bi, h, i: (bi, 0, kc_blk + h)),
                pl.BlockSpec((1, lc, PAIR), lambda bi, h, i: (bi, 0, vc_blk + h))]
    args = [q_arr, kc_arr, vc_arr]
    n_lat = 0
    if k_lat is not None:
        s = k_lat.shape[2]
        n_lat = s // cfg.tk
        in_specs += [pl.BlockSpec((1, 1, s, PAIR), lambda bi, h, i: (1, bi, 0, h)),
                     pl.BlockSpec((1, s, PAIR), lambda bi, h, i: (bi, 0, v_blk + h))]
        args += [k_lat, v_arr]
    in_specs += [pl.BlockSpec((4, HEAD_DIM), lambda bi, h, i: (0, 0)),
                 pl.BlockSpec((1, PAIR), lambda bi, h, i: (0, 0)),
                 pl.BlockSpec((1, PAIR), lambda bi, h, i: (0, 0))]
    args += [dl, g_sub, jnp.full((1, PAIR), lam_init, F32)]
    lk = lc + n_lat * cfg.tk
    return pl.pallas_call(
        functools.partial(_attn_kernel, tq=tq, tk=cfg.tk, n_lat=n_lat, q_prescaled=k_lat is not None),
        grid=(b, HEADS, lq // tq),
        in_specs=in_specs,
        out_specs=pl.BlockSpec((1, tq, PAIR), lambda bi, h, i: (bi, i, h)),
        out_shape=jax.ShapeDtypeStruct((b, lq, QK_DIM), BF16),
        scratch_shapes=[pltpu.VMEM((2 * tq, PAIR), BF16), pltpu.VMEM((lk, 2 * PAIR), BF16)],
        compiler_params=_params("parallel", "parallel", "arbitrary"),
        name="diff_attention" if n_lat else "diff_attention_ctx",
    )(*args)


def _dft_cos_sin(n, scale=1.0):
    j = jnp.arange(n, dtype=jnp.int32)
    ang = ((j[:, None] * j[None, :]) % n).astype(F32) * (2.0 * math.pi / n)
    return jnp.cos(ang) * scale, jnp.sin(ang) * scale


def _fft_tables(seq):
    n1, n2 = FFT_N1, seq // FFT_N1
    c1, s1 = _dft_cos_sin(n1)
    stage1 = jnp.concatenate([c1, -s1], axis=0).astype(BF16)
    k1 = jnp.arange(n1, dtype=jnp.int32)
    m = jnp.arange(n2, dtype=jnp.int32)
    ang = (m[:, None] * k1[None, :]).astype(F32) * (2.0 * math.pi / seq)
    tw_cos, tw_sin = jnp.cos(ang)[:, :, None], jnp.sin(ang)[:, :, None]
    c2, s2 = _dft_cos_sin(n2)
    stage2 = jnp.concatenate([jnp.concatenate([c2, s2], axis=1),
                              jnp.concatenate([-s2, c2], axis=1)], axis=0).astype(BF16)
    cc, sc = _dft_cos_sin(GROUP_DIM, scale=(seq * GROUP_DIM) ** -0.5)
    return stage1, tw_cos, tw_sin, stage2, cc.astype(BF16), sc.astype(BF16)


def _fft_a_kernel(x_ref, w_ref, c_ref, s_ref, o_ref):
    for t in range(FFT_A_PER_STEP):
        x = x_ref[0, :, t * BRANCH_DIM:(t + 1) * BRANCH_DIM]
        a = jnp.dot(w_ref[...], x, preferred_element_type=F32)
        re, im = a[:FFT_N1], a[FFT_N1:]
        c, s = c_ref[t], s_ref[t]
        base = 2 * t * BRANCH_DIM
        o_ref[0, :, base:base + BRANCH_DIM] = (re * c + im * s).astype(o_ref.dtype)
        o_ref[0, :, base + BRANCH_DIM:base + 2 * BRANCH_DIM] = (im * c - re * s).astype(o_ref.dtype)


def _fft_b_kernel(x_ref, w_ref, cc_ref, sc_ref, o_ref, *, n2):
    for t in range(FFT_B_PER_STEP):
        x = x_ref[0, t]
        z = jnp.concatenate([x[:, :BRANCH_DIM], x[:, BRANCH_DIM:]], axis=0)
        y = jnp.dot(w_ref[...], z, preferred_element_type=F32).astype(BF16)
        for g in range(GROUPS):
            sl = slice(g * GROUP_DIM, (g + 1) * GROUP_DIM)
            o_ref[0, :, t * BRANCH_DIM + g * GROUP_DIM:t * BRANCH_DIM + (g + 1) * GROUP_DIM] = (
                jnp.dot(y[:n2, sl], cc_ref[...], preferred_element_type=F32)
                + jnp.dot(y[n2:, sl], sc_ref[...], preferred_element_type=F32)).astype(o_ref.dtype)


def _fourier_latent(cfg, p_lat, tables):
    b, s = cfg.batch, cfg.seq
    n1, n2 = FFT_N1, s // FFT_N1
    stage1, tw_cos, tw_sin, stage2, cc, sc = tables
    f = p_lat[:, cfg.off_f:cfg.off_f + BRANCH_DIM]
    pa, pb = FFT_A_PER_STEP, FFT_B_PER_STEP
    mid = pl.pallas_call(
        _fft_a_kernel,
        grid=(b, n2 // pa),
        in_specs=[pl.BlockSpec((1, n1, pa * BRANCH_DIM), lambda bi, m: (bi, 0, m)),
                  pl.BlockSpec((2 * n1, n1), lambda bi, m: (0, 0)),
                  pl.BlockSpec((pa, n1, 1), lambda bi, m: (m, 0, 0)),
                  pl.BlockSpec((pa, n1, 1), lambda bi, m: (m, 0, 0))],
        out_specs=pl.BlockSpec((1, n1, pa * 2 * BRANCH_DIM), lambda bi, m: (bi, 0, m)),
        out_shape=jax.ShapeDtypeStruct((b, n1, n2 * 2 * BRANCH_DIM), BF16),
        compiler_params=_params("parallel", "parallel"), name="fft_stage_a",
    )(f.reshape(b, n1, n2 * BRANCH_DIM), stage1, tw_cos, tw_sin)
    out = pl.pallas_call(
        functools.partial(_fft_b_kernel, n2=n2),
        grid=(b, n1 // pb),
        in_specs=[pl.BlockSpec((1, pb, n2, 2 * BRANCH_DIM), lambda bi, k: (bi, k, 0, 0)),
                  pl.BlockSpec((2 * n2, 2 * n2), lambda bi, k: (0, 0)),
                  pl.BlockSpec((GROUP_DIM, GROUP_DIM), lambda bi, k: (0, 0)),
                  pl.BlockSpec((GROUP_DIM, GROUP_DIM), lambda bi, k: (0, 0))],
        out_specs=pl.BlockSpec((1, n2, pb * BRANCH_DIM), lambda bi, k: (bi, 0, k)),
        out_shape=jax.ShapeDtypeStruct((b, n2, n1 * BRANCH_DIM), BF16),
        compiler_params=_params("parallel", "parallel"), name="fft_stage_b",
    )(mid.reshape(b, n1, n2, 2 * BRANCH_DIM), stage2, cc, sc)
    return out.reshape(b * s, BRANCH_DIM)


def _dft_dense_kernel(x_ref, c_ref, s_ref, cc_ref, sc_ref, o_ref):
    x = x_ref[0]
    yr = jnp.dot(c_ref[...], x, preferred_element_type=F32).astype(BF16)
    yi = (-jnp.dot(s_ref[...], x, preferred_element_type=F32)).astype(BF16)
    for g in range(GROUPS):
        sl = slice(g * GROUP_DIM, (g + 1) * GROUP_DIM)
        o_ref[0, :, sl] = (jnp.dot(yr[:, sl], cc_ref[...], preferred_element_type=F32)
                           + jnp.dot(yi[:, sl], sc_ref[...], preferred_element_type=F32)).astype(o_ref.dtype)


def _fourier_ctx(cfg, p_ctx):
    b, n = cfg.batch, cfg.ctx
    cn, sn = _dft_cos_sin(n)
    cc, sc = _dft_cos_sin(GROUP_DIM, scale=(n * GROUP_DIM) ** -0.5)
    fblk = cfg.off_f // BRANCH_DIM
    full = lambda bi: (0, 0)
    out = pl.pallas_call(
        _dft_dense_kernel,
        grid=(b,),
        in_specs=[pl.BlockSpec((1, n, BRANCH_DIM), lambda bi: (bi, 0, fblk)),
                  pl.BlockSpec((n, n), full), pl.BlockSpec((n, n), full),
                  pl.BlockSpec((GROUP_DIM, GROUP_DIM), full), pl.BlockSpec((GROUP_DIM, GROUP_DIM), full)],
        out_specs=pl.BlockSpec((1, n, BRANCH_DIM), lambda bi: (bi, 0, 0)),
        out_shape=jax.ShapeDtypeStruct((b, n, BRANCH_DIM), BF16),
        compiler_params=_params("parallel"), name="dft_ctx",
    )(p_ctx.reshape(b, n, cfg.n_in), cn.astype(BF16), sn.astype(BF16), cc.astype(BF16), sc.astype(BF16))
    return out.reshape(b * n, BRANCH_DIM)


def _sgu_kernel(u_ref, v_ref, g_ref, w_ref, b_ref, o_ref, *, n_chunks):
    gv = g_ref[...]
    for c in range(n_chunks):
        rows = slice(c * CHUNK, (c + 1) * CHUNK)
        u = jax.nn.gelu(u_ref[rows, :].astype(F32))
        v = jax.nn.gelu(v_ref[rows, :].astype(F32))
        vc = v - jnp.mean(v, axis=-1, keepdims=True)
        vn = (vc * lax.rsqrt(jnp.mean(vc * vc, axis=-1, keepdims=True) + EPS) * gv).astype(BF16)
        for g in range(GROUPS):
            sl = slice(g * GROUP_DIM, (g + 1) * GROUP_DIM)
            sv = jnp.dot(w_ref[g], vn[:, sl], preferred_element_type=F32) + b_ref[:, g:g + 1]
            o_ref[rows, sl] = (u[:, sl] * sv).astype(o_ref.dtype)


def _sgu(cfg, p2, sgu_w, sgu_bt, sgu_g):
    t = p2.shape[0]
    tm = min(cfg.tm_sgu, t)
    ublk, vblk = cfg.off_u // BRANCH_DIM, cfg.off_v // BRANCH_DIM
    return pl.pallas_call(
        functools.partial(_sgu_kernel, n_chunks=tm // CHUNK),
        grid=(t // tm,),
        in_specs=[pl.BlockSpec((tm, BRANCH_DIM), lambda i: (i, ublk)),
                  pl.BlockSpec((tm, BRANCH_DIM), lambda i: (i, vblk)),
                  pl.BlockSpec((1, BRANCH_DIM), lambda i: (0, 0)),
                  pl.BlockSpec((GROUPS, CHUNK, CHUNK), lambda i: (0, 0, 0)),
                  pl.BlockSpec((CHUNK, GROUPS), lambda i: (0, 0))],
        out_specs=pl.BlockSpec((tm, BRANCH_DIM), lambda i: (i, 0)),
        out_shape=jax.ShapeDtypeStruct((t, BRANCH_DIM), BF16),
        compiler_params=_params("parallel"), name="spatial_gating",
    )(p2, p2, sgu_g, sgu_w, sgu_bt)


def _merge_kernel(pg0_ref, pg1_ref, pg2_ref, f_ref, s_ref, a_ref, x_ref, bg_ref,
                  wf_ref, ws_ref, wa_ref, wo_ref, g_ref, mod_ref, o_ref, *, d):
    def gate(pg_ref, k):
        return jax.nn.sigmoid(pg_ref[...].astype(F32) + bg_ref[:, k * d:(k + 1) * d])

    merged = gate(pg0_ref, 0) * jnp.dot(f_ref[...], wf_ref[...], preferred_element_type=F32)
    merged += gate(pg1_ref, 1) * jnp.dot(s_ref[...], ws_ref[...], preferred_element_type=F32)
    merged += gate(pg2_ref, 2) * jnp.dot(a_ref[...], wa_ref[...], preferred_element_type=F32)
    y = jnp.dot(merged.astype(BF16), wo_ref[...], preferred_element_type=F32)
    o_ref[...] = x_ref[...] + mod_ref[0][2:3] * _rms(y, g_ref[...])


def _resident(shape):
    return pl.BlockSpec(shape, lambda *_: (0,) * len(shape), pipeline_mode=pl.Buffered(1))


def _merge(cfg, p2, four, sgu, attn, x, b_gate, wf, ws, wa, wo, g, mod, rows_per_mod):
    t, d = x.shape
    tm = cfg.tm_merge
    row = lambda i: (i, 0)
    return pl.pallas_call(
        functools.partial(_merge_kernel, d=d),
        grid=(t // tm,),
        in_specs=[pl.BlockSpec((tm, d), lambda i: (i, 0)),
                  pl.BlockSpec((tm, d), lambda i: (i, 1)),
                  pl.BlockSpec((tm, d), lambda i: (i, 2)),
                  pl.BlockSpec((tm, BRANCH_DIM), row), pl.BlockSpec((tm, BRANCH_DIM), row),
                  pl.BlockSpec((tm, QK_DIM), row), pl.BlockSpec((tm, d), row),
                  _resident((1, 3 * d)), _resident((BRANCH_DIM, d)), _resident((BRANCH_DIM, d)),
                  _resident((QK_DIM, d)), _resident((d, d)), _resident((1, d)),
                  pl.BlockSpec((1, 6, d), lambda i: (i * tm // rows_per_mod, 0, 0))],
        out_specs=pl.BlockSpec((tm, d), row),
        out_shape=jax.ShapeDtypeStruct((t, d), F32),
        compiler_params=_params("parallel"), name="merge",
    )(p2, p2, p2, four, sgu, attn, x, b_gate, wf, ws, wa, wo, g, mod)


def _last_active(i, n_active_ref):
    return jnp.minimum(i, jnp.maximum(n_active_ref[0] - 1, 0))


def _ffn_up_kernel(eid_ref, nact_ref, a_ref, w1_ref, w3_ref, o_ref, w1b_ref, w3b_ref):
    i = pl.program_id(1)
    active = i < nact_ref[0]
    fresh = jnp.logical_or(i == 0, eid_ref[i] != eid_ref[jnp.maximum(i - 1, 0)])

    @pl.when(jnp.logical_and(active, fresh))
    def _():
        w1b_ref[...] = w1_ref[0].astype(BF16)
        w3b_ref[...] = w3_ref[0].astype(BF16)

    @pl.when(active)
    def _():
        a = a_ref[...]
        h1 = jnp.dot(a, w1b_ref[...], preferred_element_type=F32)
        h3 = jnp.dot(a, w3b_ref[...], preferred_element_type=F32)
        o_ref[...] = (jax.nn.silu(h1) * h3).astype(o_ref.dtype)

    @pl.when(jnp.logical_not(active))
    def _():
        o_ref[...] = jnp.zeros(o_ref.shape, o_ref.dtype)


def _ffn_up(cfg, a, w1, w3, tile_eid, n_active, tm):
    r, d = a.shape
    dff = w1.shape[2]
    tn = cfg.tn_up
    grid_spec = pltpu.PrefetchScalarGridSpec(
        num_scalar_prefetch=2,
        grid=(dff // tn, r // tm),
        in_specs=[pl.BlockSpec((tm, d), lambda j, i, eid, na: (_last_active(i, na), 0)),
                  pl.BlockSpec((1, d, tn), lambda j, i, eid, na: (eid[i], 0, j)),
                  pl.BlockSpec((1, d, tn), lambda j, i, eid, na: (eid[i], 0, j))],
        out_specs=pl.BlockSpec((tm, tn), lambda j, i, eid, na: (i, j)),
        scratch_shapes=[pltpu.VMEM((d, tn), BF16), pltpu.VMEM((d, tn), BF16)],
    )
    return pl.pallas_call(
        _ffn_up_kernel, grid_spec=grid_spec,
        out_shape=jax.ShapeDtypeStruct((r, dff), BF16),
        compiler_params=_params("arbitrary", "arbitrary"), name="swiglu_up",
    )(tile_eid, n_active, a, w1, w3)


def _ffn_down_kernel(eid_ref, nact_ref, *refs, residual):
    if residual:
        h_ref, w_ref, x_ref, g_ref, mod_ref, o_ref, acc_ref = refs
    else:
        h_ref, w_ref, o_ref, acc_ref = refs
    k = pl.program_id(1)

    @pl.when(k == 0)
    def _():
        acc_ref[...] = jnp.zeros(acc_ref.shape, F32)

    @pl.when(pl.program_id(0) < nact_ref[0])
    def _():
        acc_ref[...] += jnp.dot(h_ref[...], w_ref[0], preferred_element_type=F32)

    @pl.when(k == pl.num_programs(1) - 1)
    def _():
        if residual:
            o_ref[...] = x_ref[...] + mod_ref[0][5:6] * _rms(acc_ref[...], g_ref[...])
        else:
            o_ref[...] = acc_ref[...].astype(o_ref.dtype)


def _ffn_down(cfg, h, w2, tile_eid, n_active, tm, tk, x=None, g=None, mod=None, rows_per_mod=None):
    r, dff = h.shape
    d = w2.shape[2]
    residual = x is not None
    in_specs = [pl.BlockSpec((tm, tk), lambda i, k, eid, na: (_last_active(i, na), k)),
                pl.BlockSpec((1, tk, d), lambda i, k, eid, na: (eid[i], k, 0))]
    args = [h, w2]
    if residual:
        in_specs += [pl.BlockSpec((tm, d), lambda i, k, eid, na: (i, 0)),
                     pl.BlockSpec((1, d), lambda i, k, eid, na: (0, 0)),
                     pl.BlockSpec((1, 6, d), lambda i, k, eid, na: (i * tm // rows_per_mod, 0, 0))]
        args += [x, g, mod]
    grid_spec = pltpu.PrefetchScalarGridSpec(
        num_scalar_prefetch=2,
        grid=(r // tm, dff // tk),
        in_specs=in_specs,
        out_specs=pl.BlockSpec((tm, d), lambda i, k, eid, na: (i, 0)),
        scratch_shapes=[pltpu.VMEM((tm, d), F32)],
    )
    return pl.pallas_call(
        functools.partial(_ffn_down_kernel, residual=residual), grid_spec=grid_spec,
        out_shape=jax.ShapeDtypeStruct((r, d), F32 if residual else BF16),
        compiler_params=_params("parallel", "arbitrary"),
        name="swiglu_down_residual" if residual else "swiglu_down",
    )(tile_eid, n_active, *args)


def _ffn_down_cols_kernel(eid_ref, nact_ref, h_ref, w_ref, *refs):
    o_ref, wb_ref = refs[-2:]
    i = pl.program_id(1)
    active = i < nact_ref[0]
    fresh = jnp.logical_or(i == 0, eid_ref[i] != eid_ref[jnp.maximum(i - 1, 0)])

    @pl.when(jnp.logical_and(active, fresh))
    def _():
        wb_ref[...] = w_ref[0].astype(BF16)

    @pl.when(active)
    def _():
        o_ref[...] = jnp.dot(h_ref[...], wb_ref[...], preferred_element_type=F32).astype(o_ref.dtype)

    @pl.when(jnp.logical_not(active))
    def _():
        o_ref[...] = jnp.zeros(o_ref.shape, o_ref.dtype)


def _ffn_down_cols(cfg, h, w2, tile_eid, n_active, tm, y_prev=None, tile_off=0, total_rows=None):
    r, dff = h.shape
    d = w2.shape[2]
    tn = cfg.tn_down_cols
    in_specs = [pl.BlockSpec((tm, dff), lambda n, i, eid, na: (_last_active(i, na), 0)),
                pl.BlockSpec((1, dff, tn), lambda n, i, eid, na: (eid[i], 0, n))]
    args = [tile_eid, n_active, h, w2]
    aliases = {}
    if y_prev is not None:
        in_specs.append(pl.BlockSpec(memory_space=pl.ANY))
        args.append(y_prev)
        aliases = {len(args) - 1: 0}
    grid_spec = pltpu.PrefetchScalarGridSpec(
        num_scalar_prefetch=2,
        grid=(d // tn, r // tm),
        in_specs=in_specs,
        out_specs=pl.BlockSpec((tm, tn), lambda n, i, eid, na: (i + tile_off, n)),
        scratch_shapes=[pltpu.VMEM((dff, tn), BF16)],
    )
    return pl.pallas_call(
        _ffn_down_cols_kernel, grid_spec=grid_spec,
        out_shape=jax.ShapeDtypeStruct((total_rows or r, d), BF16),
        input_output_aliases=aliases,
        compiler_params=_params("arbitrary", "arbitrary"), name="swiglu_down_cols",
    )(*args)


def _combine_kernel(y1_ref, y2_ref, wt_ref, x_ref, g_ref, mod_ref, o_ref):
    wt = wt_ref[...]
    y = wt[:, 0:1] * y1_ref[...].astype(F32) + wt[:, 1:2] * y2_ref[...].astype(F32)
    o_ref[...] = x_ref[...] + mod_ref[0][5:6] * _rms(y, g_ref[...])


def _combine(cfg, y1, y2, wt, x, g, mod, rows_per_mod):
    t, d = x.shape
    tm = min(cfg.tm_norm, t)
    row = lambda i: (i, 0)
    return pl.pallas_call(
        _combine_kernel,
        grid=(t // tm,),
        in_specs=[pl.BlockSpec((tm, d), row), pl.BlockSpec((tm, d), row),
                  pl.BlockSpec((tm, TOP_K), row), pl.BlockSpec((tm, d), row),
                  pl.BlockSpec((1, d), lambda i: (0, 0)),
                  pl.BlockSpec((1, 6, d), lambda i: (i * tm // rows_per_mod, 0, 0))],
        out_specs=pl.BlockSpec((tm, d), row),
        out_shape=jax.ShapeDtypeStruct((t, d), F32),
        compiler_params=_params("parallel"), name="moe_combine",
    )(y1, y2, wt, x, g, mod)


def _dispatch_plan(route, counts, tm):
    t = route.shape[1]
    n_pairs = TOP_K * t
    n_rows = n_pairs + N_EXPERTS * tm
    e_flat = route[:TOP_K].reshape(n_pairs)
    rank = route[TOP_K:2 * TOP_K].reshape(n_pairs)
    padded = (counts + tm - 1) // tm * tm
    ends = jnp.cumsum(padded)
    starts = ends - padded
    onehot = e_flat[:, None] == jnp.arange(N_EXPERTS, dtype=jnp.int32)[None, :]
    dest = jnp.sum(jnp.where(onehot, starts[None, :], 0), axis=1) + rank
    tok = jnp.tile(jnp.arange(t, dtype=jnp.int32), TOP_K)
    row_tok = jnp.zeros((n_rows,), jnp.int32).at[dest].set(tok, mode="promise_in_bounds", unique_indices=True)
    tile_start = jnp.arange(n_rows // tm, dtype=jnp.int32) * tm
    tile_eid = jnp.minimum(jnp.sum((tile_start[:, None] >= ends[None, :]).astype(jnp.int32), axis=1),
                           N_EXPERTS - 1)
    n_active = (ends[-1] // tm).astype(jnp.int32).reshape(1)
    return row_tok, dest.reshape(TOP_K, t), tile_eid, n_active


def _take_rows(a, idx):
    return a.at[idx].get(mode="promise_in_bounds")


def _dense_plan(t, tm):
    return jnp.zeros((t // tm,), jnp.int32), jnp.full((1,), t // tm, jnp.int32)


def _permute_w_in(w):
    b3 = 3 * BRANCH_DIM
    q3 = 3 * QK_DIM
    return jnp.concatenate([w[:, b3 + q3:], w[:, b3:b3 + q3], w[:, :b3]], axis=1).astype(BF16)


def _forward(cfg, x, c, ctx, c_ctx, w_mod, b_mod, g_norm, w_in, b_gate, w_fourier_out, w_sgu_out,
             w_attn_out, w_o, sgu_w, sgu_b, sgu_g, diff_lambda, diff_subln_g,
             ffn_w1, ffn_w3, ffn_w2, router_w, moe_w1, moe_w3, moe_w2):
    b, s, d = x.shape
    n_ctx = ctx.shape[1]
    t_lat, t_ctx = b * s, b * n_ctx
    cos, sin = _rope_tables(s)
    fft_tables = _fft_tables(s)

    c_rows = jnp.concatenate([c, c_ctx[None, :], jnp.zeros((8 - b - 1, d), F32)], axis=0)
    mod_all = _modulation(cfg, c_rows, w_mod, b_mod)

    xl = x.reshape(t_lat, d)
    xc = ctx.reshape(t_ctx, d)
    for l in range(cfg.depth):
        last = l == cfg.depth - 1
        lam_init = 0.8 - 0.6 * math.exp(-0.3 * l)
        mod_l = mod_all[l, :b].reshape(b, 6, d)
        mod_c = mod_all[l, b:b + 1].reshape(1, 6, d)
        g = g_norm[l].reshape(4, 1, d)
        w_in_l = _permute_w_in(w_in[l])
        wf, ws = w_fourier_out[l].astype(BF16), w_sgu_out[l].astype(BF16)
        wa, wo = w_attn_out[l].astype(BF16), w_o[l].astype(BF16)
        bg = b_gate[l].reshape(1, 3 * d)
        sw = sgu_w[l].astype(BF16)
        sbt = sgu_b[l].T
        sg = sgu_g[l].reshape(1, BRANCH_DIM)
        dl = diff_lambda[l]
        gsub = diff_subln_g[l].reshape(1, PAIR)
        pair_blk = lambda off: off // PAIR

        p_lat = _in_proj(cfg, xl, g[0], mod_l, s, w_in_l)
        p_lat3 = p_lat.reshape(b, s, cfg.n_in)
        qk_rot = _rope(cfg, p_lat3, cos, sin)
        if last:
            p_ctx3 = _in_proj(cfg, xc, g[0], mod_c, t_ctx,
                              w_in_l[:, cfg.off_k:cfg.off_f]).reshape(b, n_ctx, 2 * QK_DIM)
            kc_blk, vc_blk = 0, pair_blk(QK_DIM)
        else:
            p_ctx = _in_proj(cfg, xc, g[0], mod_c, t_ctx, w_in_l)
            p_ctx3 = p_ctx.reshape(b, n_ctx, cfg.n_in)
            kc_blk, vc_blk = pair_blk(cfg.off_k), pair_blk(cfg.off_va)
        al = _attention(cfg, qk_rot[0], 0, p_ctx3, kc_blk, p_ctx3, vc_blk, dl, gsub, lam_init, cfg.tq,
                        k_lat=qk_rot, v_arr=p_lat3, v_blk=pair_blk(cfg.off_va))
        four_l = _fourier_latent(cfg, p_lat, fft_tables)
        sgu_l = _sgu(cfg, p_lat, sw, sbt, sg)
        xl = _merge(cfg, p_lat, four_l, sgu_l, al.reshape(t_lat, QK_DIM), xl, bg, wf, ws, wa, wo, g[1], mod_l, s)
        if not last:
            ac = _attention(cfg, p_ctx3, pair_blk(cfg.off_q), p_ctx3, kc_blk, p_ctx3, vc_blk, dl, gsub,
                            lam_init, n_ctx)
            four_c = _fourier_ctx(cfg, p_ctx)
            sgu_c = _sgu(cfg, p_ctx, sw, sbt, sg)
            xc = _merge(cfg, p_ctx, four_c, sgu_c, ac.reshape(t_ctx, QK_DIM), xc, bg, wf, ws, wa, wo, g[1],
                        mod_c, t_ctx)

        i = l // 2
        streams = [(xl, mod_l, s)] + ([] if last else [(xc, mod_c, t_ctx)])
        outs = []
        if l % 2 == 0:
            w2 = ffn_w2[i][None].astype(BF16)
            for xs, mod, rpm in streams:
                t = xs.shape[0]
                fl = _norm_mod(cfg, xs, g[2], mod, rpm, 3, 4)
                tm_up = min(cfg.tm_up, t)
                eid_up, n_up = _dense_plan(t, tm_up)
                h = _ffn_up(cfg, fl, ffn_w1, ffn_w3, eid_up + i, n_up, tm_up)
                tm_dn = min(cfg.tm_down_dense, t)
                outs.append(_ffn_down(cfg, h, w2, *_dense_plan(t, tm_dn), tm_dn, cfg.tk_down_dense, x=xs,
                                      g=g[3], mod=mod, rows_per_mod=rpm))
        else:
            w1 = moe_w1.reshape(-1, d, cfg.dff)
            w3 = moe_w3.reshape(-1, d, cfg.dff)
            w2 = moe_w2.reshape(-1, cfg.dff, d)
            rwt = router_w[i].T
            for xs, mod, rpm in streams:
                tm = cfg.tm_down if xs.shape[0] >= N_EXPERTS * cfg.tm_down else cfg.tm_down_ctx
                fl, route, top_w, counts = _norm_mod(cfg, xs, g[2], mod, rpm, 3, 4, router_wt=rwt)
                row_tok, pos, tile_eid, n_active = _dispatch_plan(
                    route, counts[:N_EXPERTS, 0].astype(jnp.int32), tm)
                n_tiles = tile_eid.shape[0]
                n_chunks = cfg.moe_chunks if tm == cfg.tm_down and n_tiles % cfg.moe_chunks == 0 else 1
                tpc = n_tiles // n_chunks
                tm_dn = min(cfg.tm_down_cols, tm)
                rep = tm // tm_dn
                eid = tile_eid + i * N_EXPERTS
                y = None
                for c in range(n_chunks):
                    na_c = jnp.clip(n_active - c * tpc, 0, tpc)
                    eid_c = eid[c * tpc:(c + 1) * tpc]
                    a = _take_rows(fl, row_tok[c * tpc * tm:(c + 1) * tpc * tm])
                    h = _ffn_up(cfg, a, w1, w3, eid_c, na_c, tm)
                    y = _ffn_down_cols(cfg, h, w2, jnp.repeat(eid_c, rep), na_c * rep, tm_dn, y_prev=y,
                                       tile_off=c * tpc * rep, total_rows=n_tiles * tm)
                outs.append(_combine(cfg, _take_rows(y, pos[0]), _take_rows(y, pos[1]),
                                     top_w[:TOP_K].T, xs, g[3], mod, rpm))
        xl = outs[0]
        if not last:
            xc = outs[1]
    return xl.reshape(b, s, d)


def kernel(x, c, ctx, c_ctx, w_mod, b_mod, g_norm, w_in, b_gate, w_fourier_out, w_sgu_out, w_attn_out, w_o,
           sgu_w, sgu_b, sgu_g, diff_lambda, diff_subln_g, ffn_w1, ffn_w3, ffn_w2, router_w,
           moe_w1, moe_w3, moe_w2):
    return _forward(PROD, x, c, ctx, c_ctx, w_mod, b_mod, g_norm, w_in, b_gate, w_fourier_out, w_sgu_out,
                    w_attn_out, w_o, sgu_w, sgu_b, sgu_g, diff_lambda, diff_subln_g,
                    ffn_w1, ffn_w3, ffn_w2, router_w, moe_w1, moe_w3, moe_w2)
```

```python
import functools
import math
from typing import NamedTuple

import jax
import jax.numpy as jnp
from jax import lax
from jax.experimental import pallas as pl
from jax.experimental.pallas import tpu as pltpu

F32 = jnp.float32
BF16 = jnp.bfloat16

EPS = 1e-6
GRID_W = 64
ROPE_BASE = 10000.0
HEAD_DIM = 64
HEADS = 8
PAIR = 2 * HEAD_DIM
QK_DIM = HEADS * PAIR
GROUP_DIM = 128
GROUPS = 4
BRANCH_DIM = GROUPS * GROUP_DIM
CHUNK = 128
FFT_N1 = 128
FFT_A_PER_STEP = 4
FFT_B_PER_STEP = 8
N_EXPERTS = 8
TOP_K = 2
EXPERT_ROWS = 8
ATTN_SCALE = HEAD_DIM ** -0.5
Q_SCALE = ATTN_SCALE * math.log2(math.e)
VMEM_LIMIT = 56 * 1024 * 1024


class Cfg(NamedTuple):
    d: int
    batch: int
    seq: int
    ctx: int
    dff: int
    depth: int
    tm_norm: int
    tm_mm: int
    tn_mm: int
    tm_rope: int
    tq: int
    tk: int
    tm_sgu: int
    tm_merge: int
    tm_up: int
    tn_up: int
    tm_down: int
    tm_down_cols: int
    tn_down_cols: int
    tm_down_ctx: int
    moe_chunks: int
    tm_down_dense: int
    tk_down_dense: int
    tn_mod: int

    @property
    def n_in(self):
        return 3 * self.d + 3 * QK_DIM + 3 * BRANCH_DIM

    @property
    def off_q(self):
        return 3 * self.d

    @property
    def off_k(self):
        return self.off_q + QK_DIM

    @property
    def off_va(self):
        return self.off_k + QK_DIM

    @property
    def off_f(self):
        return self.off_va + QK_DIM

    @property
    def off_u(self):
        return self.off_f + BRANCH_DIM

    @property
    def off_v(self):
        return self.off_u + BRANCH_DIM


PROD = Cfg(d=2048, batch=4, seq=8192, ctx=256, dff=5632, depth=4,
           tm_norm=512, tm_mm=1024, tn_mm=1536, tm_rope=512, tq=1024, tk=256,
           tm_sgu=1024, tm_merge=512, tm_up=1024, tn_up=512, tm_down=1024, tm_down_cols=512, tn_down_cols=512,
           tm_down_ctx=256, moe_chunks=4, tm_down_dense=1024, tk_down_dense=512, tn_mod=1024)


def _params(*sem):
    return pltpu.CompilerParams(dimension_semantics=sem, vmem_limit_bytes=VMEM_LIMIT)


def _rms(y, g):
    return y * lax.rsqrt(jnp.mean(y * y, axis=-1, keepdims=True) + EPS) * g


def _mod_kernel(c_ref, w_ref, b_ref, o_ref):
    sc = jax.nn.silu(c_ref[...])
    o_ref[0] = jnp.dot(sc.astype(BF16), w_ref[0].astype(BF16), preferred_element_type=F32) + b_ref[0]


def _modulation(cfg, c_rows, w_mod, b_mod):
    depth, d, n6 = w_mod.shape
    tn = cfg.tn_mod
    return pl.pallas_call(
        _mod_kernel,
        grid=(depth, n6 // tn),
        in_specs=[pl.BlockSpec((8, d), lambda l, j: (0, 0)),
                  pl.BlockSpec((1, d, tn), lambda l, j: (l, 0, j)),
                  pl.BlockSpec((1, 1, tn), lambda l, j: (l, 0, j))],
        out_specs=pl.BlockSpec((1, 8, tn), lambda l, j: (l, 0, j)),
        out_shape=jax.ShapeDtypeStruct((depth, 8, n6), F32),
        compiler_params=_params("parallel", "parallel"),
        name="modulation",
    )(c_rows, w_mod, b_mod.reshape(depth, 1, n6))


def _norm_mod_kernel(x_ref, g_ref, mod_ref, o_ref, *, shift_idx, scale_idx):
    y = _rms(x_ref[...], g_ref[...])
    m = mod_ref[0]
    o_ref[...] = (y * (1.0 + m[scale_idx:scale_idx + 1]) + m[shift_idx:shift_idx + 1]).astype(o_ref.dtype)


def _norm_mod_router_kernel(x_ref, g_ref, mod_ref, rw_ref, tri_ref, o_ref, idx_ref, wt_ref, cnt_ref, *,
                            shift_idx, scale_idx):
    @pl.when(pl.program_id(0) == 0)
    def _():
        cnt_ref[...] = jnp.zeros(cnt_ref.shape, F32)

    y = _rms(x_ref[...], g_ref[...])
    m = mod_ref[0]
    h = y * (1.0 + m[scale_idx:scale_idx + 1]) + m[shift_idx:shift_idx + 1]
    o_ref[...] = h.astype(o_ref.dtype)
    logits = lax.dot_general(rw_ref[...], h, (((1,), (1,)), ((), ())),
                             precision=lax.Precision.HIGHEST, preferred_element_type=F32)
    row = lax.broadcasted_iota(jnp.int32, logits.shape, 0).astype(F32)
    m1 = jnp.max(logits, axis=0, keepdims=True)
    i1 = jnp.min(jnp.where(logits == m1, row, float(N_EXPERTS)), axis=0, keepdims=True)
    rest = jnp.where(row == i1, -jnp.inf, logits)
    m2 = jnp.max(rest, axis=0, keepdims=True)
    i2 = jnp.min(jnp.where(rest == m2, row, float(N_EXPERTS)), axis=0, keepdims=True)
    e2 = jnp.exp(m2 - m1)
    w1 = 1.0 / (1.0 + e2)
    w2 = e2 / (1.0 + e2)
    wt_ref[...] = jnp.where(row == 0.0, w1, jnp.where(row == 1.0, w2, 0.0))
    pick1 = jnp.where(row == i1, 1.0, 0.0)
    pick2 = jnp.where(row == i2, 1.0, 0.0)
    pre1 = jnp.dot(pick1.astype(BF16), tri_ref[...], preferred_element_type=F32)
    pre2 = jnp.dot(pick2.astype(BF16), tri_ref[...], preferred_element_type=F32)
    n1 = jnp.sum(pick1, axis=1, keepdims=True)
    n2 = jnp.sum(pick2, axis=1, keepdims=True)
    seen = cnt_ref[:, 0:1]
    r1 = jnp.sum(pick1 * (pre1 - 1.0 + seen), axis=0, keepdims=True)
    r2 = jnp.sum(pick2 * (pre2 - 1.0 + seen + n1), axis=0, keepdims=True)
    cnt_ref[...] = cnt_ref[...] + (n1 + n2)
    idx_ref[...] = jnp.where(row == 0.0, i1, jnp.where(row == 1.0, i2, jnp.where(row == 2.0, r1, jnp.where(
        row == 3.0, r2, 0.0)))).astype(jnp.int32)


def _norm_mod(cfg, x, g, mod, rows_per_mod, shift_idx, scale_idx, router_wt=None):
    t, d = x.shape
    tm = min(cfg.tm_norm, t)
    in_specs = [pl.BlockSpec((tm, d), lambda i: (i, 0)),
                pl.BlockSpec((1, d), lambda i: (0, 0)),
                pl.BlockSpec((1, 6, d), lambda i: (i * tm // rows_per_mod, 0, 0))]
    h_spec = pl.BlockSpec((tm, d), lambda i: (i, 0))
    h_shape = jax.ShapeDtypeStruct((t, d), BF16)
    if router_wt is None:
        return pl.pallas_call(
            functools.partial(_norm_mod_kernel, shift_idx=shift_idx, scale_idx=scale_idx),
            grid=(t // tm,), in_specs=in_specs, out_specs=h_spec, out_shape=h_shape,
            compiler_params=_params("parallel"), name="norm_mod",
        )(x, g, mod)
    r_spec = pl.BlockSpec((EXPERT_ROWS, tm), lambda i: (0, i))
    tri = jnp.triu(jnp.ones((tm, tm), BF16))
    return pl.pallas_call(
        functools.partial(_norm_mod_router_kernel, shift_idx=shift_idx, scale_idx=scale_idx),
        grid=(t // tm,),
        in_specs=in_specs + [pl.BlockSpec((N_EXPERTS, d), lambda i: (0, 0)),
                             pl.BlockSpec((tm, tm), lambda i: (0, 0))],
        out_specs=[h_spec, r_spec, r_spec, pl.BlockSpec((EXPERT_ROWS, 128), lambda i: (0, 0))],
        out_shape=[h_shape, jax.ShapeDtypeStruct((EXPERT_ROWS, t), jnp.int32),
                   jax.ShapeDtypeStruct((EXPERT_ROWS, t), F32),
                   jax.ShapeDtypeStruct((EXPERT_ROWS, 128), F32)],
        compiler_params=_params("arbitrary"), name="norm_mod_router",
    )(x, g, mod, router_wt, tri)


def _in_proj_kernel(x_ref, g_ref, mod_ref, w_ref, o_ref, h_ref):
    @pl.when(pl.program_id(1) == 0)
    def _():
        m = mod_ref[0]
        h_ref[...] = (_rms(x_ref[...], g_ref[...]) * (1.0 + m[1:2]) + m[0:1]).astype(h_ref.dtype)

    o_ref[...] = jnp.dot(h_ref[...], w_ref[...], preferred_element_type=F32).astype(o_ref.dtype)


def _in_proj(cfg, x, g, mod, rows_per_mod, w):
    t, d = x.shape
    n = w.shape[1]
    tm = min(cfg.tm_mm, t)
    tn = cfg.tn_mm if n % cfg.tn_mm == 0 else 512
    return pl.pallas_call(
        _in_proj_kernel,
        grid=(t // tm, n // tn),
        in_specs=[pl.BlockSpec((tm, d), lambda i, j: (i, 0)),
                  pl.BlockSpec((1, d), lambda i, j: (0, 0)),
                  pl.BlockSpec((1, 6, d), lambda i, j: (i * tm // rows_per_mod, 0, 0)),
                  pl.BlockSpec((d, tn), lambda i, j: (0, j))],
        out_specs=pl.BlockSpec((tm, tn), lambda i, j: (i, j)),
        out_shape=jax.ShapeDtypeStruct((t, n), BF16),
        scratch_shapes=[pltpu.VMEM((tm, d), BF16)],
        compiler_params=_params("parallel", "arbitrary"), name="in_proj",
    )(x, g, mod, w)


def _rope_tables(seq):
    rows = seq // GRID_W
    r = jnp.repeat(jnp.arange(rows, dtype=F32), GRID_W)
    col = jnp.tile(jnp.arange(GRID_W, dtype=F32), rows)
    n_freq = HEAD_DIM // 4
    inv = ROPE_BASE ** (-jnp.arange(n_freq, dtype=F32) / n_freq)
    ang = jnp.concatenate([r[:, None] * inv, col[:, None] * inv], axis=-1)
    cos, sin = jnp.cos(ang), jnp.sin(ang)
    return jnp.tile(cos, (1, 4)), jnp.tile(jnp.concatenate([-sin, sin], axis=-1), (1, 2))


def _rope_kernel(p_ref, cos_ref, sin_ref, o_ref):
    scale = jnp.where(pl.program_id(2) == 0, Q_SCALE, 1.0).astype(F32)
    cos = cos_ref[...] * scale
    sin = sin_ref[...] * scale
    lane = lax.broadcasted_iota(jnp.int32, cos.shape, 1)
    first_half = (lane % HEAD_DIM) < (HEAD_DIM // 2)
    for c in range(QK_DIM // PAIR):
        t = p_ref[0, :, c * PAIR:(c + 1) * PAIR].astype(F32)
        partner = jnp.where(first_half, pltpu.roll(t, PAIR - HEAD_DIM // 2, 1), pltpu.roll(t, HEAD_DIM // 2, 1))
        o_ref[0, 0, :, c * PAIR:(c + 1) * PAIR] = (t * cos + partner * sin).astype(o_ref.dtype)


def _rope(cfg, p3, cos, sin):
    b, s, _ = p3.shape
    tm = cfg.tm_rope
    qblk = cfg.off_q // QK_DIM
    return pl.pallas_call(
        _rope_kernel,
        grid=(b, s // tm, 2),
        in_specs=[pl.BlockSpec((1, tm, QK_DIM), lambda bi, i, j: (bi, i, qblk + j)),
                  pl.BlockSpec((tm, PAIR), lambda bi, i, j: (i, 0)),
                  pl.BlockSpec((tm, PAIR), lambda bi, i, j: (i, 0))],
        out_specs=pl.BlockSpec((1, 1, tm, QK_DIM), lambda bi, i, j: (j, bi, i, 0)),
        out_shape=jax.ShapeDtypeStruct((2, b, s, QK_DIM), BF16),
        compiler_params=_params("parallel", "parallel", "parallel"), name="rope",
    )(p3, cos, sin)


def _attn_kernel(*refs, tq, tk, n_lat, q_prescaled):
    if n_lat:
        q_ref, kc_ref, vc_ref, k_ref, v_ref, dl_ref, g_ref, li_ref, o_ref, q2_ref, vx_ref = refs
    else:
        q_ref, kc_ref, vc_ref, dl_ref, g_ref, li_ref, o_ref, q2_ref, vx_ref = refs
    lc = kc_ref.shape[1]

    @pl.when(pl.program_id(2) == 0)
    def _():
        vx_ref[:lc, :PAIR] = vc_ref[0]
        vx_ref[:lc, PAIR:] = jnp.ones((lc, PAIR), BF16)
        if n_lat:
            vx_ref[lc:, :PAIR] = v_ref[0]
            vx_ref[lc:, PAIR:] = jnp.ones((n_lat * tk, PAIR), BF16)

    q = q_ref[0]
    if not q_prescaled:
        q = (q.astype(F32) * Q_SCALE).astype(BF16)
    lane = lax.broadcasted_iota(jnp.int32, q.shape, 1)
    zero = jnp.zeros_like(q)
    q2_ref[:tq] = jnp.where(lane < HEAD_DIM, q, zero)
    q2_ref[tq:] = jnp.where(lane >= HEAD_DIM, q, zero)
    q2 = q2_ref[...]

    chunks = [(kc_ref[0], 0, lc)]
    for j in range(n_lat):
        chunks.append((k_ref[0, 0, j * tk:(j + 1) * tk, :], lc + j * tk, tk))
    m = jnp.full((2 * tq, 1), -jnp.inf, F32)
    acc = jnp.zeros((2 * tq, 2 * PAIR), F32)
    for k, off, n in chunks:
        s = lax.dot_general(q2, k, (((1,), (1,)), ((), ())), preferred_element_type=F32)
        m_new = jnp.maximum(m, jnp.max(s, axis=-1, keepdims=True))
        p = jnp.exp2(s - m_new).astype(BF16)
        acc = jnp.exp2(m - m_new) * acc + jnp.dot(p, vx_ref[off:off + n, :], preferred_element_type=F32)
        m = m_new

    o12 = acc[:, :PAIR] / acc[:, PAIR:]
    dl = dl_ref[...]
    lam_init = li_ref[...]
    lam = (jnp.exp(jnp.sum(dl[0:1] * dl[1:2], axis=-1, keepdims=True))
           - jnp.exp(jnp.sum(dl[2:3] * dl[3:4], axis=-1, keepdims=True)) + lam_init)
    o = o12[:tq] - lam * o12[tq:]
    o_ref[0] = (_rms(o, g_ref[...]) * (1.0 - lam_init)).astype(o_ref.dtype)


def _attention(cfg, q_arr, q_blk, kc_arr, kc_blk, vc_arr, vc_blk, dl, g_sub, lam_init, tq,
               k_lat=None, v_arr=None, v_blk=0):
    b, lq = q_arr.shape[0], q_arr.shape[1]
    lc = kc_arr.shape[1]
    in_specs = [pl.BlockSpec((1, tq, PAIR), lambda bi, h, i: (bi, i, q_blk + h)),
                pl.BlockSpec((1, lc, PAIR), lambda bi, h, i: (bi, 0, kc_blk + h)),
                pl.BlockSpec((1, lc, PAIR), lambda bi, h, i: (bi, 0, vc_blk + h))]
    args = [q_arr, kc_arr, vc_arr]
    n_lat = 0
    if k_lat is not None:
        s = k_lat.shape[2]
        n_lat = s // cfg.tk
        in_specs += [pl.BlockSpec((1, 1, s, PAIR), lambda bi, h, i: (1, bi, 0, h)),
                     pl.BlockSpec((1, s, PAIR), lambda bi, h, i: (bi, 0, v_blk + h))]
        args += [k_lat, v_arr]
    in_specs += [pl.BlockSpec((4, HEAD_DIM), lambda bi, h, i: (0, 0)),
                 pl.BlockSpec((1, PAIR), lambda bi, h, i: (0, 0)),
                 pl.BlockSpec((1, PAIR), lambda bi, h, i: (0, 0))]
    args += [dl, g_sub, jnp.full((1, PAIR), lam_init, F32)]
    lk = lc + n_lat * cfg.tk
    return pl.pallas_call(
        functools.partial(_attn_kernel, tq=tq, tk=cfg.tk, n_lat=n_lat, q_prescaled=k_lat is not None),
        grid=(b, HEADS, lq // tq),
        in_specs=in_specs,
        out_specs=pl.BlockSpec((1, tq, PAIR), lambda bi, h, i: (bi, i, h)),
        out_shape=jax.ShapeDtypeStruct((b, lq, QK_DIM), BF16),
        scratch_shapes=[pltpu.VMEM((2 * tq, PAIR), BF16), pltpu.VMEM((lk, 2 * PAIR), BF16)],
        compiler_params=_params("parallel", "parallel", "arbitrary"),
        name="diff_attention" if n_lat else "diff_attention_ctx",
    )(*args)


def _dft_cos_sin(n, scale=1.0):
    j = jnp.arange(n, dtype=jnp.int32)
    ang = ((j[:, None] * j[None, :]) % n).astype(F32) * (2.0 * math.pi / n)
    return jnp.cos(ang) * scale, jnp.sin(ang) * scale


def _fft_tables(seq):
    n1, n2 = FFT_N1, seq // FFT_N1
    c1, s1 = _dft_cos_sin(n1)
    stage1 = jnp.concatenate([c1, -s1], axis=0).astype(BF16)
    k1 = jnp.arange(n1, dtype=jnp.int32)
    m = jnp.arange(n2, dtype=jnp.int32)
    ang = (k1[:, None] * m[None, :]).astype(F32) * (2.0 * math.pi / seq)
    tw_cos, tw_sin = jnp.cos(ang), jnp.sin(ang)
    c2, s2 = _dft_cos_sin(n2)
    stage2 = jnp.concatenate([jnp.concatenate([c2, s2], axis=1),
                              jnp.concatenate([-s2, c2], axis=1)], axis=0).astype(BF16)
    cc, sc = _dft_cos_sin(GROUP_DIM, scale=(seq * GROUP_DIM) ** -0.5)
    return stage1, tw_cos, tw_sin, stage2, cc.astype(BF16), sc.astype(BF16)


def _fft_fused_kernel(x_ref, w1_ref, twc_ref, tws_ref, w2_ref, cc_ref, sc_ref, o_ref, xs_ref, bre_ref, bim_ref,
                      *, n2):
    n1 = FFT_N1
    xs_ref[...] = x_ref[0].astype(F32)
    twc = twc_ref[...]
    tws = tws_ref[...]
    for m in range(0, n2, FFT_A_PER_STEP):
        cols = [xs_ref[pl.ds(m + t, n1, stride=n2), :] for t in range(FFT_A_PER_STEP)]
        a = jnp.dot(w1_ref[...], jnp.concatenate(cols, axis=1).astype(BF16), preferred_element_type=F32)
        for t in range(FFT_A_PER_STEP):
            re = a[:n1, t * GROUP_DIM:(t + 1) * GROUP_DIM]
            im = a[n1:, t * GROUP_DIM:(t + 1) * GROUP_DIM]
            c = twc[:, m + t:m + t + 1]
            s = tws[:, m + t:m + t + 1]
            bre_ref[(m + t) * n1:(m + t + 1) * n1, :] = re * c + im * s
            bim_ref[(m + t) * n1:(m + t + 1) * n1, :] = im * c - re * s
    pb = FFT_B_PER_STEP
    for k in range(0, n1, pb):
        z = [jnp.concatenate([bre_ref[pl.ds(k + t, n2, stride=n1), :],
                              bim_ref[pl.ds(k + t, n2, stride=n1), :]], axis=0) for t in range(pb)]
        y = jnp.dot(w2_ref[...], jnp.concatenate(z, axis=1).astype(BF16),
                    preferred_element_type=F32).astype(BF16)
        yre = jnp.concatenate([y[:n2, t * GROUP_DIM:(t + 1) * GROUP_DIM] for t in range(pb)], axis=0)
        yim = jnp.concatenate([y[n2:, t * GROUP_DIM:(t + 1) * GROUP_DIM] for t in range(pb)], axis=0)
        o = (jnp.dot(yre, cc_ref[...], preferred_element_type=F32)
             + jnp.dot(yim, sc_ref[...], preferred_element_type=F32))
        for t in range(pb):
            o_ref[0, pl.ds(k + t, n2, stride=n1), :] = o[t * n2:(t + 1) * n2]


def _fourier_latent_fused(cfg, p_lat3, tables):
    b, s = cfg.batch, cfg.seq
    n1, n2 = FFT_N1, s // FFT_N1
    stage1, tw_cos, tw_sin, stage2, cc, sc = tables
    gblk = cfg.off_f // GROUP_DIM
    full = lambda bi, g: (0, 0)
    out = pl.pallas_call(
        functools.partial(_fft_fused_kernel, n2=n2),
        grid=(b, GROUPS),
        in_specs=[pl.BlockSpec((1, s, GROUP_DIM), lambda bi, g: (bi, 0, gblk + g)),
                  pl.BlockSpec((2 * n1, n1), full), pl.BlockSpec((n1, n2), full), pl.BlockSpec((n1, n2), full),
                  pl.BlockSpec((2 * n2, 2 * n2), full),
                  pl.BlockSpec((GROUP_DIM, GROUP_DIM), full), pl.BlockSpec((GROUP_DIM, GROUP_DIM), full)],
        out_specs=pl.BlockSpec((1, s, GROUP_DIM), lambda bi, g: (bi, 0, g)),
        out_shape=jax.ShapeDtypeStruct((b, s, BRANCH_DIM), F32),
        scratch_shapes=[pltpu.VMEM((s, GROUP_DIM), F32)] * 3,
        compiler_params=_params("parallel", "parallel"), name="fft_fused",
    )(p_lat3, stage1, tw_cos, tw_sin, stage2, cc, sc)
    return out.reshape(b * s, BRANCH_DIM)


def _dft_dense_kernel(x_ref, c_ref, s_ref, cc_ref, sc_ref, o_ref):
    x = x_ref[0]
    yr = jnp.dot(c_ref[...], x, preferred_element_type=F32).astype(BF16)
    yi = (-jnp.dot(s_ref[...], x, preferred_element_type=F32)).astype(BF16)
    for g in range(GROUPS):
        sl = slice(g * GROUP_DIM, (g + 1) * GROUP_DIM)
        o_ref[0, :, sl] = (jnp.dot(yr[:, sl], cc_ref[...], preferred_element_type=F32)
                           + jnp.dot(yi[:, sl], sc_ref[...], preferred_element_type=F32)).astype(o_ref.dtype)


def _fourier_ctx(cfg, p_ctx):
    b, n = cfg.batch, cfg.ctx
    cn, sn = _dft_cos_sin(n)
    cc, sc = _dft_cos_sin(GROUP_DIM, scale=(n * GROUP_DIM) ** -0.5)
    fblk = cfg.off_f // BRANCH_DIM
    full = lambda bi: (0, 0)
    out = pl.pallas_call(
        _dft_dense_kernel,
        grid=(b,),
        in_specs=[pl.BlockSpec((1, n, BRANCH_DIM), lambda bi: (bi, 0, fblk)),
                  pl.BlockSpec((n, n), full), pl.BlockSpec((n, n), full),
                  pl.BlockSpec((GROUP_DIM, GROUP_DIM), full), pl.BlockSpec((GROUP_DIM, GROUP_DIM), full)],
        out_specs=pl.BlockSpec((1, n, BRANCH_DIM), lambda bi: (bi, 0, 0)),
        out_shape=jax.ShapeDtypeStruct((b, n, BRANCH_DIM), BF16),
        compiler_params=_params("parallel"), name="dft_ctx",
    )(p_ctx.reshape(b, n, cfg.n_in), cn.astype(BF16), sn.astype(BF16), cc.astype(BF16), sc.astype(BF16))
    return out.reshape(b * n, BRANCH_DIM)


def _sgu_kernel(u_ref, v_ref, g_ref, w_ref, b_ref, o_ref, *, n_chunks):
    gv = g_ref[...]
    for c in range(n_chunks):
        rows = slice(c * CHUNK, (c + 1) * CHUNK)
        u = jax.nn.gelu(u_ref[rows, :].astype(F32))
        v = jax.nn.gelu(v_ref[rows, :].astype(F32))
        vc = v - jnp.mean(v, axis=-1, keepdims=True)
        vn = (vc * lax.rsqrt(jnp.mean(vc * vc, axis=-1, keepdims=True) + EPS) * gv).astype(BF16)
        for g in range(GROUPS):
            sl = slice(g * GROUP_DIM, (g + 1) * GROUP_DIM)
            sv = jnp.dot(w_ref[g], vn[:, sl], preferred_element_type=F32) + b_ref[:, g:g + 1]
            o_ref[rows, sl] = (u[:, sl] * sv).astype(o_ref.dtype)


def _sgu(cfg, p2, sgu_w, sgu_bt, sgu_g):
    t = p2.shape[0]
    tm = min(cfg.tm_sgu, t)
    ublk, vblk = cfg.off_u // BRANCH_DIM, cfg.off_v // BRANCH_DIM
    return pl.pallas_call(
        functools.partial(_sgu_kernel, n_chunks=tm // CHUNK),
        grid=(t // tm,),
        in_specs=[pl.BlockSpec((tm, BRANCH_DIM), lambda i: (i, ublk)),
                  pl.BlockSpec((tm, BRANCH_DIM), lambda i: (i, vblk)),
                  pl.BlockSpec((1, BRANCH_DIM), lambda i: (0, 0)),
                  pl.BlockSpec((GROUPS, CHUNK, CHUNK), lambda i: (0, 0, 0)),
                  pl.BlockSpec((CHUNK, GROUPS), lambda i: (0, 0))],
        out_specs=pl.BlockSpec((tm, BRANCH_DIM), lambda i: (i, 0)),
        out_shape=jax.ShapeDtypeStruct((t, BRANCH_DIM), BF16),
        compiler_params=_params("parallel"), name="spatial_gating",
    )(p2, p2, sgu_g, sgu_w, sgu_bt)


def _merge_kernel(pg0_ref, pg1_ref, pg2_ref, f_ref, s_ref, a_ref, x_ref, bg_ref,
                  wf_ref, ws_ref, wa_ref, wo_ref, g_ref, mod_ref, o_ref, *, d):
    def gate(pg_ref, k):
        return jax.nn.sigmoid(pg_ref[...].astype(F32) + bg_ref[:, k * d:(k + 1) * d])

    merged = gate(pg0_ref, 0) * jnp.dot(f_ref[...].astype(BF16), wf_ref[...], preferred_element_type=F32)
    merged += gate(pg1_ref, 1) * jnp.dot(s_ref[...], ws_ref[...], preferred_element_type=F32)
    merged += gate(pg2_ref, 2) * jnp.dot(a_ref[...], wa_ref[...], preferred_element_type=F32)
    y = jnp.dot(merged.astype(BF16), wo_ref[...], preferred_element_type=F32)
    o_ref[...] = x_ref[...] + mod_ref[0][2:3] * _rms(y, g_ref[...])


def _resident(shape):
    return pl.BlockSpec(shape, lambda *_: (0,) * len(shape), pipeline_mode=pl.Buffered(1))


def _merge(cfg, p2, four, sgu, attn, x, b_gate, wf, ws, wa, wo, g, mod, rows_per_mod):
    t, d = x.shape
    tm = cfg.tm_merge
    row = lambda i: (i, 0)
    return pl.pallas_call(
        functools.partial(_merge_kernel, d=d),
        grid=(t // tm,),
        in_specs=[pl.BlockSpec((tm, d), lambda i: (i, 0)),
                  pl.BlockSpec((tm, d), lambda i: (i, 1)),
                  pl.BlockSpec((tm, d), lambda i: (i, 2)),
                  pl.BlockSpec((tm, BRANCH_DIM), row), pl.BlockSpec((tm, BRANCH_DIM), row),
                  pl.BlockSpec((tm, QK_DIM), row), pl.BlockSpec((tm, d), row),
                  _resident((1, 3 * d)), _resident((BRANCH_DIM, d)), _resident((BRANCH_DIM, d)),
                  _resident((QK_DIM, d)), _resident((d, d)), _resident((1, d)),
                  pl.BlockSpec((1, 6, d), lambda i: (i * tm // rows_per_mod, 0, 0))],
        out_specs=pl.BlockSpec((tm, d), row),
        out_shape=jax.ShapeDtypeStruct((t, d), F32),
        compiler_params=_params("parallel"), name="merge",
    )(p2, p2, p2, four, sgu, attn, x, b_gate, wf, ws, wa, wo, g, mod)


def _last_active(i, n_active_ref):
    return jnp.minimum(i, jnp.maximum(n_active_ref[0] - 1, 0))


def _ffn_up_kernel(eid_ref, nact_ref, a_ref, w1_ref, w3_ref, o_ref, w1b_ref, w3b_ref):
    i = pl.program_id(1)
    active = i < nact_ref[0]
    fresh = jnp.logical_or(i == 0, eid_ref[i] != eid_ref[jnp.maximum(i - 1, 0)])

    @pl.when(jnp.logical_and(active, fresh))
    def _():
        w1b_ref[...] = w1_ref[0].astype(BF16)
        w3b_ref[...] = w3_ref[0].astype(BF16)

    @pl.when(active)
    def _():
        a = a_ref[...]
        h1 = jnp.dot(a, w1b_ref[...], preferred_element_type=F32)
        h3 = jnp.dot(a, w3b_ref[...], preferred_element_type=F32)
        o_ref[...] = (jax.nn.silu(h1) * h3).astype(o_ref.dtype)

    @pl.when(jnp.logical_not(active))
    def _():
        o_ref[...] = jnp.zeros(o_ref.shape, o_ref.dtype)


def _ffn_up(cfg, a, w1, w3, tile_eid, n_active, tm):
    r, d = a.shape
    dff = w1.shape[2]
    tn = cfg.tn_up
    grid_spec = pltpu.PrefetchScalarGridSpec(
        num_scalar_prefetch=2,
        grid=(dff // tn, r // tm),
        in_specs=[pl.BlockSpec((tm, d), lambda j, i, eid, na: (_last_active(i, na), 0)),
                  pl.BlockSpec((1, d, tn), lambda j, i, eid, na: (eid[i], 0, j)),
                  pl.BlockSpec((1, d, tn), lambda j, i, eid, na: (eid[i], 0, j))],
        out_specs=pl.BlockSpec((tm, tn), lambda j, i, eid, na: (i, j)),
        scratch_shapes=[pltpu.VMEM((d, tn), BF16), pltpu.VMEM((d, tn), BF16)],
    )
    return pl.pallas_call(
        _ffn_up_kernel, grid_spec=grid_spec,
        out_shape=jax.ShapeDtypeStruct((r, dff), BF16),
        compiler_params=_params("arbitrary", "arbitrary"), name="swiglu_up",
    )(tile_eid, n_active, a, w1, w3)


def _ffn_down_kernel(eid_ref, nact_ref, *refs, residual):
    if residual:
        h_ref, w_ref, x_ref, g_ref, mod_ref, o_ref, acc_ref = refs
    else:
        h_ref, w_ref, o_ref, acc_ref = refs
    k = pl.program_id(1)

    @pl.when(k == 0)
    def _():
        acc_ref[...] = jnp.zeros(acc_ref.shape, F32)

    @pl.when(pl.program_id(0) < nact_ref[0])
    def _():
        acc_ref[...] += jnp.dot(h_ref[...], w_ref[0], preferred_element_type=F32)

    @pl.when(k == pl.num_programs(1) - 1)
    def _():
        if residual:
            o_ref[...] = x_ref[...] + mod_ref[0][5:6] * _rms(acc_ref[...], g_ref[...])
        else:
            o_ref[...] = acc_ref[...].astype(o_ref.dtype)


def _ffn_down(cfg, h, w2, tile_eid, n_active, tm, tk, x=None, g=None, mod=None, rows_per_mod=None):
    r, dff = h.shape
    d = w2.shape[2]
    residual = x is not None
    in_specs = [pl.BlockSpec((tm, tk), lambda i, k, eid, na: (_last_active(i, na), k)),
                pl.BlockSpec((1, tk, d), lambda i, k, eid, na: (eid[i], k, 0))]
    args = [h, w2]
    if residual:
        in_specs += [pl.BlockSpec((tm, d), lambda i, k, eid, na: (i, 0)),
                     pl.BlockSpec((1, d), lambda i, k, eid, na: (0, 0)),
                     pl.BlockSpec((1, 6, d), lambda i, k, eid, na: (i * tm // rows_per_mod, 0, 0))]
        args += [x, g, mod]
    grid_spec = pltpu.PrefetchScalarGridSpec(
        num_scalar_prefetch=2,
        grid=(r // tm, dff // tk),
        in_specs=in_specs,
        out_specs=pl.BlockSpec((tm, d), lambda i, k, eid, na: (i, 0)),
        scratch_shapes=[pltpu.VMEM((tm, d), F32)],
    )
    return pl.pallas_call(
        functools.partial(_ffn_down_kernel, residual=residual), grid_spec=grid_spec,
        out_shape=jax.ShapeDtypeStruct((r, d), F32 if residual else BF16),
        compiler_params=_params("parallel", "arbitrary"),
        name="swiglu_down_residual" if residual else "swiglu_down",
    )(tile_eid, n_active, *args)


def _ffn_down_cols_kernel(eid_ref, nact_ref, h_ref, w_ref, *refs):
    o_ref, wb_ref = refs[-2:]
    i = pl.program_id(1)
    active = i < nact_ref[0]
    fresh = jnp.logical_or(i == 0, eid_ref[i] != eid_ref[jnp.maximum(i - 1, 0)])

    @pl.when(jnp.logical_and(active, fresh))
    def _():
        wb_ref[...] = w_ref[0].astype(BF16)

    @pl.when(active)
    def _():
        o_ref[...] = jnp.dot(h_ref[...], wb_ref[...], preferred_element_type=F32).astype(o_ref.dtype)

    @pl.when(jnp.logical_not(active))
    def _():
        o_ref[...] = jnp.zeros(o_ref.shape, o_ref.dtype)


def _ffn_down_cols(cfg, h, w2, tile_eid, n_active, tm, y_prev=None, tile_off=0, total_rows=None):
    r, dff = h.shape
    d = w2.shape[2]
    tn = cfg.tn_down_cols
    in_specs = [pl.BlockSpec((tm, dff), lambda n, i, eid, na: (_last_active(i, na), 0)),
                pl.BlockSpec((1, dff, tn), lambda n, i, eid, na: (eid[i], 0, n))]
    args = [tile_eid, n_active, h, w2]
    aliases = {}
    if y_prev is not None:
        in_specs.append(pl.BlockSpec(memory_space=pl.ANY))
        args.append(y_prev)
        aliases = {len(args) - 1: 0}
    grid_spec = pltpu.PrefetchScalarGridSpec(
        num_scalar_prefetch=2,
        grid=(d // tn, r // tm),
        in_specs=in_specs,
        out_specs=pl.BlockSpec((tm, tn), lambda n, i, eid, na: (i + tile_off, n)),
        scratch_shapes=[pltpu.VMEM((dff, tn), BF16)],
    )
    return pl.pallas_call(
        _ffn_down_cols_kernel, grid_spec=grid_spec,
        out_shape=jax.ShapeDtypeStruct((total_rows or r, d), BF16),
        input_output_aliases=aliases,
        compiler_params=_params("arbitrary", "arbitrary"), name="swiglu_down_cols",
    )(*args)


def _combine_kernel(y1_ref, y2_ref, wt_ref, x_ref, g_ref, mod_ref, o_ref):
    wt = wt_ref[...]
    y = wt[:, 0:1] * y1_ref[...].astype(F32) + wt[:, 1:2] * y2_ref[...].astype(F32)
    o_ref[...] = x_ref[...] + mod_ref[0][5:6] * _rms(y, g_ref[...])


def _combine(cfg, y1, y2, wt, x, g, mod, rows_per_mod):
    t, d = x.shape
    tm = min(cfg.tm_norm, t)
    row = lambda i: (i, 0)
    return pl.pallas_call(
        _combine_kernel,
        grid=(t // tm,),
        in_specs=[pl.BlockSpec((tm, d), row), pl.BlockSpec((tm, d), row),
                  pl.BlockSpec((tm, TOP_K), row), pl.BlockSpec((tm, d), row),
                  pl.BlockSpec((1, d), lambda i: (0, 0)),
                  pl.BlockSpec((1, 6, d), lambda i: (i * tm // rows_per_mod, 0, 0))],
        out_specs=pl.BlockSpec((tm, d), row),
        out_shape=jax.ShapeDtypeStruct((t, d), F32),
        compiler_params=_params("parallel"), name="moe_combine",
    )(y1, y2, wt, x, g, mod)


def _dispatch_plan(route, counts, tm):
    t = route.shape[1]
    n_pairs = TOP_K * t
    n_rows = n_pairs + N_EXPERTS * tm
    e_flat = route[:TOP_K].reshape(n_pairs)
    rank = route[TOP_K:2 * TOP_K].reshape(n_pairs)
    padded = (counts + tm - 1) // tm * tm
    ends = jnp.cumsum(padded)
    starts = ends - padded
    onehot = e_flat[:, None] == jnp.arange(N_EXPERTS, dtype=jnp.int32)[None, :]
    dest = jnp.sum(jnp.where(onehot, starts[None, :], 0), axis=1) + rank
    tok = jnp.tile(jnp.arange(t, dtype=jnp.int32), TOP_K)
    row_tok = jnp.zeros((n_rows,), jnp.int32).at[dest].set(tok, mode="promise_in_bounds", unique_indices=True)
    tile_start = jnp.arange(n_rows // tm, dtype=jnp.int32) * tm
    tile_eid = jnp.minimum(jnp.sum((tile_start[:, None] >= ends[None, :]).astype(jnp.int32), axis=1),
                           N_EXPERTS - 1)
    n_active = (ends[-1] // tm).astype(jnp.int32).reshape(1)
    return row_tok, dest.reshape(TOP_K, t), tile_eid, n_active


def _take_rows(a, idx):
    return a.at[idx].get(mode="promise_in_bounds")


def _dense_plan(t, tm):
    return jnp.zeros((t // tm,), jnp.int32), jnp.full((1,), t // tm, jnp.int32)


def _permute_w_in(w):
    b3 = 3 * BRANCH_DIM
    q3 = 3 * QK_DIM
    return jnp.concatenate([w[:, b3 + q3:], w[:, b3:b3 + q3], w[:, :b3]], axis=1).astype(BF16)


def _forward(cfg, x, c, ctx, c_ctx, w_mod, b_mod, g_norm, w_in, b_gate, w_fourier_out, w_sgu_out,
             w_attn_out, w_o, sgu_w, sgu_b, sgu_g, diff_lambda, diff_subln_g,
             ffn_w1, ffn_w3, ffn_w2, router_w, moe_w1, moe_w3, moe_w2):
    b, s, d = x.shape
    n_ctx = ctx.shape[1]
    t_lat, t_ctx = b * s, b * n_ctx
    cos, sin = _rope_tables(s)
    fft_tables = _fft_tables(s)

    c_rows = jnp.concatenate([c, c_ctx[None, :], jnp.zeros((8 - b - 1, d), F32)], axis=0)
    mod_all = _modulation(cfg, c_rows, w_mod, b_mod)

    xl = x.reshape(t_lat, d)
    xc = ctx.reshape(t_ctx, d)
    for l in range(cfg.depth):
        last = l == cfg.depth - 1
        lam_init = 0.8 - 0.6 * math.exp(-0.3 * l)
        mod_l = mod_all[l, :b].reshape(b, 6, d)
        mod_c = mod_all[l, b:b + 1].reshape(1, 6, d)
        g = g_norm[l].reshape(4, 1, d)
        w_in_l = _permute_w_in(w_in[l])
        wf, ws = w_fourier_out[l].astype(BF16), w_sgu_out[l].astype(BF16)
        wa, wo = w_attn_out[l].astype(BF16), w_o[l].astype(BF16)
        bg = b_gate[l].reshape(1, 3 * d)
        sw = sgu_w[l].astype(BF16)
        sbt = sgu_b[l].T
        sg = sgu_g[l].reshape(1, BRANCH_DIM)
        dl = diff_lambda[l]
        gsub = diff_subln_g[l].reshape(1, PAIR)
        pair_blk = lambda off: off // PAIR

        p_lat = _in_proj(cfg, xl, g[0], mod_l, s, w_in_l)
        p_lat3 = p_lat.reshape(b, s, cfg.n_in)
        qk_rot = _rope(cfg, p_lat3, cos, sin)
        if last:
            p_ctx3 = _in_proj(cfg, xc, g[0], mod_c, t_ctx,
                              w_in_l[:, cfg.off_k:cfg.off_f]).reshape(b, n_ctx, 2 * QK_DIM)
            kc_blk, vc_blk = 0, pair_blk(QK_DIM)
        else:
            p_ctx = _in_proj(cfg, xc, g[0], mod_c, t_ctx, w_in_l)
            p_ctx3 = p_ctx.reshape(b, n_ctx, cfg.n_in)
            kc_blk, vc_blk = pair_blk(cfg.off_k), pair_blk(cfg.off_va)
        al = _attention(cfg, qk_rot[0], 0, p_ctx3, kc_blk, p_ctx3, vc_blk, dl, gsub, lam_init, cfg.tq,
                        k_lat=qk_rot, v_arr=p_lat3, v_blk=pair_blk(cfg.off_va))
        four_l = _fourier_latent_fused(cfg, p_lat3, fft_tables)
        sgu_l = _sgu(cfg, p_lat, sw, sbt, sg)
        xl = _merge(cfg, p_lat, four_l, sgu_l, al.reshape(t_lat, QK_DIM), xl, bg, wf, ws, wa, wo, g[1], mod_l, s)
        if not last:
            ac = _attention(cfg, p_ctx3, pair_blk(cfg.off_q), p_ctx3, kc_blk, p_ctx3, vc_blk, dl, gsub,
                            lam_init, n_ctx)
            four_c = _fourier_ctx(cfg, p_ctx)
            sgu_c = _sgu(cfg, p_ctx, sw, sbt, sg)
            xc = _merge(cfg, p_ctx, four_c, sgu_c, ac.reshape(t_ctx, QK_DIM), xc, bg, wf, ws, wa, wo, g[1],
                        mod_c, t_ctx)

        i = l // 2
        streams = [(xl, mod_l, s)] + ([] if last else [(xc, mod_c, t_ctx)])
        outs = []
        if l % 2 == 0:
            w2 = ffn_w2[i][None].astype(BF16)
            for xs, mod, rpm in streams:
                t = xs.shape[0]
                fl = _norm_mod(cfg, xs, g[2], mod, rpm, 3, 4)
                tm_up = min(cfg.tm_up, t)
                eid_up, n_up = _dense_plan(t, tm_up)
                h = _ffn_up(cfg, fl, ffn_w1, ffn_w3, eid_up + i, n_up, tm_up)
                tm_dn = min(cfg.tm_down_dense, t)
                outs.append(_ffn_down(cfg, h, w2, *_dense_plan(t, tm_dn), tm_dn, cfg.tk_down_dense, x=xs,
                                      g=g[3], mod=mod, rows_per_mod=rpm))
        else:
            w1 = moe_w1.reshape(-1, d, cfg.dff)
            w3 = moe_w3.reshape(-1, d, cfg.dff)
            w2 = moe_w2.reshape(-1, cfg.dff, d)
            rwt = router_w[i].T
            for xs, mod, rpm in streams:
                tm = cfg.tm_down if xs.shape[0] >= N_EXPERTS * cfg.tm_down else cfg.tm_down_ctx
                fl, route, top_w, counts = _norm_mod(cfg, xs, g[2], mod, rpm, 3, 4, router_wt=rwt)
                row_tok, pos, tile_eid, n_active = _dispatch_plan(
                    route, counts[:N_EXPERTS, 0].astype(jnp.int32), tm)
                n_tiles = tile_eid.shape[0]
                n_chunks = cfg.moe_chunks if tm == cfg.tm_down and n_tiles % cfg.moe_chunks == 0 else 1
                tpc = n_tiles // n_chunks
                tm_dn = min(cfg.tm_down_cols, tm)
                rep = tm // tm_dn
                eid = tile_eid + i * N_EXPERTS
                y = None
                for c in range(n_chunks):
                    na_c = jnp.clip(n_active - c * tpc, 0, tpc)
                    eid_c = eid[c * tpc:(c + 1) * tpc]
                    a = _take_rows(fl, row_tok[c * tpc * tm:(c + 1) * tpc * tm])
                    h = _ffn_up(cfg, a, w1, w3, eid_c, na_c, tm)
                    y = _ffn_down_cols(cfg, h, w2, jnp.repeat(eid_c, rep), na_c * rep, tm_dn, y_prev=y,
                                       tile_off=c * tpc * rep, total_rows=n_tiles * tm)
                outs.append(_combine(cfg, _take_rows(y, pos[0]), _take_rows(y, pos[1]),
                                     top_w[:TOP_K].T, xs, g[3], mod, rpm))
        xl = outs[0]
        if not last:
            xc = outs[1]
    return xl.reshape(b, s, d)


def kernel(x, c, ctx, c_ctx, w_mod, b_mod, g_norm, w_in, b_gate, w_fourier_out, w_sgu_out, w_attn_out, w_o,
           sgu_w, sgu_b, sgu_g, diff_lambda, diff_subln_g, ffn_w1, ffn_w3, ffn_w2, router_w,
           moe_w1, moe_w3, moe_w2):
    return _forward(PROD, x, c, ctx, c_ctx, w_mod, b_mod, g_norm, w_in, b_gate, w_fourier_out, w_sgu_out,
                    w_attn_out, w_o, sgu_w, sgu_b, sgu_g, diff_lambda, diff_subln_g,
                    ffn_w1, ffn_w3, ffn_w2, router_w, moe_w1, moe_w3, moe_w2)
```

```python
import functools
import math
from typing import NamedTuple

import jax
import jax.numpy as jnp
from jax import lax
from jax.experimental import pallas as pl
from jax.experimental.pallas import tpu as pltpu

F32 = jnp.float32
BF16 = jnp.bfloat16

EPS = 1e-6
GRID_W = 64
ROPE_BASE = 10000.0
HEAD_DIM = 64
HEADS = 8
PAIR = 2 * HEAD_DIM
QK_DIM = HEADS * PAIR
GROUP_DIM = 128
GROUPS = 4
BRANCH_DIM = GROUPS * GROUP_DIM
CHUNK = 128
FFT_N1 = 128
FFT_A_PER_STEP = 4
FFT_B_PER_STEP = 8
N_EXPERTS = 8
TOP_K = 2
EXPERT_ROWS = 8
ATTN_SCALE = HEAD_DIM ** -0.5
Q_SCALE = ATTN_SCALE * math.log2(math.e)
VMEM_LIMIT = 56 * 1024 * 1024


class Cfg(NamedTuple):
    d: int
    batch: int
    seq: int
    ctx: int
    dff: int
    depth: int
    tm_norm: int
    tm_mm: int
    tn_mm: int
    tm_rope: int
    tq: int
    tk: int
    tm_sgu: int
    tm_merge: int
    tm_up: int
    tn_up: int
    tm_down: int
    tm_down_cols: int
    tn_down_cols: int
    tm_down_ctx: int
    moe_chunks: int
    tm_down_dense: int
    tk_down_dense: int
    tn_mod: int

    @property
    def n_in(self):
        return 3 * self.d + 3 * QK_DIM + 3 * BRANCH_DIM

    @property
    def off_q(self):
        return 3 * self.d

    @property
    def off_k(self):
        return self.off_q + QK_DIM

    @property
    def off_va(self):
        return self.off_k + QK_DIM

    @property
    def off_f(self):
        return self.off_va + QK_DIM

    @property
    def off_u(self):
        return self.off_f + BRANCH_DIM

    @property
    def off_v(self):
        return self.off_u + BRANCH_DIM


PROD = Cfg(d=2048, batch=4, seq=8192, ctx=256, dff=5632, depth=4,
           tm_norm=512, tm_mm=1024, tn_mm=1536, tm_rope=512, tq=1024, tk=256,
           tm_sgu=1024, tm_merge=256, tm_up=1024, tn_up=512, tm_down=1024, tm_down_cols=512, tn_down_cols=512,
           tm_down_ctx=256, moe_chunks=2, tm_down_dense=1024, tk_down_dense=512, tn_mod=1024)


def _params(*sem):
    return pltpu.CompilerParams(dimension_semantics=sem, vmem_limit_bytes=VMEM_LIMIT)


def _rms(y, g):
    return y * lax.rsqrt(jnp.mean(y * y, axis=-1, keepdims=True) + EPS) * g


def _mod_kernel(c_ref, w_ref, b_ref, o_ref):
    sc = jax.nn.silu(c_ref[...])
    o_ref[0] = jnp.dot(sc.astype(BF16), w_ref[0].astype(BF16), preferred_element_type=F32) + b_ref[0]


def _modulation(cfg, c_rows, w_mod, b_mod):
    depth, d, n6 = w_mod.shape
    tn = cfg.tn_mod
    return pl.pallas_call(
        _mod_kernel,
        grid=(depth, n6 // tn),
        in_specs=[pl.BlockSpec((8, d), lambda l, j: (0, 0)),
                  pl.BlockSpec((1, d, tn), lambda l, j: (l, 0, j)),
                  pl.BlockSpec((1, 1, tn), lambda l, j: (l, 0, j))],
        out_specs=pl.BlockSpec((1, 8, tn), lambda l, j: (l, 0, j)),
        out_shape=jax.ShapeDtypeStruct((depth, 8, n6), F32),
        compiler_params=_params("parallel", "parallel"),
        name="modulation",
    )(c_rows, w_mod, b_mod.reshape(depth, 1, n6))


def _norm_mod_router_kernel(x_ref, g_ref, mod_ref, rw_ref, tri_ref, o_ref, idx_ref, wt_ref, cnt_ref, *,
                            shift_idx, scale_idx):
    @pl.when(pl.program_id(0) == 0)
    def _():
        cnt_ref[...] = jnp.zeros(cnt_ref.shape, F32)

    y = _rms(x_ref[...], g_ref[...])
    m = mod_ref[0]
    h = y * (1.0 + m[scale_idx:scale_idx + 1]) + m[shift_idx:shift_idx + 1]
    o_ref[...] = h.astype(o_ref.dtype)
    logits = lax.dot_general(rw_ref[...], h, (((1,), (1,)), ((), ())),
                             precision=lax.Precision.HIGHEST, preferred_element_type=F32)
    row = lax.broadcasted_iota(jnp.int32, logits.shape, 0).astype(F32)
    m1 = jnp.max(logits, axis=0, keepdims=True)
    i1 = jnp.min(jnp.where(logits == m1, row, float(N_EXPERTS)), axis=0, keepdims=True)
    rest = jnp.where(row == i1, -jnp.inf, logits)
    m2 = jnp.max(rest, axis=0, keepdims=True)
    i2 = jnp.min(jnp.where(rest == m2, row, float(N_EXPERTS)), axis=0, keepdims=True)
    e2 = jnp.exp(m2 - m1)
    w1 = 1.0 / (1.0 + e2)
    w2 = e2 / (1.0 + e2)
    wt_ref[...] = jnp.where(row == 0.0, w1, jnp.where(row == 1.0, w2, 0.0))
    pick1 = jnp.where(row == i1, 1.0, 0.0)
    pick2 = jnp.where(row == i2, 1.0, 0.0)
    pre1 = jnp.dot(pick1.astype(BF16), tri_ref[...], preferred_element_type=F32)
    pre2 = jnp.dot(pick2.astype(BF16), tri_ref[...], preferred_element_type=F32)
    n1 = jnp.sum(pick1, axis=1, keepdims=True)
    n2 = jnp.sum(pick2, axis=1, keepdims=True)
    seen = cnt_ref[:, 0:1]
    r1 = jnp.sum(pick1 * (pre1 - 1.0 + seen), axis=0, keepdims=True)
    r2 = jnp.sum(pick2 * (pre2 - 1.0 + seen + n1), axis=0, keepdims=True)
    cnt_ref[...] = cnt_ref[...] + (n1 + n2)
    idx_ref[...] = jnp.where(row == 0.0, i1, jnp.where(row == 1.0, i2, jnp.where(row == 2.0, r1, jnp.where(
        row == 3.0, r2, 0.0)))).astype(jnp.int32)


def _norm_mod_router(cfg, x, g, mod, rows_per_mod, shift_idx, scale_idx, router_wt):
    t, d = x.shape
    tm = min(cfg.tm_norm, t)
    in_specs = [pl.BlockSpec((tm, d), lambda i: (i, 0)),
                pl.BlockSpec((1, d), lambda i: (0, 0)),
                pl.BlockSpec((1, 6, d), lambda i: (i * tm // rows_per_mod, 0, 0))]
    h_spec = pl.BlockSpec((tm, d), lambda i: (i, 0))
    h_shape = jax.ShapeDtypeStruct((t, d), BF16)
    r_spec = pl.BlockSpec((EXPERT_ROWS, tm), lambda i: (0, i))
    tri = jnp.triu(jnp.ones((tm, tm), BF16))
    return pl.pallas_call(
        functools.partial(_norm_mod_router_kernel, shift_idx=shift_idx, scale_idx=scale_idx),
        grid=(t // tm,),
        in_specs=in_specs + [pl.BlockSpec((N_EXPERTS, d), lambda i: (0, 0)),
                             pl.BlockSpec((tm, tm), lambda i: (0, 0))],
        out_specs=[h_spec, r_spec, r_spec, pl.BlockSpec((EXPERT_ROWS, 128), lambda i: (0, 0))],
        out_shape=[h_shape, jax.ShapeDtypeStruct((EXPERT_ROWS, t), jnp.int32),
                   jax.ShapeDtypeStruct((EXPERT_ROWS, t), F32),
                   jax.ShapeDtypeStruct((EXPERT_ROWS, 128), F32)],
        compiler_params=_params("arbitrary"), name="norm_mod_router",
    )(x, g, mod, router_wt, tri)


def _in_proj_kernel(x_ref, g_ref, mod_ref, w_ref, o_ref, h_ref):
    @pl.when(pl.program_id(1) == 0)
    def _():
        m = mod_ref[0]
        h_ref[...] = (_rms(x_ref[...], g_ref[...]) * (1.0 + m[1:2]) + m[0:1]).astype(h_ref.dtype)

    o_ref[...] = jnp.dot(h_ref[...], w_ref[...], preferred_element_type=F32).astype(o_ref.dtype)


def _in_proj(cfg, x, g, mod, rows_per_mod, w):
    t, d = x.shape
    n = w.shape[1]
    tm = min(cfg.tm_mm, t)
    tn = cfg.tn_mm if n % cfg.tn_mm == 0 else 512
    return pl.pallas_call(
        _in_proj_kernel,
        grid=(t // tm, n // tn),
        in_specs=[pl.BlockSpec((tm, d), lambda i, j: (i, 0)),
                  pl.BlockSpec((1, d), lambda i, j: (0, 0)),
                  pl.BlockSpec((1, 6, d), lambda i, j: (i * tm // rows_per_mod, 0, 0)),
                  pl.BlockSpec((d, tn), lambda i, j: (0, j))],
        out_specs=pl.BlockSpec((tm, tn), lambda i, j: (i, j)),
        out_shape=jax.ShapeDtypeStruct((t, n), BF16),
        scratch_shapes=[pltpu.VMEM((tm, d), BF16)],
        compiler_params=_params("parallel", "arbitrary"), name="in_proj",
    )(x, g, mod, w)


def _rope_tables(seq):
    rows = seq // GRID_W
    r = jnp.repeat(jnp.arange(rows, dtype=F32), GRID_W)
    col = jnp.tile(jnp.arange(GRID_W, dtype=F32), rows)
    n_freq = HEAD_DIM // 4
    inv = ROPE_BASE ** (-jnp.arange(n_freq, dtype=F32) / n_freq)
    ang = jnp.concatenate([r[:, None] * inv, col[:, None] * inv], axis=-1)
    cos, sin = jnp.cos(ang), jnp.sin(ang)
    return jnp.tile(cos, (1, 4)), jnp.tile(jnp.concatenate([-sin, sin], axis=-1), (1, 2))


def _rotate_pair_block(t, cos, sin):
    lane = lax.broadcasted_iota(jnp.int32, t.shape, 1)
    first_half = (lane % HEAD_DIM) < (HEAD_DIM // 2)
    partner = jnp.where(first_half, pltpu.roll(t, PAIR - HEAD_DIM // 2, 1), pltpu.roll(t, HEAD_DIM // 2, 1))
    return t * cos + partner * sin


def _rope_kernel(p_ref, cos_ref, sin_ref, o_ref):
    scale = jnp.where(pl.program_id(2) == 0, Q_SCALE, 1.0).astype(F32)
    cos = cos_ref[...] * scale
    sin = sin_ref[...] * scale
    for c in range(QK_DIM // PAIR):
        t = p_ref[0, :, c * PAIR:(c + 1) * PAIR].astype(F32)
        o_ref[0, 0, :, c * PAIR:(c + 1) * PAIR] = _rotate_pair_block(t, cos, sin).astype(o_ref.dtype)


def _rope(cfg, p3, cos, sin):
    b, s, _ = p3.shape
    tm = cfg.tm_rope
    qblk = cfg.off_q // QK_DIM
    return pl.pallas_call(
        _rope_kernel,
        grid=(b, s // tm, 2),
        in_specs=[pl.BlockSpec((1, tm, QK_DIM), lambda bi, i, j: (bi, i, qblk + j)),
                  pl.BlockSpec((tm, PAIR), lambda bi, i, j: (i, 0)),
                  pl.BlockSpec((tm, PAIR), lambda bi, i, j: (i, 0))],
        out_specs=pl.BlockSpec((1, 1, tm, QK_DIM), lambda bi, i, j: (j, bi, i, 0)),
        out_shape=jax.ShapeDtypeStruct((2, b, s, QK_DIM), BF16),
        compiler_params=_params("parallel", "parallel", "parallel"), name="rope",
    )(p3, cos, sin)


def _attn_kernel(*refs, tq, tk, n_lat):
    if n_lat:
        q_ref, kc_ref, vc_ref, k_ref, v_ref, dl_ref, g_ref, li_ref, o_ref, q2_ref, vx_ref = refs
    else:
        q_ref, kc_ref, vc_ref, dl_ref, g_ref, li_ref, o_ref, q2_ref, vx_ref = refs
    lc = kc_ref.shape[1]

    @pl.when(pl.program_id(2) == 0)
    def _():
        vx_ref[:lc, :PAIR] = vc_ref[0]
        vx_ref[:lc, PAIR:] = jnp.ones((lc, PAIR), BF16)
        if n_lat:
            vx_ref[lc:, :PAIR] = v_ref[0]
            vx_ref[lc:, PAIR:] = jnp.ones((n_lat * tk, PAIR), BF16)

    q = q_ref[0]
    if not n_lat:
        q = (q.astype(F32) * Q_SCALE).astype(BF16)
    lane = lax.broadcasted_iota(jnp.int32, q.shape, 1)
    zero = jnp.zeros_like(q)
    q2_ref[:tq] = jnp.where(lane < HEAD_DIM, q, zero)
    q2_ref[tq:] = jnp.where(lane >= HEAD_DIM, q, zero)
    q2 = q2_ref[...]

    chunks = [(kc_ref[0], 0, lc)]
    for j in range(n_lat):
        chunks.append((k_ref[0, 0, j * tk:(j + 1) * tk, :], lc + j * tk, tk))
    m = jnp.full((2 * tq, 1), -jnp.inf, F32)
    acc = jnp.zeros((2 * tq, 2 * PAIR), F32)
    for k, off, n in chunks:
        s = lax.dot_general(q2, k, (((1,), (1,)), ((), ())), preferred_element_type=F32)
        m_new = jnp.maximum(m, jnp.max(s, axis=-1, keepdims=True))
        p = jnp.exp2(s - m_new).astype(BF16)
        acc = jnp.exp2(m - m_new) * acc + jnp.dot(p, vx_ref[off:off + n, :], preferred_element_type=F32)
        m = m_new

    o12 = acc[:, :PAIR] / acc[:, PAIR:]
    dl = dl_ref[...]
    lam_init = li_ref[...]
    lam = (jnp.exp(jnp.sum(dl[0:1] * dl[1:2], axis=-1, keepdims=True))
           - jnp.exp(jnp.sum(dl[2:3] * dl[3:4], axis=-1, keepdims=True)) + lam_init)
    o = o12[:tq] - lam * o12[tq:]
    o_ref[0] = (_rms(o, g_ref[...]) * (1.0 - lam_init)).astype(o_ref.dtype)


def _attention(cfg, q_arr, q_blk, kc_arr, kc_blk, vc_arr, vc_blk, dl, g_sub, lam_init, tq,
               k_lat=None, v_arr=None, v_blk=0):
    b, lq = q_arr.shape[0], q_arr.shape[1]
    lc = kc_arr.shape[1]
    in_specs = [pl.BlockSpec((1, tq, PAIR), lambda bi, h, i: (bi, i, q_blk + h)),
                pl.BlockSpec((1, lc, PAIR), lambda bi, h, i: (bi, 0, kc_blk + h)),
                pl.BlockSpec((1, lc, PAIR), lambda bi, h, i: (bi, 0, vc_blk + h))]
    args = [q_arr, kc_arr, vc_arr]
    n_lat = 0
    if k_lat is not None:
        s = k_lat.shape[2]
        n_lat = s // cfg.tk
        in_specs += [pl.BlockSpec((1, 1, s, PAIR), lambda bi, h, i: (1, bi, 0, h)),
                     pl.BlockSpec((1, s, PAIR), lambda bi, h, i: (bi, 0, v_blk + h))]
        args += [k_lat, v_arr]
    in_specs += [pl.BlockSpec((4, HEAD_DIM), lambda bi, h, i: (0, 0)),
                 pl.BlockSpec((1, PAIR), lambda bi, h, i: (0, 0)),
                 pl.BlockSpec((1, PAIR), lambda bi, h, i: (0, 0))]
    args += [dl, g_sub, jnp.full((1, PAIR), lam_init, F32)]
    lk = lc + n_lat * cfg.tk
    return pl.pallas_call(
        functools.partial(_attn_kernel, tq=tq, tk=cfg.tk, n_lat=n_lat),
        grid=(b, HEADS, lq // tq),
        in_specs=in_specs,
        out_specs=pl.BlockSpec((1, tq, PAIR), lambda bi, h, i: (bi, i, h)),
        out_shape=jax.ShapeDtypeStruct((b, lq, QK_DIM), BF16),
        scratch_shapes=[pltpu.VMEM((2 * tq, PAIR), BF16), pltpu.VMEM((lk, 2 * PAIR), BF16)],
        compiler_params=_params("parallel", "parallel", "arbitrary"),
        name="diff_attention" if n_lat else "diff_attention_ctx",
    )(*args)


def _dft_cos_sin(n, scale=1.0):
    j = jnp.arange(n, dtype=jnp.int32)
    ang = ((j[:, None] * j[None, :]) % n).astype(F32) * (2.0 * math.pi / n)
    return jnp.cos(ang) * scale, jnp.sin(ang) * scale


def _fft_tables(seq):
    n1, n2 = FFT_N1, seq // FFT_N1
    c1, s1 = _dft_cos_sin(n1)
    stage1 = jnp.concatenate([c1, -s1], axis=0).astype(BF16)
    k1 = jnp.arange(n1, dtype=jnp.int32)
    m = jnp.arange(n2, dtype=jnp.int32)
    ang = (k1[:, None] * m[None, :]).astype(F32) * (2.0 * math.pi / seq)
    tw_cos, tw_sin = jnp.cos(ang), jnp.sin(ang)
    c2, s2 = _dft_cos_sin(n2)
    stage2 = jnp.concatenate([jnp.concatenate([c2, s2], axis=1),
                              jnp.concatenate([-s2, c2], axis=1)], axis=0).astype(BF16)
    cc, sc = _dft_cos_sin(GROUP_DIM, scale=(seq * GROUP_DIM) ** -0.5)
    return stage1, tw_cos, tw_sin, stage2, cc.astype(BF16), sc.astype(BF16)


def _fft_fused_kernel(x_ref, w1_ref, twc_ref, tws_ref, w2_ref, cc_ref, sc_ref, o_ref, xs_ref, bre_ref, bim_ref,
                      *, n2):
    n1 = FFT_N1
    xs_ref[...] = x_ref[0].astype(F32)
    twc = twc_ref[...]
    tws = tws_ref[...]
    for m in range(0, n2, FFT_A_PER_STEP):
        cols = [xs_ref[pl.ds(m + t, n1, stride=n2), :] for t in range(FFT_A_PER_STEP)]
        a = jnp.dot(w1_ref[...], jnp.concatenate(cols, axis=1).astype(BF16), preferred_element_type=F32)
        for t in range(FFT_A_PER_STEP):
            re = a[:n1, t * GROUP_DIM:(t + 1) * GROUP_DIM]
            im = a[n1:, t * GROUP_DIM:(t + 1) * GROUP_DIM]
            c = twc[:, m + t:m + t + 1]
            s = tws[:, m + t:m + t + 1]
            bre_ref[(m + t) * n1:(m + t + 1) * n1, :] = re * c + im * s
            bim_ref[(m + t) * n1:(m + t + 1) * n1, :] = im * c - re * s
    pb = FFT_B_PER_STEP
    for k in range(0, n1, pb):
        z = [jnp.concatenate([bre_ref[pl.ds(k + t, n2, stride=n1), :],
                              bim_ref[pl.ds(k + t, n2, stride=n1), :]], axis=0) for t in range(pb)]
        y = jnp.dot(w2_ref[...], jnp.concatenate(z, axis=1).astype(BF16),
                    preferred_element_type=F32).astype(BF16)
        yre = jnp.concatenate([y[:n2, t * GROUP_DIM:(t + 1) * GROUP_DIM] for t in range(pb)], axis=0)
        yim = jnp.concatenate([y[n2:, t * GROUP_DIM:(t + 1) * GROUP_DIM] for t in range(pb)], axis=0)
        o = (jnp.dot(yre, cc_ref[...], preferred_element_type=F32)
             + jnp.dot(yim, sc_ref[...], preferred_element_type=F32))
        for t in range(pb):
            o_ref[0, pl.ds(k + t, n2, stride=n1), :] = o[t * n2:(t + 1) * n2]


def _fourier_latent_fused(cfg, p_lat3, tables):
    b, s = cfg.batch, cfg.seq
    n1, n2 = FFT_N1, s // FFT_N1
    stage1, tw_cos, tw_sin, stage2, cc, sc = tables
    gblk = cfg.off_f // GROUP_DIM
    full = lambda bi, g: (0, 0)
    out = pl.pallas_call(
        functools.partial(_fft_fused_kernel, n2=n2),
        grid=(b, GROUPS),
        in_specs=[pl.BlockSpec((1, s, GROUP_DIM), lambda bi, g: (bi, 0, gblk + g)),
                  pl.BlockSpec((2 * n1, n1), full), pl.BlockSpec((n1, n2), full), pl.BlockSpec((n1, n2), full),
                  pl.BlockSpec((2 * n2, 2 * n2), full),
                  pl.BlockSpec((GROUP_DIM, GROUP_DIM), full), pl.BlockSpec((GROUP_DIM, GROUP_DIM), full)],
        out_specs=pl.BlockSpec((1, s, GROUP_DIM), lambda bi, g: (bi, 0, g)),
        out_shape=jax.ShapeDtypeStruct((b, s, BRANCH_DIM), F32),
        scratch_shapes=[pltpu.VMEM((s, GROUP_DIM), F32)] * 3,
        compiler_params=_params("parallel", "parallel"), name="fft_fused",
    )(p_lat3, stage1, tw_cos, tw_sin, stage2, cc, sc)
    return out.reshape(b * s, BRANCH_DIM)


def _dft_dense_kernel(x_ref, c_ref, s_ref, cc_ref, sc_ref, o_ref):
    x = x_ref[0]
    yr = jnp.dot(c_ref[...], x, preferred_element_type=F32).astype(BF16)
    yi = (-jnp.dot(s_ref[...], x, preferred_element_type=F32)).astype(BF16)
    for g in range(GROUPS):
        sl = slice(g * GROUP_DIM, (g + 1) * GROUP_DIM)
        o_ref[0, :, sl] = (jnp.dot(yr[:, sl], cc_ref[...], preferred_element_type=F32)
                           + jnp.dot(yi[:, sl], sc_ref[...], preferred_element_type=F32)).astype(o_ref.dtype)


def _fourier_ctx(cfg, p_ctx):
    b, n = cfg.batch, cfg.ctx
    cn, sn = _dft_cos_sin(n)
    cc, sc = _dft_cos_sin(GROUP_DIM, scale=(n * GROUP_DIM) ** -0.5)
    fblk = cfg.off_f // BRANCH_DIM
    full = lambda bi: (0, 0)
    out = pl.pallas_call(
        _dft_dense_kernel,
        grid=(b,),
        in_specs=[pl.BlockSpec((1, n, BRANCH_DIM), lambda bi: (bi, 0, fblk)),
                  pl.BlockSpec((n, n), full), pl.BlockSpec((n, n), full),
                  pl.BlockSpec((GROUP_DIM, GROUP_DIM), full), pl.BlockSpec((GROUP_DIM, GROUP_DIM), full)],
        out_specs=pl.BlockSpec((1, n, BRANCH_DIM), lambda bi: (bi, 0, 0)),
        out_shape=jax.ShapeDtypeStruct((b, n, BRANCH_DIM), BF16),
        compiler_params=_params("parallel"), name="dft_ctx",
    )(p_ctx.reshape(b, n, cfg.n_in), cn.astype(BF16), sn.astype(BF16), cc.astype(BF16), sc.astype(BF16))
    return out.reshape(b * n, BRANCH_DIM)


def _sgu_kernel(u_ref, v_ref, g_ref, w_ref, b_ref, o_ref, *, n_chunks):
    gv = g_ref[...]
    for c in range(n_chunks):
        rows = slice(c * CHUNK, (c + 1) * CHUNK)
        u = jax.nn.gelu(u_ref[rows, :].astype(F32))
        v = jax.nn.gelu(v_ref[rows, :].astype(F32))
        vc = v - jnp.mean(v, axis=-1, keepdims=True)
        vn = (vc * lax.rsqrt(jnp.mean(vc * vc, axis=-1, keepdims=True) + EPS) * gv).astype(BF16)
        for g in range(GROUPS):
            sl = slice(g * GROUP_DIM, (g + 1) * GROUP_DIM)
            sv = jnp.dot(w_ref[g], vn[:, sl], preferred_element_type=F32) + b_ref[:, g:g + 1]
            o_ref[rows, sl] = (u[:, sl] * sv).astype(o_ref.dtype)


def _sgu(cfg, p2, sgu_w, sgu_bt, sgu_g):
    t = p2.shape[0]
    tm = min(cfg.tm_sgu, t)
    ublk, vblk = cfg.off_u // BRANCH_DIM, cfg.off_v // BRANCH_DIM
    return pl.pallas_call(
        functools.partial(_sgu_kernel, n_chunks=tm // CHUNK),
        grid=(t // tm,),
        in_specs=[pl.BlockSpec((tm, BRANCH_DIM), lambda i: (i, ublk)),
                  pl.BlockSpec((tm, BRANCH_DIM), lambda i: (i, vblk)),
                  pl.BlockSpec((1, BRANCH_DIM), lambda i: (0, 0)),
                  pl.BlockSpec((GROUPS, CHUNK, CHUNK), lambda i: (0, 0, 0)),
                  pl.BlockSpec((CHUNK, GROUPS), lambda i: (0, 0))],
        out_specs=pl.BlockSpec((tm, BRANCH_DIM), lambda i: (i, 0)),
        out_shape=jax.ShapeDtypeStruct((t, BRANCH_DIM), BF16),
        compiler_params=_params("parallel"), name="spatial_gating",
    )(p2, p2, sgu_g, sgu_w, sgu_bt)


def _merge_kernel(pg0_ref, pg1_ref, pg2_ref, f_ref, s_ref, a_ref, x_ref, bg_ref,
                  wf_ref, ws_ref, wa_ref, wo_ref, g_ref, mod_ref, *rest, d):
    o_ref = rest[-2] if len(rest) == 3 else rest[0]

    def gate(pg_ref, k):
        return jax.nn.sigmoid(pg_ref[...].astype(F32) + bg_ref[:, k * d:(k + 1) * d])

    merged = gate(pg0_ref, 0) * jnp.dot(f_ref[...].astype(BF16), wf_ref[...], preferred_element_type=F32)
    merged += gate(pg1_ref, 1) * jnp.dot(s_ref[...], ws_ref[...], preferred_element_type=F32)
    merged += gate(pg2_ref, 2) * jnp.dot(a_ref[...], wa_ref[...], preferred_element_type=F32)
    y = jnp.dot(merged.astype(BF16), wo_ref[...], preferred_element_type=F32)
    m = mod_ref[0]
    x_new = x_ref[...] + m[2:3] * _rms(y, g_ref[...])
    o_ref[...] = x_new
    if len(rest) == 3:
        g2_ref, _, h_ref = rest
        h_ref[...] = (_rms(x_new, g2_ref[...]) * (1.0 + m[4:5]) + m[3:4]).astype(h_ref.dtype)


def _resident(shape):
    return pl.BlockSpec(shape, lambda *_: (0,) * len(shape), pipeline_mode=pl.Buffered(1))


def _merge(cfg, p2, four, sgu, attn, x, b_gate, wf, ws, wa, wo, g, mod, rows_per_mod, g_ffn=None):
    t, d = x.shape
    tm = cfg.tm_merge
    row = lambda i: (i, 0)
    in_specs = [pl.BlockSpec((tm, d), lambda i: (i, 0)),
                pl.BlockSpec((tm, d), lambda i: (i, 1)),
                pl.BlockSpec((tm, d), lambda i: (i, 2)),
                pl.BlockSpec((tm, BRANCH_DIM), row), pl.BlockSpec((tm, BRANCH_DIM), row),
                pl.BlockSpec((tm, QK_DIM), row), pl.BlockSpec((tm, d), row),
                _resident((1, 3 * d)), _resident((BRANCH_DIM, d)), _resident((BRANCH_DIM, d)),
                _resident((QK_DIM, d)), _resident((d, d)), _resident((1, d)),
                pl.BlockSpec((1, 6, d), lambda i: (i * tm // rows_per_mod, 0, 0))]
    args = [p2, p2, p2, four, sgu, attn, x, b_gate, wf, ws, wa, wo, g, mod]
    out_specs = pl.BlockSpec((tm, d), row)
    out_shape = jax.ShapeDtypeStruct((t, d), F32)
    if g_ffn is not None:
        in_specs.append(_resident((1, d)))
        args.append(g_ffn)
        out_specs = [out_specs, pl.BlockSpec((tm, d), row)]
        out_shape = [out_shape, jax.ShapeDtypeStruct((t, d), BF16)]
    return pl.pallas_call(
        functools.partial(_merge_kernel, d=d),
        grid=(t // tm,), in_specs=in_specs, out_specs=out_specs, out_shape=out_shape,
        compiler_params=_params("parallel"), name="merge",
    )(*args)


def _last_active(i, n_active_ref):
    return jnp.minimum(i, jnp.maximum(n_active_ref[0] - 1, 0))


def _ffn_up_kernel(eid_ref, nact_ref, a_ref, w1_ref, w3_ref, o_ref, w1b_ref, w3b_ref):
    i = pl.program_id(1)
    active = i < nact_ref[0]
    fresh = jnp.logical_or(i == 0, eid_ref[i] != eid_ref[jnp.maximum(i - 1, 0)])

    @pl.when(jnp.logical_and(active, fresh))
    def _():
        w1b_ref[...] = w1_ref[0].astype(BF16)
        w3b_ref[...] = w3_ref[0].astype(BF16)

    @pl.when(active)
    def _():
        a = a_ref[...]
        h1 = jnp.dot(a, w1b_ref[...], preferred_element_type=F32)
        h3 = jnp.dot(a, w3b_ref[...], preferred_element_type=F32)
        o_ref[...] = (jax.nn.silu(h1) * h3).astype(o_ref.dtype)

    @pl.when(jnp.logical_not(active))
    def _():
        o_ref[...] = jnp.zeros(o_ref.shape, o_ref.dtype)


def _ffn_up(cfg, a, w1, w3, tile_eid, n_active, tm):
    r, d = a.shape
    dff = w1.shape[2]
    tn = cfg.tn_up
    grid_spec = pltpu.PrefetchScalarGridSpec(
        num_scalar_prefetch=2,
        grid=(dff // tn, r // tm),
        in_specs=[pl.BlockSpec((tm, d), lambda j, i, eid, na: (_last_active(i, na), 0)),
                  pl.BlockSpec((1, d, tn), lambda j, i, eid, na: (eid[i], 0, j)),
                  pl.BlockSpec((1, d, tn), lambda j, i, eid, na: (eid[i], 0, j))],
        out_specs=pl.BlockSpec((tm, tn), lambda j, i, eid, na: (i, j)),
        scratch_shapes=[pltpu.VMEM((d, tn), BF16), pltpu.VMEM((d, tn), BF16)],
    )
    return pl.pallas_call(
        _ffn_up_kernel, grid_spec=grid_spec,
        out_shape=jax.ShapeDtypeStruct((r, dff), BF16),
        compiler_params=_params("arbitrary", "arbitrary"), name="swiglu_up",
    )(tile_eid, n_active, a, w1, w3)


def _ffn_down_kernel(eid_ref, nact_ref, *refs, residual):
    if residual:
        h_ref, w_ref, x_ref, g_ref, mod_ref, o_ref, acc_ref = refs
    else:
        h_ref, w_ref, o_ref, acc_ref = refs
    k = pl.program_id(1)

    @pl.when(k == 0)
    def _():
        acc_ref[...] = jnp.zeros(acc_ref.shape, F32)

    @pl.when(pl.program_id(0) < nact_ref[0])
    def _():
        acc_ref[...] += jnp.dot(h_ref[...], w_ref[0], preferred_element_type=F32)

    @pl.when(k == pl.num_programs(1) - 1)
    def _():
        if residual:
            o_ref[...] = x_ref[...] + mod_ref[0][5:6] * _rms(acc_ref[...], g_ref[...])
        else:
            o_ref[...] = acc_ref[...].astype(o_ref.dtype)


def _ffn_down(cfg, h, w2, tile_eid, n_active, tm, tk, x=None, g=None, mod=None, rows_per_mod=None):
    r, dff = h.shape
    d = w2.shape[2]
    residual = x is not None
    in_specs = [pl.BlockSpec((tm, tk), lambda i, k, eid, na: (_last_active(i, na), k)),
                pl.BlockSpec((1, tk, d), lambda i, k, eid, na: (eid[i], k, 0))]
    args = [h, w2]
    if residual:
        in_specs += [pl.BlockSpec((tm, d), lambda i, k, eid, na: (i, 0)),
                     pl.BlockSpec((1, d), lambda i, k, eid, na: (0, 0)),
                     pl.BlockSpec((1, 6, d), lambda i, k, eid, na: (i * tm // rows_per_mod, 0, 0))]
        args += [x, g, mod]
    grid_spec = pltpu.PrefetchScalarGridSpec(
        num_scalar_prefetch=2,
        grid=(r // tm, dff // tk),
        in_specs=in_specs,
        out_specs=pl.BlockSpec((tm, d), lambda i, k, eid, na: (i, 0)),
        scratch_shapes=[pltpu.VMEM((tm, d), F32)],
    )
    return pl.pallas_call(
        functools.partial(_ffn_down_kernel, residual=residual), grid_spec=grid_spec,
        out_shape=jax.ShapeDtypeStruct((r, d), F32 if residual else BF16),
        compiler_params=_params("parallel", "arbitrary"),
        name="swiglu_down_residual" if residual else "swiglu_down",
    )(tile_eid, n_active, *args)


def _ffn_down_cols_kernel(eid_ref, nact_ref, h_ref, w_ref, *refs):
    o_ref, wb_ref = refs[-2:]
    i = pl.program_id(1)
    active = i < nact_ref[0]
    fresh = jnp.logical_or(i == 0, eid_ref[i] != eid_ref[jnp.maximum(i - 1, 0)])

    @pl.when(jnp.logical_and(active, fresh))
    def _():
        wb_ref[...] = w_ref[0].astype(BF16)

    @pl.when(active)
    def _():
        o_ref[...] = jnp.dot(h_ref[...], wb_ref[...], preferred_element_type=F32).astype(o_ref.dtype)

    @pl.when(jnp.logical_not(active))
    def _():
        o_ref[...] = jnp.zeros(o_ref.shape, o_ref.dtype)


def _ffn_down_cols(cfg, h, w2, tile_eid, n_active, tm, y_prev=None, tile_off=0, total_rows=None):
    r, dff = h.shape
    d = w2.shape[2]
    tn = cfg.tn_down_cols
    in_specs = [pl.BlockSpec((tm, dff), lambda n, i, eid, na: (_last_active(i, na), 0)),
                pl.BlockSpec((1, dff, tn), lambda n, i, eid, na: (eid[i], 0, n))]
    args = [tile_eid, n_active, h, w2]
    aliases = {}
    if y_prev is not None:
        in_specs.append(pl.BlockSpec(memory_space=pl.ANY))
        args.append(y_prev)
        aliases = {len(args) - 1: 0}
    grid_spec = pltpu.PrefetchScalarGridSpec(
        num_scalar_prefetch=2,
        grid=(d // tn, r // tm),
        in_specs=in_specs,
        out_specs=pl.BlockSpec((tm, tn), lambda n, i, eid, na: (i + tile_off, n)),
        scratch_shapes=[pltpu.VMEM((dff, tn), BF16)],
    )
    return pl.pallas_call(
        _ffn_down_cols_kernel, grid_spec=grid_spec,
        out_shape=jax.ShapeDtypeStruct((total_rows or r, d), BF16),
        input_output_aliases=aliases,
        compiler_params=_params("arbitrary", "arbitrary"), name="swiglu_down_cols",
    )(*args)


def _combine_kernel(y1_ref, y2_ref, wt_ref, x_ref, g_ref, mod_ref, o_ref):
    wt = wt_ref[...]
    y = wt[:, 0:1] * y1_ref[...].astype(F32) + wt[:, 1:2] * y2_ref[...].astype(F32)
    o_ref[...] = x_ref[...] + mod_ref[0][5:6] * _rms(y, g_ref[...])


def _combine(cfg, y1, y2, wt, x, g, mod, rows_per_mod):
    t, d = x.shape
    tm = min(cfg.tm_norm, t)
    row = lambda i: (i, 0)
    return pl.pallas_call(
        _combine_kernel,
        grid=(t // tm,),
        in_specs=[pl.BlockSpec((tm, d), row), pl.BlockSpec((tm, d), row),
                  pl.BlockSpec((tm, TOP_K), row), pl.BlockSpec((tm, d), row),
                  pl.BlockSpec((1, d), lambda i: (0, 0)),
                  pl.BlockSpec((1, 6, d), lambda i: (i * tm // rows_per_mod, 0, 0))],
        out_specs=pl.BlockSpec((tm, d), row),
        out_shape=jax.ShapeDtypeStruct((t, d), F32),
        compiler_params=_params("parallel"), name="moe_combine",
    )(y1, y2, wt, x, g, mod)


def _dispatch_plan(route, counts, tm):
    t = route.shape[1]
    n_pairs = TOP_K * t
    n_rows = n_pairs + N_EXPERTS * tm
    e_flat = route[:TOP_K].reshape(n_pairs)
    rank = route[TOP_K:2 * TOP_K].reshape(n_pairs)
    padded = (counts + tm - 1) // tm * tm
    ends = jnp.cumsum(padded)
    starts = ends - padded
    onehot = e_flat[:, None] == jnp.arange(N_EXPERTS, dtype=jnp.int32)[None, :]
    dest = jnp.sum(jnp.where(onehot, starts[None, :], 0), axis=1) + rank
    tok = jnp.tile(jnp.arange(t, dtype=jnp.int32), TOP_K)
    row_tok = jnp.zeros((n_rows,), jnp.int32).at[dest].set(tok, mode="promise_in_bounds", unique_indices=True)
    tile_start = jnp.arange(n_rows // tm, dtype=jnp.int32) * tm
    tile_eid = jnp.minimum(jnp.sum((tile_start[:, None] >= ends[None, :]).astype(jnp.int32), axis=1),
                           N_EXPERTS - 1)
    n_active = (ends[-1] // tm).astype(jnp.int32).reshape(1)
    return row_tok, dest.reshape(TOP_K, t), tile_eid, n_active


def _take_rows(a, idx):
    return a.at[idx].get(mode="promise_in_bounds")


def _dense_plan(t, tm):
    return jnp.zeros((t // tm,), jnp.int32), jnp.full((1,), t // tm, jnp.int32)


def _permute_w_in(w):
    b3 = 3 * BRANCH_DIM
    q3 = 3 * QK_DIM
    return jnp.concatenate([w[:, b3 + q3:], w[:, b3:b3 + q3], w[:, :b3]], axis=1).astype(BF16)


def _forward(cfg, x, c, ctx, c_ctx, w_mod, b_mod, g_norm, w_in, b_gate, w_fourier_out, w_sgu_out,
             w_attn_out, w_o, sgu_w, sgu_b, sgu_g, diff_lambda, diff_subln_g,
             ffn_w1, ffn_w3, ffn_w2, router_w, moe_w1, moe_w3, moe_w2):
    b, s, d = x.shape
    n_ctx = ctx.shape[1]
    t_lat, t_ctx = b * s, b * n_ctx
    cos, sin = _rope_tables(s)
    fft_tables = _fft_tables(s)

    c_rows = jnp.concatenate([c, c_ctx[None, :], jnp.zeros((8 - b - 1, d), F32)], axis=0)
    mod_all = _modulation(cfg, c_rows, w_mod, b_mod)

    xl = x.reshape(t_lat, d)
    xc = ctx.reshape(t_ctx, d)
    for l in range(cfg.depth):
        last = l == cfg.depth - 1
        lam_init = 0.8 - 0.6 * math.exp(-0.3 * l)
        mod_l = mod_all[l, :b].reshape(b, 6, d)
        mod_c = mod_all[l, b:b + 1].reshape(1, 6, d)
        g = g_norm[l].reshape(4, 1, d)
        w_in_l = _permute_w_in(w_in[l])
        wf, ws = w_fourier_out[l].astype(BF16), w_sgu_out[l].astype(BF16)
        wa, wo = w_attn_out[l].astype(BF16), w_o[l].astype(BF16)
        bg = b_gate[l].reshape(1, 3 * d)
        sw = sgu_w[l].astype(BF16)
        sbt = sgu_b[l].T
        sg = sgu_g[l].reshape(1, BRANCH_DIM)
        dl = diff_lambda[l]
        gsub = diff_subln_g[l].reshape(1, PAIR)
        pair_blk = lambda off: off // PAIR

        p_lat = _in_proj(cfg, xl, g[0], mod_l, s, w_in_l)
        p_lat3 = p_lat.reshape(b, s, cfg.n_in)
        qk_rot = _rope(cfg, p_lat3, cos, sin)
        if last:
            p_ctx3 = _in_proj(cfg, xc, g[0], mod_c, t_ctx,
                              w_in_l[:, cfg.off_k:cfg.off_f]).reshape(b, n_ctx, 2 * QK_DIM)
            kc_blk, vc_blk = 0, pair_blk(QK_DIM)
        else:
            p_ctx = _in_proj(cfg, xc, g[0], mod_c, t_ctx, w_in_l)
            p_ctx3 = p_ctx.reshape(b, n_ctx, cfg.n_in)
            kc_blk, vc_blk = pair_blk(cfg.off_k), pair_blk(cfg.off_va)
        al = _attention(cfg, qk_rot[0], 0, p_ctx3, kc_blk, p_ctx3, vc_blk, dl, gsub, lam_init, cfg.tq,
                        k_lat=qk_rot, v_arr=p_lat3, v_blk=pair_blk(cfg.off_va))
        four_l = _fourier_latent_fused(cfg, p_lat3, fft_tables)
        sgu_l = _sgu(cfg, p_lat, sw, sbt, sg)
        dense = l % 2 == 0
        g_ffn = g[2] if dense else None
        xl = _merge(cfg, p_lat, four_l, sgu_l, al.reshape(t_lat, QK_DIM), xl, bg, wf, ws, wa, wo, g[1], mod_l, s,
                    g_ffn=g_ffn)
        fl_l = fl_c = None
        if dense:
            xl, fl_l = xl
        if not last:
            ac = _attention(cfg, p_ctx3, pair_blk(cfg.off_q), p_ctx3, kc_blk, p_ctx3, vc_blk, dl, gsub,
                            lam_init, n_ctx)
            four_c = _fourier_ctx(cfg, p_ctx)
            sgu_c = _sgu(cfg, p_ctx, sw, sbt, sg)
            xc = _merge(cfg, p_ctx, four_c, sgu_c, ac.reshape(t_ctx, QK_DIM), xc, bg, wf, ws, wa, wo, g[1],
                        mod_c, t_ctx, g_ffn=g_ffn)
            if dense:
                xc, fl_c = xc

        i = l // 2
        streams = [(xl, mod_l, s, fl_l)] + ([] if last else [(xc, mod_c, t_ctx, fl_c)])
        outs = []
        if dense:
            w2 = ffn_w2[i][None].astype(BF16)
            for xs, mod, rpm, fl in streams:
                t = xs.shape[0]
                tm_up = min(cfg.tm_up, t)
                eid_up, n_up = _dense_plan(t, tm_up)
                h = _ffn_up(cfg, fl, ffn_w1, ffn_w3, eid_up + i, n_up, tm_up)
                tm_dn = min(cfg.tm_down_dense, t)
                outs.append(_ffn_down(cfg, h, w2, *_dense_plan(t, tm_dn), tm_dn, cfg.tk_down_dense, x=xs,
                                      g=g[3], mod=mod, rows_per_mod=rpm))
        else:
            w1 = moe_w1.reshape(-1, d, cfg.dff)
            w3 = moe_w3.reshape(-1, d, cfg.dff)
            w2 = moe_w2.reshape(-1, cfg.dff, d)
            rwt = router_w[i].T
            for xs, mod, rpm, _ in streams:
                tm = cfg.tm_down if xs.shape[0] >= N_EXPERTS * cfg.tm_down else cfg.tm_down_ctx
                fl, route, top_w, counts = _norm_mod_router(cfg, xs, g[2], mod, rpm, 3, 4, rwt)
                row_tok, pos, tile_eid, n_active = _dispatch_plan(
                    route, counts[:N_EXPERTS, 0].astype(jnp.int32), tm)
                n_tiles = tile_eid.shape[0]
                n_chunks = cfg.moe_chunks if tm == cfg.tm_down and n_tiles % cfg.moe_chunks == 0 else 1
                tpc = n_tiles // n_chunks
                tm_dn = min(cfg.tm_down_cols, tm)
                rep = tm // tm_dn
                eid = tile_eid + i * N_EXPERTS
                y = None
                for c in range(n_chunks):
                    na_c = jnp.clip(n_active - c * tpc, 0, tpc)
                    eid_c = eid[c * tpc:(c + 1) * tpc]
                    a = _take_rows(fl, row_tok[c * tpc * tm:(c + 1) * tpc * tm])
                    h = _ffn_up(cfg, a, w1, w3, eid_c, na_c, tm)
                    y = _ffn_down_cols(cfg, h, w2, jnp.repeat(eid_c, rep), na_c * rep, tm_dn, y_prev=y,
                                       tile_off=c * tpc * rep, total_rows=n_tiles * tm)
                outs.append(_combine(cfg, _take_rows(y, pos[0]), _take_rows(y, pos[1]),
                                     top_w[:TOP_K].T, xs, g[3], mod, rpm))
        xl = outs[0]
        if not last:
            xc = outs[1]
    return xl.reshape(b, s, d)


def kernel(x, c, ctx, c_ctx, w_mod, b_mod, g_norm, w_in, b_gate, w_fourier_out, w_sgu_out, w_attn_out, w_o,
           sgu_w, sgu_b, sgu_g, diff_lambda, diff_subln_g, ffn_w1, ffn_w3, ffn_w2, router_w,
           moe_w1, moe_w3, moe_w2):
    return _forward(PROD, x, c, ctx, c_ctx, w_mod, b_mod, g_norm, w_in, b_gate, w_fourier_out, w_sgu_out,
                    w_attn_out, w_o, sgu_w, sgu_b, sgu_g, diff_lambda, diff_subln_g,
                    ffn_w1, ffn_w3, ffn_w2, router_w, moe_w1, moe_w3, moe_w2)
```

```python
import functools
import math
from typing import NamedTuple

import jax
import jax.numpy as jnp
from jax import lax
from jax.experimental import pallas as pl
from jax.experimental.pallas import tpu as pltpu

F32 = jnp.float32
BF16 = jnp.bfloat16

EPS = 1e-6
GRID_W = 64
ROPE_BASE = 10000.0
HEAD_DIM = 64
HEADS = 8
PAIR = 2 * HEAD_DIM
QK_DIM = HEADS * PAIR
GROUP_DIM = 128
GROUPS = 4
BRANCH_DIM = GROUPS * GROUP_DIM
CHUNK = 128
FFT_N1 = 128
FFT_A_PER_STEP = 4
FFT_B_PER_STEP = 8
N_EXPERTS = 8
TOP_K = 2
EXPERT_ROWS = 8
ATTN_SCALE = HEAD_DIM ** -0.5
Q_SCALE = ATTN_SCALE * math.log2(math.e)
VMEM_LIMIT = 56 * 1024 * 1024


class Cfg(NamedTuple):
    d: int
    batch: int
    seq: int
    ctx: int
    dff: int
    depth: int
    tm_norm: int
    tm_mm: int
    tn_mm: int
    tm_rope: int
    tq: int
    tk: int
    tm_sgu: int
    tm_merge: int
    tm_up: int
    tn_up: int
    tm_down: int
    tm_down_cols: int
    tn_down_cols: int
    moe_chunks: int
    tm_down_dense: int
    tk_down_dense: int
    tn_mod: int

    @property
    def n_in(self):
        return 3 * self.d + 3 * QK_DIM + 3 * BRANCH_DIM

    @property
    def off_q(self):
        return 3 * self.d

    @property
    def off_k(self):
        return self.off_q + QK_DIM

    @property
    def off_va(self):
        return self.off_k + QK_DIM

    @property
    def off_f(self):
        return self.off_va + QK_DIM

    @property
    def off_u(self):
        return self.off_f + BRANCH_DIM

    @property
    def off_v(self):
        return self.off_u + BRANCH_DIM


PROD = Cfg(d=2048, batch=4, seq=8192, ctx=256, dff=5632, depth=4,
           tm_norm=512, tm_mm=1024, tn_mm=1536, tm_rope=512, tq=1024, tk=256,
           tm_sgu=1024, tm_merge=256, tm_up=1024, tn_up=512, tm_down=1024, tm_down_cols=512, tn_down_cols=512,
           moe_chunks=2, tm_down_dense=1024, tk_down_dense=512, tn_mod=1024)


def _params(*sem):
    return pltpu.CompilerParams(dimension_semantics=sem, vmem_limit_bytes=VMEM_LIMIT)


def _rms(y, g):
    return y * lax.rsqrt(jnp.mean(y * y, axis=-1, keepdims=True) + EPS) * g


def _mod_kernel(c_ref, w_ref, b_ref, o_ref):
    sc = jax.nn.silu(c_ref[...])
    o_ref[0] = jnp.dot(sc.astype(BF16), w_ref[0].astype(BF16), preferred_element_type=F32) + b_ref[0]


def _modulation(cfg, c_rows, w_mod, b_mod):
    depth, d, n6 = w_mod.shape
    tn = cfg.tn_mod
    return pl.pallas_call(
        _mod_kernel,
        grid=(depth, n6 // tn),
        in_specs=[pl.BlockSpec((8, d), lambda l, j: (0, 0)),
                  pl.BlockSpec((1, d, tn), lambda l, j: (l, 0, j)),
                  pl.BlockSpec((1, 1, tn), lambda l, j: (l, 0, j))],
        out_specs=pl.BlockSpec((1, 8, tn), lambda l, j: (l, 0, j)),
        out_shape=jax.ShapeDtypeStruct((depth, 8, n6), F32),
        compiler_params=_params("parallel", "parallel"),
        name="modulation",
    )(c_rows, w_mod, b_mod.reshape(depth, 1, n6))


def _norm_mod_router_kernel(x_ref, g_ref, mod_ref, rw_ref, tri_ref, o_ref, idx_ref, wt_ref, cnt_ref, *,
                            shift_idx, scale_idx):
    @pl.when(pl.program_id(0) == 0)
    def _():
        cnt_ref[...] = jnp.zeros(cnt_ref.shape, F32)

    y = _rms(x_ref[...], g_ref[...])
    m = mod_ref[0]
    h = y * (1.0 + m[scale_idx:scale_idx + 1]) + m[shift_idx:shift_idx + 1]
    o_ref[...] = h.astype(o_ref.dtype)
    logits = lax.dot_general(rw_ref[...], h, (((1,), (1,)), ((), ())),
                             precision=lax.Precision.HIGHEST, preferred_element_type=F32)
    row = lax.broadcasted_iota(jnp.int32, logits.shape, 0).astype(F32)
    m1 = jnp.max(logits, axis=0, keepdims=True)
    i1 = jnp.min(jnp.where(logits == m1, row, float(N_EXPERTS)), axis=0, keepdims=True)
    rest = jnp.where(row == i1, -jnp.inf, logits)
    m2 = jnp.max(rest, axis=0, keepdims=True)
    i2 = jnp.min(jnp.where(rest == m2, row, float(N_EXPERTS)), axis=0, keepdims=True)
    e2 = jnp.exp(m2 - m1)
    w1 = 1.0 / (1.0 + e2)
    w2 = e2 / (1.0 + e2)
    wt_ref[...] = jnp.where(row == 0.0, w1, jnp.where(row == 1.0, w2, 0.0))
    pick1 = jnp.where(row == i1, 1.0, 0.0)
    pick2 = jnp.where(row == i2, 1.0, 0.0)
    pre1 = jnp.dot(pick1.astype(BF16), tri_ref[...], preferred_element_type=F32)
    pre2 = jnp.dot(pick2.astype(BF16), tri_ref[...], preferred_element_type=F32)
    n1 = jnp.sum(pick1, axis=1, keepdims=True)
    n2 = jnp.sum(pick2, axis=1, keepdims=True)
    seen = cnt_ref[:, 0:1]
    r1 = jnp.sum(pick1 * (pre1 - 1.0 + seen), axis=0, keepdims=True)
    r2 = jnp.sum(pick2 * (pre2 - 1.0 + seen + n1), axis=0, keepdims=True)
    cnt_ref[...] = cnt_ref[...] + (n1 + n2)
    idx_ref[...] = jnp.where(row == 0.0, i1, jnp.where(row == 1.0, i2, jnp.where(row == 2.0, r1, jnp.where(
        row == 3.0, r2, 0.0)))).astype(jnp.int32)


def _norm_mod_router(cfg, x, g, mod, rows_per_mod, shift_idx, scale_idx, router_wt):
    t, d = x.shape
    tm = min(cfg.tm_norm, t)
    in_specs = [pl.BlockSpec((tm, d), lambda i: (i, 0)),
                pl.BlockSpec((1, d), lambda i: (0, 0)),
                pl.BlockSpec((1, 6, d), lambda i: (i * tm // rows_per_mod, 0, 0))]
    h_spec = pl.BlockSpec((tm, d), lambda i: (i, 0))
    h_shape = jax.ShapeDtypeStruct((t, d), BF16)
    r_spec = pl.BlockSpec((EXPERT_ROWS, tm), lambda i: (0, i))
    tri = jnp.triu(jnp.ones((tm, tm), BF16))
    return pl.pallas_call(
        functools.partial(_norm_mod_router_kernel, shift_idx=shift_idx, scale_idx=scale_idx),
        grid=(t // tm,),
        in_specs=in_specs + [pl.BlockSpec((N_EXPERTS, d), lambda i: (0, 0)),
                             pl.BlockSpec((tm, tm), lambda i: (0, 0))],
        out_specs=[h_spec, r_spec, r_spec, pl.BlockSpec((EXPERT_ROWS, 128), lambda i: (0, 0))],
        out_shape=[h_shape, jax.ShapeDtypeStruct((EXPERT_ROWS, t), jnp.int32),
                   jax.ShapeDtypeStruct((EXPERT_ROWS, t), F32),
                   jax.ShapeDtypeStruct((EXPERT_ROWS, 128), F32)],
        compiler_params=_params("arbitrary"), name="norm_mod_router",
    )(x, g, mod, router_wt, tri)


def _in_proj_kernel(x_ref, g_ref, mod_ref, w_ref, o_ref, h_ref):
    @pl.when(pl.program_id(1) == 0)
    def _():
        m = mod_ref[0]
        h_ref[...] = (_rms(x_ref[...], g_ref[...]) * (1.0 + m[1:2]) + m[0:1]).astype(h_ref.dtype)

    o_ref[...] = jnp.dot(h_ref[...], w_ref[...], preferred_element_type=F32).astype(o_ref.dtype)


def _in_proj(cfg, x, g, mod, rows_per_mod, w):
    t, d = x.shape
    n = w.shape[1]
    tm = min(cfg.tm_mm, t)
    tn = cfg.tn_mm if n % cfg.tn_mm == 0 else 512
    return pl.pallas_call(
        _in_proj_kernel,
        grid=(t // tm, n // tn),
        in_specs=[pl.BlockSpec((tm, d), lambda i, j: (i, 0)),
                  pl.BlockSpec((1, d), lambda i, j: (0, 0)),
                  pl.BlockSpec((1, 6, d), lambda i, j: (i * tm // rows_per_mod, 0, 0)),
                  pl.BlockSpec((d, tn), lambda i, j: (0, j))],
        out_specs=pl.BlockSpec((tm, tn), lambda i, j: (i, j)),
        out_shape=jax.ShapeDtypeStruct((t, n), BF16),
        scratch_shapes=[pltpu.VMEM((tm, d), BF16)],
        compiler_params=_params("parallel", "arbitrary"), name="in_proj",
    )(x, g, mod, w)


def _rope_tables(seq):
    rows = seq // GRID_W
    r = jnp.repeat(jnp.arange(rows, dtype=F32), GRID_W)
    col = jnp.tile(jnp.arange(GRID_W, dtype=F32), rows)
    n_freq = HEAD_DIM // 4
    inv = ROPE_BASE ** (-jnp.arange(n_freq, dtype=F32) / n_freq)
    ang = jnp.concatenate([r[:, None] * inv, col[:, None] * inv], axis=-1)
    cos, sin = jnp.cos(ang), jnp.sin(ang)
    return jnp.tile(cos, (1, 4)), jnp.tile(jnp.concatenate([-sin, sin], axis=-1), (1, 2))


def _rotate_pair_block(t, cos, sin):
    lane = lax.broadcasted_iota(jnp.int32, t.shape, 1)
    first_half = (lane % HEAD_DIM) < (HEAD_DIM // 2)
    partner = jnp.where(first_half, pltpu.roll(t, PAIR - HEAD_DIM // 2, 1), pltpu.roll(t, HEAD_DIM // 2, 1))
    return t * cos + partner * sin


def _rope_kernel(p_ref, cos_ref, sin_ref, o_ref):
    scale = jnp.where(pl.program_id(2) == 0, Q_SCALE, 1.0).astype(F32)
    cos = cos_ref[...] * scale
    sin = sin_ref[...] * scale
    for c in range(QK_DIM // PAIR):
        t = p_ref[0, :, c * PAIR:(c + 1) * PAIR].astype(F32)
        o_ref[0, 0, :, c * PAIR:(c + 1) * PAIR] = _rotate_pair_block(t, cos, sin).astype(o_ref.dtype)


def _rope(cfg, p3, cos, sin):
    b, s, _ = p3.shape
    tm = cfg.tm_rope
    qblk = cfg.off_q // QK_DIM
    return pl.pallas_call(
        _rope_kernel,
        grid=(b, s // tm, 2),
        in_specs=[pl.BlockSpec((1, tm, QK_DIM), lambda bi, i, j: (bi, i, qblk + j)),
                  pl.BlockSpec((tm, PAIR), lambda bi, i, j: (i, 0)),
                  pl.BlockSpec((tm, PAIR), lambda bi, i, j: (i, 0))],
        out_specs=pl.BlockSpec((1, 1, tm, QK_DIM), lambda bi, i, j: (j, bi, i, 0)),
        out_shape=jax.ShapeDtypeStruct((2, b, s, QK_DIM), BF16),
        compiler_params=_params("parallel", "parallel", "parallel"), name="rope",
    )(p3, cos, sin)


def _attn_kernel(*refs, tq, tk, n_lat):
    if n_lat:
        q_ref, kc_ref, vc_ref, k_ref, v_ref, dl_ref, g_ref, li_ref, o_ref, q2_ref, vx_ref = refs
    else:
        q_ref, kc_ref, vc_ref, dl_ref, g_ref, li_ref, o_ref, q2_ref, vx_ref = refs
    lc = kc_ref.shape[1]

    @pl.when(pl.program_id(2) == 0)
    def _():
        vx_ref[:lc, :PAIR] = vc_ref[0]
        vx_ref[:lc, PAIR:] = jnp.ones((lc, PAIR), BF16)
        if n_lat:
            vx_ref[lc:, :PAIR] = v_ref[0]
            vx_ref[lc:, PAIR:] = jnp.ones((n_lat * tk, PAIR), BF16)

    q = q_ref[0]
    if not n_lat:
        q = (q.astype(F32) * Q_SCALE).astype(BF16)
    lane = lax.broadcasted_iota(jnp.int32, q.shape, 1)
    zero = jnp.zeros_like(q)
    q2_ref[:tq] = jnp.where(lane < HEAD_DIM, q, zero)
    q2_ref[tq:] = jnp.where(lane >= HEAD_DIM, q, zero)
    q2 = q2_ref[...]

    chunks = [(kc_ref[0], 0, lc)]
    for j in range(n_lat):
        chunks.append((k_ref[0, 0, j * tk:(j + 1) * tk, :], lc + j * tk, tk))
    m = jnp.full((2 * tq, 1), -jnp.inf, F32)
    acc = jnp.zeros((2 * tq, 2 * PAIR), F32)
    for k, off, n in chunks:
        s = lax.dot_general(q2, k, (((1,), (1,)), ((), ())), preferred_element_type=F32)
        m_new = jnp.maximum(m, jnp.max(s, axis=-1, keepdims=True))
        p = jnp.exp2(s - m_new).astype(BF16)
        acc = jnp.exp2(m - m_new) * acc + jnp.dot(p, vx_ref[off:off + n, :], preferred_element_type=F32)
        m = m_new

    o12 = acc[:, :PAIR] / acc[:, PAIR:]
    dl = dl_ref[...]
    lam_init = li_ref[...]
    lam = (jnp.exp(jnp.sum(dl[0:1] * dl[1:2], axis=-1, keepdims=True))
           - jnp.exp(jnp.sum(dl[2:3] * dl[3:4], axis=-1, keepdims=True)) + lam_init)
    o = o12[:tq] - lam * o12[tq:]
    o_ref[0] = (_rms(o, g_ref[...]) * (1.0 - lam_init)).astype(o_ref.dtype)


def _attention(cfg, q_arr, q_blk, kc_arr, kc_blk, vc_arr, vc_blk, dl, g_sub, lam_init, tq,
               k_lat=None, v_arr=None, v_blk=0):
    b, lq = q_arr.shape[0], q_arr.shape[1]
    lc = kc_arr.shape[1]
    in_specs = [pl.BlockSpec((1, tq, PAIR), lambda bi, h, i: (bi, i, q_blk + h)),
                pl.BlockSpec((1, lc, PAIR), lambda bi, h, i: (bi, 0, kc_blk + h)),
                pl.BlockSpec((1, lc, PAIR), lambda bi, h, i: (bi, 0, vc_blk + h))]
    args = [q_arr, kc_arr, vc_arr]
    n_lat = 0
    if k_lat is not None:
        s = k_lat.shape[2]
        n_lat = s // cfg.tk
        in_specs += [pl.BlockSpec((1, 1, s, PAIR), lambda bi, h, i: (1, bi, 0, h)),
                     pl.BlockSpec((1, s, PAIR), lambda bi, h, i: (bi, 0, v_blk + h))]
        args += [k_lat, v_arr]
    in_specs += [pl.BlockSpec((4, HEAD_DIM), lambda bi, h, i: (0, 0)),
                 pl.BlockSpec((1, PAIR), lambda bi, h, i: (0, 0)),
                 pl.BlockSpec((1, PAIR), lambda bi, h, i: (0, 0))]
    args += [dl, g_sub, jnp.full((1, PAIR), lam_init, F32)]
    lk = lc + n_lat * cfg.tk
    return pl.pallas_call(
        functools.partial(_attn_kernel, tq=tq, tk=cfg.tk, n_lat=n_lat),
        grid=(b, HEADS, lq // tq),
        in_specs=in_specs,
        out_specs=pl.BlockSpec((1, tq, PAIR), lambda bi, h, i: (bi, i, h)),
        out_shape=jax.ShapeDtypeStruct((b, lq, QK_DIM), BF16),
        scratch_shapes=[pltpu.VMEM((2 * tq, PAIR), BF16), pltpu.VMEM((lk, 2 * PAIR), BF16)],
        compiler_params=_params("parallel", "parallel", "arbitrary"),
        name="diff_attention" if n_lat else "diff_attention_ctx",
    )(*args)


def _dft_cos_sin(n, scale=1.0):
    j = jnp.arange(n, dtype=jnp.int32)
    ang = ((j[:, None] * j[None, :]) % n).astype(F32) * (2.0 * math.pi / n)
    return jnp.cos(ang) * scale, jnp.sin(ang) * scale


def _fft_tables(seq):
    n1, n2 = FFT_N1, seq // FFT_N1
    c1, s1 = _dft_cos_sin(n1)
    stage1 = jnp.concatenate([c1, -s1], axis=0).astype(BF16)
    k1 = jnp.arange(n1, dtype=jnp.int32)
    m = jnp.arange(n2, dtype=jnp.int32)
    ang = (k1[:, None] * m[None, :]).astype(F32) * (2.0 * math.pi / seq)
    tw_cos, tw_sin = jnp.cos(ang), jnp.sin(ang)
    c2, s2 = _dft_cos_sin(n2)
    stage2 = jnp.concatenate([jnp.concatenate([c2, s2], axis=1),
                              jnp.concatenate([-s2, c2], axis=1)], axis=0).astype(BF16)
    cc, sc = _dft_cos_sin(GROUP_DIM, scale=(seq * GROUP_DIM) ** -0.5)
    return stage1, tw_cos, tw_sin, stage2, cc.astype(BF16), sc.astype(BF16)


def _fft_fused_kernel(x_ref, w1_ref, twc_ref, tws_ref, w2_ref, cc_ref, sc_ref, o_ref, xs_ref, bre_ref, bim_ref,
                      *, n2):
    n1 = FFT_N1
    xs_ref[...] = x_ref[0].astype(F32)
    twc = twc_ref[...]
    tws = tws_ref[...]
    for m in range(0, n2, FFT_A_PER_STEP):
        cols = [xs_ref[pl.ds(m + t, n1, stride=n2), :] for t in range(FFT_A_PER_STEP)]
        a = jnp.dot(w1_ref[...], jnp.concatenate(cols, axis=1).astype(BF16), preferred_element_type=F32)
        for t in range(FFT_A_PER_STEP):
            re = a[:n1, t * GROUP_DIM:(t + 1) * GROUP_DIM]
            im = a[n1:, t * GROUP_DIM:(t + 1) * GROUP_DIM]
            c = twc[:, m + t:m + t + 1]
            s = tws[:, m + t:m + t + 1]
            bre_ref[(m + t) * n1:(m + t + 1) * n1, :] = re * c + im * s
            bim_ref[(m + t) * n1:(m + t + 1) * n1, :] = im * c - re * s
    pb = FFT_B_PER_STEP
    for k in range(0, n1, pb):
        z = [jnp.concatenate([bre_ref[pl.ds(k + t, n2, stride=n1), :],
                              bim_ref[pl.ds(k + t, n2, stride=n1), :]], axis=0) for t in range(pb)]
        y = jnp.dot(w2_ref[...], jnp.concatenate(z, axis=1).astype(BF16),
                    preferred_element_type=F32).astype(BF16)
        yre = jnp.concatenate([y[:n2, t * GROUP_DIM:(t + 1) * GROUP_DIM] for t in range(pb)], axis=0)
        yim = jnp.concatenate([y[n2:, t * GROUP_DIM:(t + 1) * GROUP_DIM] for t in range(pb)], axis=0)
        o = (jnp.dot(yre, cc_ref[...], preferred_element_type=F32)
             + jnp.dot(yim, sc_ref[...], preferred_element_type=F32))
        for t in range(pb):
            o_ref[0, pl.ds(k + t, n2, stride=n1), :] = o[t * n2:(t + 1) * n2]


def _fourier_latent_fused(cfg, p_lat3, tables):
    b, s = cfg.batch, cfg.seq
    n1, n2 = FFT_N1, s // FFT_N1
    stage1, tw_cos, tw_sin, stage2, cc, sc = tables
    gblk = cfg.off_f // GROUP_DIM
    full = lambda bi, g: (0, 0)
    out = pl.pallas_call(
        functools.partial(_fft_fused_kernel, n2=n2),
        grid=(b, GROUPS),
        in_specs=[pl.BlockSpec((1, s, GROUP_DIM), lambda bi, g: (bi, 0, gblk + g)),
                  pl.BlockSpec((2 * n1, n1), full), pl.BlockSpec((n1, n2), full), pl.BlockSpec((n1, n2), full),
                  pl.BlockSpec((2 * n2, 2 * n2), full),
                  pl.BlockSpec((GROUP_DIM, GROUP_DIM), full), pl.BlockSpec((GROUP_DIM, GROUP_DIM), full)],
        out_specs=pl.BlockSpec((1, s, GROUP_DIM), lambda bi, g: (bi, 0, g)),
        out_shape=jax.ShapeDtypeStruct((b, s, BRANCH_DIM), F32),
        scratch_shapes=[pltpu.VMEM((s, GROUP_DIM), F32)] * 3,
        compiler_params=_params("parallel", "parallel"), name="fft_fused",
    )(p_lat3, stage1, tw_cos, tw_sin, stage2, cc, sc)
    return out.reshape(b * s, BRANCH_DIM)


def _dft_dense_kernel(x_ref, c_ref, s_ref, cc_ref, sc_ref, o_ref):
    x = x_ref[0]
    yr = jnp.dot(c_ref[...], x, preferred_element_type=F32).astype(BF16)
    yi = (-jnp.dot(s_ref[...], x, preferred_element_type=F32)).astype(BF16)
    for g in range(GROUPS):
        sl = slice(g * GROUP_DIM, (g + 1) * GROUP_DIM)
        o_ref[0, :, sl] = (jnp.dot(yr[:, sl], cc_ref[...], preferred_element_type=F32)
                           + jnp.dot(yi[:, sl], sc_ref[...], preferred_element_type=F32)).astype(o_ref.dtype)


def _fourier_ctx(cfg, p_ctx):
    b, n = cfg.batch, cfg.ctx
    cn, sn = _dft_cos_sin(n)
    cc, sc = _dft_cos_sin(GROUP_DIM, scale=(n * GROUP_DIM) ** -0.5)
    fblk = cfg.off_f // BRANCH_DIM
    full = lambda bi: (0, 0)
    out = pl.pallas_call(
        _dft_dense_kernel,
        grid=(b,),
        in_specs=[pl.BlockSpec((1, n, BRANCH_DIM), lambda bi: (bi, 0, fblk)),
                  pl.BlockSpec((n, n), full), pl.BlockSpec((n, n), full),
                  pl.BlockSpec((GROUP_DIM, GROUP_DIM), full), pl.BlockSpec((GROUP_DIM, GROUP_DIM), full)],
        out_specs=pl.BlockSpec((1, n, BRANCH_DIM), lambda bi: (bi, 0, 0)),
        out_shape=jax.ShapeDtypeStruct((b, n, BRANCH_DIM), BF16),
        compiler_params=_params("parallel"), name="dft_ctx",
    )(p_ctx.reshape(b, n, cfg.n_in), cn.astype(BF16), sn.astype(BF16), cc.astype(BF16), sc.astype(BF16))
    return out.reshape(b * n, BRANCH_DIM)


def _sgu_kernel(u_ref, v_ref, g_ref, w_ref, b_ref, o_ref, *, n_chunks):
    gv = g_ref[...]
    for c in range(n_chunks):
        rows = slice(c * CHUNK, (c + 1) * CHUNK)
        u = jax.nn.gelu(u_ref[rows, :].astype(F32))
        v = jax.nn.gelu(v_ref[rows, :].astype(F32))
        vc = v - jnp.mean(v, axis=-1, keepdims=True)
        vn = (vc * lax.rsqrt(jnp.mean(vc * vc, axis=-1, keepdims=True) + EPS) * gv).astype(BF16)
        for g in range(GROUPS):
            sl = slice(g * GROUP_DIM, (g + 1) * GROUP_DIM)
            sv = jnp.dot(w_ref[g], vn[:, sl], preferred_element_type=F32) + b_ref[:, g:g + 1]
            o_ref[rows, sl] = (u[:, sl] * sv).astype(o_ref.dtype)


def _sgu(cfg, p2, sgu_w, sgu_bt, sgu_g):
    t = p2.shape[0]
    tm = min(cfg.tm_sgu, t)
    ublk, vblk = cfg.off_u // BRANCH_DIM, cfg.off_v // BRANCH_DIM
    return pl.pallas_call(
        functools.partial(_sgu_kernel, n_chunks=tm // CHUNK),
        grid=(t // tm,),
        in_specs=[pl.BlockSpec((tm, BRANCH_DIM), lambda i: (i, ublk)),
                  pl.BlockSpec((tm, BRANCH_DIM), lambda i: (i, vblk)),
                  pl.BlockSpec((1, BRANCH_DIM), lambda i: (0, 0)),
                  pl.BlockSpec((GROUPS, CHUNK, CHUNK), lambda i: (0, 0, 0)),
                  pl.BlockSpec((CHUNK, GROUPS), lambda i: (0, 0))],
        out_specs=pl.BlockSpec((tm, BRANCH_DIM), lambda i: (i, 0)),
        out_shape=jax.ShapeDtypeStruct((t, BRANCH_DIM), BF16),
        compiler_params=_params("parallel"), name="spatial_gating",
    )(p2, p2, sgu_g, sgu_w, sgu_bt)


def _merge_kernel(pg0_ref, pg1_ref, pg2_ref, f_ref, s_ref, a_ref, x_ref, bg_ref,
                  wf_ref, ws_ref, wa_ref, wo_ref, g_ref, mod_ref, *rest, d):
    o_ref = rest[-2] if len(rest) == 3 else rest[0]

    def gate(pg_ref, k):
        return jax.nn.sigmoid(pg_ref[...].astype(F32) + bg_ref[:, k * d:(k + 1) * d])

    merged = gate(pg0_ref, 0) * jnp.dot(f_ref[...].astype(BF16), wf_ref[...], preferred_element_type=F32)
    merged += gate(pg1_ref, 1) * jnp.dot(s_ref[...], ws_ref[...], preferred_element_type=F32)
    merged += gate(pg2_ref, 2) * jnp.dot(a_ref[...], wa_ref[...], preferred_element_type=F32)
    y = jnp.dot(merged.astype(BF16), wo_ref[...], preferred_element_type=F32)
    m = mod_ref[0]
    x_new = x_ref[...] + m[2:3] * _rms(y, g_ref[...])
    o_ref[...] = x_new
    if len(rest) == 3:
        g2_ref, _, h_ref = rest
        h_ref[...] = (_rms(x_new, g2_ref[...]) * (1.0 + m[4:5]) + m[3:4]).astype(h_ref.dtype)


def _resident(shape):
    return pl.BlockSpec(shape, lambda *_: (0,) * len(shape), pipeline_mode=pl.Buffered(1))


def _merge(cfg, p2, four, sgu, attn, x, b_gate, wf, ws, wa, wo, g, mod, rows_per_mod, g_ffn=None):
    t, d = x.shape
    tm = cfg.tm_merge
    row = lambda i: (i, 0)
    in_specs = [pl.BlockSpec((tm, d), lambda i: (i, 0)),
                pl.BlockSpec((tm, d), lambda i: (i, 1)),
                pl.BlockSpec((tm, d), lambda i: (i, 2)),
                pl.BlockSpec((tm, BRANCH_DIM), row), pl.BlockSpec((tm, BRANCH_DIM), row),
                pl.BlockSpec((tm, QK_DIM), row), pl.BlockSpec((tm, d), row),
                _resident((1, 3 * d)), _resident((BRANCH_DIM, d)), _resident((BRANCH_DIM, d)),
                _resident((QK_DIM, d)), _resident((d, d)), _resident((1, d)),
                pl.BlockSpec((1, 6, d), lambda i: (i * tm // rows_per_mod, 0, 0))]
    args = [p2, p2, p2, four, sgu, attn, x, b_gate, wf, ws, wa, wo, g, mod]
    out_specs = pl.BlockSpec((tm, d), row)
    out_shape = jax.ShapeDtypeStruct((t, d), F32)
    if g_ffn is not None:
        in_specs.append(_resident((1, d)))
        args.append(g_ffn)
        out_specs = [out_specs, pl.BlockSpec((tm, d), row)]
        out_shape = [out_shape, jax.ShapeDtypeStruct((t, d), BF16)]
    return pl.pallas_call(
        functools.partial(_merge_kernel, d=d),
        grid=(t // tm,), in_specs=in_specs, out_specs=out_specs, out_shape=out_shape,
        compiler_params=_params("parallel"), name="merge",
    )(*args)


def _last_active(i, n_active_ref):
    return jnp.minimum(i, jnp.maximum(n_active_ref[0] - 1, 0))


def _ffn_up_kernel(eid_ref, nact_ref, a_ref, w1_ref, w3_ref, o_ref, w1b_ref, w3b_ref):
    i = pl.program_id(1)
    active = i < nact_ref[0]
    fresh = jnp.logical_or(i == 0, eid_ref[i] != eid_ref[jnp.maximum(i - 1, 0)])

    @pl.when(jnp.logical_and(active, fresh))
    def _():
        w1b_ref[...] = w1_ref[0].astype(BF16)
        w3b_ref[...] = w3_ref[0].astype(BF16)

    @pl.when(active)
    def _():
        a = a_ref[...]
        h1 = jnp.dot(a, w1b_ref[...], preferred_element_type=F32)
        h3 = jnp.dot(a, w3b_ref[...], preferred_element_type=F32)
        o_ref[...] = (jax.nn.silu(h1) * h3).astype(o_ref.dtype)

    @pl.when(jnp.logical_not(active))
    def _():
        o_ref[...] = jnp.zeros(o_ref.shape, o_ref.dtype)


def _ffn_up(cfg, a, w1, w3, tile_eid, n_active, tm):
    r, d = a.shape
    dff = w1.shape[2]
    tn = cfg.tn_up
    grid_spec = pltpu.PrefetchScalarGridSpec(
        num_scalar_prefetch=2,
        grid=(dff // tn, r // tm),
        in_specs=[pl.BlockSpec((tm, d), lambda j, i, eid, na: (_last_active(i, na), 0)),
                  pl.BlockSpec((1, d, tn), lambda j, i, eid, na: (eid[i], 0, j)),
                  pl.BlockSpec((1, d, tn), lambda j, i, eid, na: (eid[i], 0, j))],
        out_specs=pl.BlockSpec((tm, tn), lambda j, i, eid, na: (i, j)),
        scratch_shapes=[pltpu.VMEM((d, tn), BF16), pltpu.VMEM((d, tn), BF16)],
    )
    return pl.pallas_call(
        _ffn_up_kernel, grid_spec=grid_spec,
        out_shape=jax.ShapeDtypeStruct((r, dff), BF16),
        compiler_params=_params("arbitrary", "arbitrary"), name="swiglu_up",
    )(tile_eid, n_active, a, w1, w3)


def _ffn_down_kernel(eid_ref, nact_ref, *refs, residual):
    if residual:
        h_ref, w_ref, x_ref, g_ref, mod_ref, o_ref, acc_ref = refs
    else:
        h_ref, w_ref, o_ref, acc_ref = refs
    k = pl.program_id(1)

    @pl.when(k == 0)
    def _():
        acc_ref[...] = jnp.zeros(acc_ref.shape, F32)

    @pl.when(pl.program_id(0) < nact_ref[0])
    def _():
        acc_ref[...] += jnp.dot(h_ref[...], w_ref[0], preferred_element_type=F32)

    @pl.when(k == pl.num_programs(1) - 1)
    def _():
        if residual:
            o_ref[...] = x_ref[...] + mod_ref[0][5:6] * _rms(acc_ref[...], g_ref[...])
        else:
            o_ref[...] = acc_ref[...].astype(o_ref.dtype)


def _ffn_down(cfg, h, w2, tile_eid, n_active, tm, tk, x=None, g=None, mod=None, rows_per_mod=None):
    r, dff = h.shape
    d = w2.shape[2]
    residual = x is not None
    in_specs = [pl.BlockSpec((tm, tk), lambda i, k, eid, na: (_last_active(i, na), k)),
                pl.BlockSpec((1, tk, d), lambda i, k, eid, na: (eid[i], k, 0))]
    args = [h, w2]
    if residual:
        in_specs += [pl.BlockSpec((tm, d), lambda i, k, eid, na: (i, 0)),
                     pl.BlockSpec((1, d), lambda i, k, eid, na: (0, 0)),
                     pl.BlockSpec((1, 6, d), lambda i, k, eid, na: (i * tm // rows_per_mod, 0, 0))]
        args += [x, g, mod]
    grid_spec = pltpu.PrefetchScalarGridSpec(
        num_scalar_prefetch=2,
        grid=(r // tm, dff // tk),
        in_specs=in_specs,
        out_specs=pl.BlockSpec((tm, d), lambda i, k, eid, na: (i, 0)),
        scratch_shapes=[pltpu.VMEM((tm, d), F32)],
    )
    return pl.pallas_call(
        functools.partial(_ffn_down_kernel, residual=residual), grid_spec=grid_spec,
        out_shape=jax.ShapeDtypeStruct((r, d), F32 if residual else BF16),
        compiler_params=_params("parallel", "arbitrary"),
        name="swiglu_down_residual" if residual else "swiglu_down",
    )(tile_eid, n_active, *args)


def _ffn_down_cols_kernel(eid_ref, nact_ref, h_ref, w_ref, *refs):
    o_ref, wb_ref = refs[-2:]
    i = pl.program_id(1)
    active = i < nact_ref[0]
    fresh = jnp.logical_or(i == 0, eid_ref[i] != eid_ref[jnp.maximum(i - 1, 0)])

    @pl.when(jnp.logical_and(active, fresh))
    def _():
        wb_ref[...] = w_ref[0].astype(BF16)

    @pl.when(active)
    def _():
        o_ref[...] = jnp.dot(h_ref[...], wb_ref[...], preferred_element_type=F32).astype(o_ref.dtype)

    @pl.when(jnp.logical_not(active))
    def _():
        o_ref[...] = jnp.zeros(o_ref.shape, o_ref.dtype)


def _ffn_down_cols(cfg, h, w2, tile_eid, n_active, tm, y_prev=None, tile_off=0, total_rows=None):
    r, dff = h.shape
    d = w2.shape[2]
    tn = cfg.tn_down_cols
    in_specs = [pl.BlockSpec((tm, dff), lambda n, i, eid, na: (_last_active(i, na), 0)),
                pl.BlockSpec((1, dff, tn), lambda n, i, eid, na: (eid[i], 0, n))]
    args = [tile_eid, n_active, h, w2]
    aliases = {}
    if y_prev is not None:
        in_specs.append(pl.BlockSpec(memory_space=pl.ANY))
        args.append(y_prev)
        aliases = {len(args) - 1: 0}
    grid_spec = pltpu.PrefetchScalarGridSpec(
        num_scalar_prefetch=2,
        grid=(d // tn, r // tm),
        in_specs=in_specs,
        out_specs=pl.BlockSpec((tm, tn), lambda n, i, eid, na: (i + tile_off, n)),
        scratch_shapes=[pltpu.VMEM((dff, tn), BF16)],
    )
    return pl.pallas_call(
        _ffn_down_cols_kernel, grid_spec=grid_spec,
        out_shape=jax.ShapeDtypeStruct((total_rows or r, d), BF16),
        input_output_aliases=aliases,
        compiler_params=_params("arbitrary", "arbitrary"), name="swiglu_down_cols",
    )(*args)


def _combine_kernel(y1_ref, y2_ref, wt_ref, x_ref, g_ref, mod_ref, o_ref):
    wt = wt_ref[...]
    y = wt[:, 0:1] * y1_ref[...].astype(F32) + wt[:, 1:2] * y2_ref[...].astype(F32)
    o_ref[...] = x_ref[...] + mod_ref[0][5:6] * _rms(y, g_ref[...])


def _combine(cfg, y1, y2, wt, x, g, mod, rows_per_mod):
    t, d = x.shape
    tm = min(cfg.tm_norm, t)
    row = lambda i: (i, 0)
    return pl.pallas_call(
        _combine_kernel,
        grid=(t // tm,),
        in_specs=[pl.BlockSpec((tm, d), row), pl.BlockSpec((tm, d), row),
                  pl.BlockSpec((tm, TOP_K), row), pl.BlockSpec((tm, d), row),
                  pl.BlockSpec((1, d), lambda i: (0, 0)),
                  pl.BlockSpec((1, 6, d), lambda i: (i * tm // rows_per_mod, 0, 0))],
        out_specs=pl.BlockSpec((tm, d), row),
        out_shape=jax.ShapeDtypeStruct((t, d), F32),
        compiler_params=_params("parallel"), name="moe_combine",
    )(y1, y2, wt, x, g, mod)


def _dispatch_plan(route, counts, tm):
    t = route.shape[1]
    n_pairs = TOP_K * t
    n_rows = n_pairs + N_EXPERTS * tm
    e_flat = route[:TOP_K].reshape(n_pairs)
    rank = route[TOP_K:2 * TOP_K].reshape(n_pairs)
    padded = (counts + tm - 1) // tm * tm
    ends = jnp.cumsum(padded)
    starts = ends - padded
    onehot = e_flat[:, None] == jnp.arange(N_EXPERTS, dtype=jnp.int32)[None, :]
    dest = jnp.sum(jnp.where(onehot, starts[None, :], 0), axis=1) + rank
    tok = jnp.tile(jnp.arange(t, dtype=jnp.int32), TOP_K)
    row_tok = jnp.zeros((n_rows,), jnp.int32).at[dest].set(tok, mode="promise_in_bounds", unique_indices=True)
    tile_start = jnp.arange(n_rows // tm, dtype=jnp.int32) * tm
    tile_eid = jnp.minimum(jnp.sum((tile_start[:, None] >= ends[None, :]).astype(jnp.int32), axis=1),
                           N_EXPERTS - 1)
    n_active = (ends[-1] // tm).astype(jnp.int32).reshape(1)
    return row_tok, dest.reshape(TOP_K, t), tile_eid, n_active


def _take_rows(a, idx):
    return a.at[idx].get(mode="promise_in_bounds")


def _dense_plan(t, tm):
    return jnp.zeros((t // tm,), jnp.int32), jnp.full((1,), t // tm, jnp.int32)


def _permute_w_in(w):
    b3 = 3 * BRANCH_DIM
    q3 = 3 * QK_DIM
    return jnp.concatenate([w[:, b3 + q3:], w[:, b3:b3 + q3], w[:, :b3]], axis=1).astype(BF16)


def _forward(cfg, x, c, ctx, c_ctx, w_mod, b_mod, g_norm, w_in, b_gate, w_fourier_out, w_sgu_out,
             w_attn_out, w_o, sgu_w, sgu_b, sgu_g, diff_lambda, diff_subln_g,
             ffn_w1, ffn_w3, ffn_w2, router_w, moe_w1, moe_w3, moe_w2):
    b, s, d = x.shape
    n_ctx = ctx.shape[1]
    t_lat, t_ctx = b * s, b * n_ctx
    cos, sin = _rope_tables(s)
    fft_tables = _fft_tables(s)

    c_rows = jnp.concatenate([c, c_ctx[None, :], jnp.zeros((8 - b - 1, d), F32)], axis=0)
    mod_all = _modulation(cfg, c_rows, w_mod, b_mod)

    xl = x.reshape(t_lat, d)
    xc = ctx.reshape(t_ctx, d)
    for l in range(cfg.depth):
        last = l == cfg.depth - 1
        lam_init = 0.8 - 0.6 * math.exp(-0.3 * l)
        mod_l = mod_all[l, :b].reshape(b, 6, d)
        mod_c = mod_all[l, b:b + 1].reshape(1, 6, d)
        g = g_norm[l].reshape(4, 1, d)
        w_in_l = _permute_w_in(w_in[l])
        wf, ws = w_fourier_out[l].astype(BF16), w_sgu_out[l].astype(BF16)
        wa, wo = w_attn_out[l].astype(BF16), w_o[l].astype(BF16)
        bg = b_gate[l].reshape(1, 3 * d)
        sw = sgu_w[l].astype(BF16)
        sbt = sgu_b[l].T
        sg = sgu_g[l].reshape(1, BRANCH_DIM)
        dl = diff_lambda[l]
        gsub = diff_subln_g[l].reshape(1, PAIR)
        pair_blk = lambda off: off // PAIR

        p_lat = _in_proj(cfg, xl, g[0], mod_l, s, w_in_l)
        p_lat3 = p_lat.reshape(b, s, cfg.n_in)
        qk_rot = _rope(cfg, p_lat3, cos, sin)
        if last:
            p_ctx3 = _in_proj(cfg, xc, g[0], mod_c, t_ctx,
                              w_in_l[:, cfg.off_k:cfg.off_f]).reshape(b, n_ctx, 2 * QK_DIM)
            kc_blk, vc_blk = 0, pair_blk(QK_DIM)
        else:
            p_ctx = _in_proj(cfg, xc, g[0], mod_c, t_ctx, w_in_l)
            p_ctx3 = p_ctx.reshape(b, n_ctx, cfg.n_in)
            kc_blk, vc_blk = pair_blk(cfg.off_k), pair_blk(cfg.off_va)
        al = _attention(cfg, qk_rot[0], 0, p_ctx3, kc_blk, p_ctx3, vc_blk, dl, gsub, lam_init, cfg.tq,
                        k_lat=qk_rot, v_arr=p_lat3, v_blk=pair_blk(cfg.off_va))
        four_l = _fourier_latent_fused(cfg, p_lat3, fft_tables)
        sgu_l = _sgu(cfg, p_lat, sw, sbt, sg)
        dense = l % 2 == 0
        g_ffn = g[2] if dense else None
        xl = _merge(cfg, p_lat, four_l, sgu_l, al.reshape(t_lat, QK_DIM), xl, bg, wf, ws, wa, wo, g[1], mod_l, s,
                    g_ffn=g_ffn)
        fl_l = fl_c = None
        if dense:
            xl, fl_l = xl
        if not last:
            ac = _attention(cfg, p_ctx3, pair_blk(cfg.off_q), p_ctx3, kc_blk, p_ctx3, vc_blk, dl, gsub,
                            lam_init, n_ctx)
            four_c = _fourier_ctx(cfg, p_ctx)
            sgu_c = _sgu(cfg, p_ctx, sw, sbt, sg)
            xc = _merge(cfg, p_ctx, four_c, sgu_c, ac.reshape(t_ctx, QK_DIM), xc, bg, wf, ws, wa, wo, g[1],
                        mod_c, t_ctx, g_ffn=g_ffn)
            if dense:
                xc, fl_c = xc

        i = l // 2
        streams = [(xl, mod_l, s, fl_l)] + ([] if last else [(xc, mod_c, t_ctx, fl_c)])
        outs = []
        if dense:
            w2 = ffn_w2[i][None].astype(BF16)
            for xs, mod, rpm, fl in streams:
                t = xs.shape[0]
                tm_up = min(cfg.tm_up, t)
                eid_up, n_up = _dense_plan(t, tm_up)
                h = _ffn_up(cfg, fl, ffn_w1, ffn_w3, eid_up + i, n_up, tm_up)
                tm_dn = min(cfg.tm_down_dense, t)
                outs.append(_ffn_down(cfg, h, w2, *_dense_plan(t, tm_dn), tm_dn, cfg.tk_down_dense, x=xs,
                                      g=g[3], mod=mod, rows_per_mod=rpm))
        else:
            w1 = moe_w1.reshape(-1, d, cfg.dff)
            w3 = moe_w3.reshape(-1, d, cfg.dff)
            w2 = moe_w2.reshape(-1, cfg.dff, d)
            rwt = router_w[i].T
            tm = cfg.tm_down
            routed = [_norm_mod_router(cfg, xs, g[2], mod, rpm, 3, 4, rwt) for xs, mod, rpm, _ in streams]
            seen = jnp.zeros((N_EXPERTS,), jnp.int32)
            routes = []
            for _, route, _, counts in routed:
                experts = route[:TOP_K]
                hit = experts[..., None] == jnp.arange(N_EXPERTS, dtype=jnp.int32)
                earlier = jnp.sum(jnp.where(hit, seen, 0), axis=-1)
                routes.append(jnp.concatenate([experts, route[TOP_K:2 * TOP_K] + earlier], axis=0))
                seen = seen + counts[:N_EXPERTS, 0].astype(jnp.int32)
            fl = jnp.concatenate([r[0] for r in routed], axis=0) if len(routed) > 1 else routed[0][0]
            row_tok, pos, tile_eid, n_active = _dispatch_plan(jnp.concatenate(routes, axis=1), seen, tm)
            n_tiles = tile_eid.shape[0]
            n_chunks = cfg.moe_chunks if n_tiles % cfg.moe_chunks == 0 else 1
            tpc = n_tiles // n_chunks
            tm_dn = min(cfg.tm_down_cols, tm)
            rep = tm // tm_dn
            eid = tile_eid + i * N_EXPERTS
            y = None
            for c in range(n_chunks):
                na_c = jnp.clip(n_active - c * tpc, 0, tpc)
                eid_c = eid[c * tpc:(c + 1) * tpc]
                a = _take_rows(fl, row_tok[c * tpc * tm:(c + 1) * tpc * tm])
                h = _ffn_up(cfg, a, w1, w3, eid_c, na_c, tm)
                y = _ffn_down_cols(cfg, h, w2, jnp.repeat(eid_c, rep), na_c * rep, tm_dn, y_prev=y,
                                   tile_off=c * tpc * rep, total_rows=n_tiles * tm)
            first = 0
            for (xs, mod, rpm, _), (_, _, top_w, _) in zip(streams, routed):
                p = pos[:, first:first + xs.shape[0]]
                first += xs.shape[0]
                outs.append(_combine(cfg, _take_rows(y, p[0]), _take_rows(y, p[1]),
                                     top_w[:TOP_K].T, xs, g[3], mod, rpm))
        xl = outs[0]
        if not last:
            xc = outs[1]
    return xl.reshape(b, s, d)


def kernel(x, c, ctx, c_ctx, w_mod, b_mod, g_norm, w_in, b_gate, w_fourier_out, w_sgu_out, w_attn_out, w_o,
           sgu_w, sgu_b, sgu_g, diff_lambda, diff_subln_g, ffn_w1, ffn_w3, ffn_w2, router_w,
           moe_w1, moe_w3, moe_w2):
    return _forward(PROD, x, c, ctx, c_ctx, w_mod, b_mod, g_norm, w_in, b_gate, w_fourier_out, w_sgu_out,
                    w_attn_out, w_o, sgu_w, sgu_b, sgu_g, diff_lambda, diff_subln_g,
                    ffn_w1, ffn_w3, ffn_w2, router_w, moe_w1, moe_w3, moe_w2)
```

```python
import functools
import math
from typing import NamedTuple

import jax
import jax.numpy as jnp
from jax import lax
from jax.experimental import pallas as pl
from jax.experimental.pallas import tpu as pltpu

F32 = jnp.float32
BF16 = jnp.bfloat16

EPS = 1e-6
GRID_W = 64
ROPE_BASE = 10000.0
HEAD_DIM = 64
HEADS = 8
PAIR = 2 * HEAD_DIM
QK_DIM = HEADS * PAIR
GROUP_DIM = 128
GROUPS = 4
BRANCH_DIM = GROUPS * GROUP_DIM
CHUNK = 128
FFT_N1 = 128
FFT_A_PER_STEP = 4
FFT_B_PER_STEP = 8
N_EXPERTS = 8
TOP_K = 2
SUBLANES = 8
LANES = 128
EXPERT_ROWS = SUBLANES
ATTN_SCALE = HEAD_DIM ** -0.5
Q_SCALE = ATTN_SCALE * math.log2(math.e)
VMEM_LIMIT = 56 * 1024 * 1024


class Cfg(NamedTuple):
    d: int
    batch: int
    seq: int
    ctx: int
    dff: int
    depth: int
    tm_norm: int
    tm_mm: int
    tn_mm: int
    tm_rope: int
    tq: int
    tk: int
    tm_sgu: int
    tm_merge: int
    tm_up: int
    tn_up: int
    tm_down: int
    tm_down_cols: int
    tn_down_cols: int
    moe_chunks: int
    tm_down_dense: int
    tk_down_dense: int
    tn_mod: int

    @property
    def n_in(self):
        return 3 * self.d + 3 * QK_DIM + 3 * BRANCH_DIM

    @property
    def off_q(self):
        return 3 * self.d

    @property
    def off_k(self):
        return self.off_q + QK_DIM

    @property
    def off_va(self):
        return self.off_k + QK_DIM

    @property
    def off_f(self):
        return self.off_va + QK_DIM

    @property
    def off_u(self):
        return self.off_f + BRANCH_DIM

    @property
    def off_v(self):
        return self.off_u + BRANCH_DIM


PROD = Cfg(d=2048, batch=4, seq=8192, ctx=256, dff=5632, depth=4,
           tm_norm=512, tm_mm=1024, tn_mm=1536, tm_rope=512, tq=1024, tk=256,
           tm_sgu=1024, tm_merge=256, tm_up=1024, tn_up=512, tm_down=1024, tm_down_cols=512, tn_down_cols=512,
           moe_chunks=2, tm_down_dense=1024, tk_down_dense=512, tn_mod=1024)


def _params(*sem):
    return pltpu.CompilerParams(dimension_semantics=sem, vmem_limit_bytes=VMEM_LIMIT)


def _rms(y, g):
    return y * lax.rsqrt(jnp.mean(y * y, axis=-1, keepdims=True) + EPS) * g


def _mod_kernel(c_ref, w_ref, b_ref, o_ref):
    sc = jax.nn.silu(c_ref[...])
    o_ref[0] = jnp.dot(sc.astype(BF16), w_ref[0].astype(BF16), preferred_element_type=F32) + b_ref[0]


def _modulation(cfg, c_rows, w_mod, b_mod):
    depth, d, n6 = w_mod.shape
    rows = c_rows.shape[0]
    tn = cfg.tn_mod
    return pl.pallas_call(
        _mod_kernel,
        grid=(depth, n6 // tn),
        in_specs=[pl.BlockSpec((rows, d), lambda l, j: (0, 0)),
                  pl.BlockSpec((1, d, tn), lambda l, j: (l, 0, j)),
                  pl.BlockSpec((1, 1, tn), lambda l, j: (l, 0, j))],
        out_specs=pl.BlockSpec((1, rows, tn), lambda l, j: (l, 0, j)),
        out_shape=jax.ShapeDtypeStruct((depth, rows, n6), F32),
        compiler_params=_params("parallel", "parallel"),
        name="modulation",
    )(c_rows, w_mod, b_mod.reshape(depth, 1, n6))


def _norm_mod_router_kernel(x_ref, g_ref, mod_ref, rw_ref, tri_ref, o_ref, idx_ref, wt_ref, cnt_ref, *,
                            shift_idx, scale_idx):
    @pl.when(pl.program_id(0) == 0)
    def _():
        cnt_ref[...] = jnp.zeros(cnt_ref.shape, F32)

    y = _rms(x_ref[...], g_ref[...])
    m = mod_ref[0]
    h = y * (1.0 + m[scale_idx:scale_idx + 1]) + m[shift_idx:shift_idx + 1]
    o_ref[...] = h.astype(o_ref.dtype)
    logits = lax.dot_general(rw_ref[...], h, (((1,), (1,)), ((), ())),
                             precision=lax.Precision.HIGHEST, preferred_element_type=F32)
    row = lax.broadcasted_iota(jnp.int32, logits.shape, 0).astype(F32)
    m1 = jnp.max(logits, axis=0, keepdims=True)
    i1 = jnp.min(jnp.where(logits == m1, row, float(N_EXPERTS)), axis=0, keepdims=True)
    rest = jnp.where(row == i1, -jnp.inf, logits)
    m2 = jnp.max(rest, axis=0, keepdims=True)
    i2 = jnp.min(jnp.where(rest == m2, row, float(N_EXPERTS)), axis=0, keepdims=True)
    e2 = jnp.exp(m2 - m1)
    w1 = 1.0 / (1.0 + e2)
    w2 = e2 / (1.0 + e2)
    wt_ref[...] = jnp.where(row == 0.0, w1, jnp.where(row == 1.0, w2, 0.0))
    pick1 = jnp.where(row == i1, 1.0, 0.0)
    pick2 = jnp.where(row == i2, 1.0, 0.0)
    pre1 = jnp.dot(pick1.astype(BF16), tri_ref[...], preferred_element_type=F32)
    pre2 = jnp.dot(pick2.astype(BF16), tri_ref[...], preferred_element_type=F32)
    n1 = jnp.sum(pick1, axis=1, keepdims=True)
    n2 = jnp.sum(pick2, axis=1, keepdims=True)
    seen = cnt_ref[:, 0:1]
    r1 = jnp.sum(pick1 * (pre1 - 1.0 + seen), axis=0, keepdims=True)
    r2 = jnp.sum(pick2 * (pre2 - 1.0 + seen + n1), axis=0, keepdims=True)
    cnt_ref[...] = cnt_ref[...] + (n1 + n2)
    idx_ref[...] = jnp.where(row == 0.0, i1, jnp.where(row == 1.0, i2, jnp.where(row == 2.0, r1, jnp.where(
        row == 3.0, r2, 0.0)))).astype(jnp.int32)


def _norm_mod_router(cfg, x, g, mod, rows_per_mod, shift_idx, scale_idx, router_wt):
    t, d = x.shape
    tm = min(cfg.tm_norm, t)
    in_specs = [pl.BlockSpec((tm, d), lambda i: (i, 0)),
                pl.BlockSpec((1, d), lambda i: (0, 0)),
                pl.BlockSpec((1, 6, d), lambda i: (i * tm // rows_per_mod, 0, 0))]
    h_spec = pl.BlockSpec((tm, d), lambda i: (i, 0))
    h_shape = jax.ShapeDtypeStruct((t, d), BF16)
    r_spec = pl.BlockSpec((EXPERT_ROWS, tm), lambda i: (0, i))
    tri = jnp.triu(jnp.ones((tm, tm), BF16))
    return pl.pallas_call(
        functools.partial(_norm_mod_router_kernel, shift_idx=shift_idx, scale_idx=scale_idx),
        grid=(t // tm,),
        in_specs=in_specs + [pl.BlockSpec((N_EXPERTS, d), lambda i: (0, 0)),
                             pl.BlockSpec((tm, tm), lambda i: (0, 0))],
        out_specs=[h_spec, r_spec, r_spec, pl.BlockSpec((EXPERT_ROWS, LANES), lambda i: (0, 0))],
        out_shape=[h_shape, jax.ShapeDtypeStruct((EXPERT_ROWS, t), jnp.int32),
                   jax.ShapeDtypeStruct((EXPERT_ROWS, t), F32),
                   jax.ShapeDtypeStruct((EXPERT_ROWS, LANES), F32)],
        compiler_params=_params("arbitrary"), name="norm_mod_router",
    )(x, g, mod, router_wt, tri)


def _in_proj_kernel(x_ref, g_ref, mod_ref, w_ref, o_ref, h_ref):
    @pl.when(pl.program_id(1) == 0)
    def _():
        m = mod_ref[0]
        h_ref[...] = (_rms(x_ref[...], g_ref[...]) * (1.0 + m[1:2]) + m[0:1]).astype(h_ref.dtype)

    o_ref[...] = jnp.dot(h_ref[...], w_ref[...], preferred_element_type=F32).astype(o_ref.dtype)


def _in_proj(cfg, x, g, mod, rows_per_mod, w):
    t, d = x.shape
    n = w.shape[1]
    tm = min(cfg.tm_mm, t)
    tn = cfg.tn_mm if n % cfg.tn_mm == 0 else 512
    return pl.pallas_call(
        _in_proj_kernel,
        grid=(t // tm, n // tn),
        in_specs=[pl.BlockSpec((tm, d), lambda i, j: (i, 0)),
                  pl.BlockSpec((1, d), lambda i, j: (0, 0)),
                  pl.BlockSpec((1, 6, d), lambda i, j: (i * tm // rows_per_mod, 0, 0)),
                  pl.BlockSpec((d, tn), lambda i, j: (0, j))],
        out_specs=pl.BlockSpec((tm, tn), lambda i, j: (i, j)),
        out_shape=jax.ShapeDtypeStruct((t, n), BF16),
        scratch_shapes=[pltpu.VMEM((tm, d), BF16)],
        compiler_params=_params("parallel", "arbitrary"), name="in_proj",
    )(x, g, mod, w)


def _rope_tables(seq):
    rows = seq // GRID_W
    r = jnp.repeat(jnp.arange(rows, dtype=F32), GRID_W)
    col = jnp.tile(jnp.arange(GRID_W, dtype=F32), rows)
    n_freq = HEAD_DIM // 4
    inv = ROPE_BASE ** (-jnp.arange(n_freq, dtype=F32) / n_freq)
    ang = jnp.concatenate([r[:, None] * inv, col[:, None] * inv], axis=-1)
    cos, sin = jnp.cos(ang), jnp.sin(ang)
    return jnp.tile(cos, (1, 4)), jnp.tile(jnp.concatenate([-sin, sin], axis=-1), (1, 2))


def _rotate_pair_block(t, cos, sin):
    lane = lax.broadcasted_iota(jnp.int32, t.shape, 1)
    first_half = (lane % HEAD_DIM) < (HEAD_DIM // 2)
    partner = jnp.where(first_half, pltpu.roll(t, PAIR - HEAD_DIM // 2, 1), pltpu.roll(t, HEAD_DIM // 2, 1))
    return t * cos + partner * sin


def _rope_kernel(p_ref, cos_ref, sin_ref, o_ref):
    scale = jnp.where(pl.program_id(2) == 0, Q_SCALE, 1.0).astype(F32)
    cos = cos_ref[...] * scale
    sin = sin_ref[...] * scale
    for c in range(QK_DIM // PAIR):
        t = p_ref[0, :, c * PAIR:(c + 1) * PAIR].astype(F32)
        o_ref[0, 0, :, c * PAIR:(c + 1) * PAIR] = _rotate_pair_block(t, cos, sin).astype(o_ref.dtype)


def _rope(cfg, p3, cos, sin):
    b, s, _ = p3.shape
    tm = cfg.tm_rope
    qblk = cfg.off_q // QK_DIM
    return pl.pallas_call(
        _rope_kernel,
        grid=(b, s // tm, 2),
        in_specs=[pl.BlockSpec((1, tm, QK_DIM), lambda bi, i, j: (bi, i, qblk + j)),
                  pl.BlockSpec((tm, PAIR), lambda bi, i, j: (i, 0)),
                  pl.BlockSpec((tm, PAIR), lambda bi, i, j: (i, 0))],
        out_specs=pl.BlockSpec((1, 1, tm, QK_DIM), lambda bi, i, j: (j, bi, i, 0)),
        out_shape=jax.ShapeDtypeStruct((2, b, s, QK_DIM), BF16),
        compiler_params=_params("parallel", "parallel", "parallel"), name="rope",
    )(p3, cos, sin)


def _attn_kernel(*refs, tq, tk, n_lat):
    if n_lat:
        q_ref, kc_ref, vc_ref, k_ref, v_ref, dl_ref, g_ref, li_ref, o_ref, q2_ref, vx_ref = refs
    else:
        q_ref, kc_ref, vc_ref, dl_ref, g_ref, li_ref, o_ref, q2_ref, vx_ref = refs
    lc = kc_ref.shape[1]

    @pl.when(pl.program_id(2) == 0)
    def _():
        vx_ref[:lc, :PAIR] = vc_ref[0]
        vx_ref[:lc, PAIR:] = jnp.ones((lc, PAIR), BF16)
        if n_lat:
            vx_ref[lc:, :PAIR] = v_ref[0]
            vx_ref[lc:, PAIR:] = jnp.ones((n_lat * tk, PAIR), BF16)

    q = q_ref[0]
    if not n_lat:
        q = (q.astype(F32) * Q_SCALE).astype(BF16)
    lane = lax.broadcasted_iota(jnp.int32, q.shape, 1)
    zero = jnp.zeros_like(q)
    q2_ref[:tq] = jnp.where(lane < HEAD_DIM, q, zero)
    q2_ref[tq:] = jnp.where(lane >= HEAD_DIM, q, zero)
    q2 = q2_ref[...]

    chunks = [(kc_ref[0], 0, lc)]
    for j in range(n_lat):
        chunks.append((k_ref[0, 0, j * tk:(j + 1) * tk, :], lc + j * tk, tk))
    m = jnp.full((2 * tq, 1), -jnp.inf, F32)
    acc = jnp.zeros((2 * tq, 2 * PAIR), F32)
    for k, off, n in chunks:
        s = lax.dot_general(q2, k, (((1,), (1,)), ((), ())), preferred_element_type=F32)
        m_new = jnp.maximum(m, jnp.max(s, axis=-1, keepdims=True))
        p = jnp.exp2(s - m_new).astype(BF16)
        acc = jnp.exp2(m - m_new) * acc + jnp.dot(p, vx_ref[off:off + n, :], preferred_element_type=F32)
        m = m_new

    o12 = acc[:, :PAIR] / acc[:, PAIR:]
    dl = dl_ref[...]
    lam_init = li_ref[...]
    lam = (jnp.exp(jnp.sum(dl[0:1] * dl[1:2], axis=-1, keepdims=True))
           - jnp.exp(jnp.sum(dl[2:3] * dl[3:4], axis=-1, keepdims=True)) + lam_init)
    o = o12[:tq] - lam * o12[tq:]
    o_ref[0] = (_rms(o, g_ref[...]) * (1.0 - lam_init)).astype(o_ref.dtype)


def _attention(cfg, q_arr, q_blk, kc_arr, kc_blk, vc_arr, vc_blk, dl, g_sub, lam_init, tq,
               k_lat=None, v_arr=None, v_blk=0):
    b, lq = q_arr.shape[0], q_arr.shape[1]
    lc = kc_arr.shape[1]
    in_specs = [pl.BlockSpec((1, tq, PAIR), lambda bi, h, i: (bi, i, q_blk + h)),
                pl.BlockSpec((1, lc, PAIR), lambda bi, h, i: (bi, 0, kc_blk + h)),
                pl.BlockSpec((1, lc, PAIR), lambda bi, h, i: (bi, 0, vc_blk + h))]
    args = [q_arr, kc_arr, vc_arr]
    n_lat = 0
    if k_lat is not None:
        s = k_lat.shape[2]
        n_lat = s // cfg.tk
        in_specs += [pl.BlockSpec((1, 1, s, PAIR), lambda bi, h, i: (1, bi, 0, h)),
                     pl.BlockSpec((1, s, PAIR), lambda bi, h, i: (bi, 0, v_blk + h))]
        args += [k_lat, v_arr]
    in_specs += [pl.BlockSpec((4, HEAD_DIM), lambda bi, h, i: (0, 0)),
                 pl.BlockSpec((1, PAIR), lambda bi, h, i: (0, 0)),
                 pl.BlockSpec((1, PAIR), lambda bi, h, i: (0, 0))]
    args += [dl, g_sub, jnp.full((1, PAIR), lam_init, F32)]
    lk = lc + n_lat * cfg.tk
    return pl.pallas_call(
        functools.partial(_attn_kernel, tq=tq, tk=cfg.tk, n_lat=n_lat),
        grid=(b, HEADS, lq // tq),
        in_specs=in_specs,
        out_specs=pl.BlockSpec((1, tq, PAIR), lambda bi, h, i: (bi, i, h)),
        out_shape=jax.ShapeDtypeStruct((b, lq, QK_DIM), BF16),
        scratch_shapes=[pltpu.VMEM((2 * tq, PAIR), BF16), pltpu.VMEM((lk, 2 * PAIR), BF16)],
        compiler_params=_params("parallel", "parallel", "arbitrary"),
        name="diff_attention" if n_lat else "diff_attention_ctx",
    )(*args)


def _dft_cos_sin(n, scale=1.0):
    j = jnp.arange(n, dtype=jnp.int32)
    ang = ((j[:, None] * j[None, :]) % n).astype(F32) * (2.0 * math.pi / n)
    return jnp.cos(ang) * scale, jnp.sin(ang) * scale


def _fft_tables(seq):
    n1, n2 = FFT_N1, seq // FFT_N1
    c1, s1 = _dft_cos_sin(n1)
    stage1 = jnp.concatenate([c1, -s1], axis=0).astype(BF16)
    k1 = jnp.arange(n1, dtype=jnp.int32)
    m = jnp.arange(n2, dtype=jnp.int32)
    ang = (k1[:, None] * m[None, :]).astype(F32) * (2.0 * math.pi / seq)
    tw_cos, tw_sin = jnp.cos(ang), jnp.sin(ang)
    c2, s2 = _dft_cos_sin(n2)
    stage2 = jnp.concatenate([jnp.concatenate([c2, s2], axis=1),
                              jnp.concatenate([-s2, c2], axis=1)], axis=0).astype(BF16)
    cc, sc = _dft_cos_sin(GROUP_DIM, scale=(seq * GROUP_DIM) ** -0.5)
    return stage1, tw_cos, tw_sin, stage2, cc.astype(BF16), sc.astype(BF16)


def _fft_fused_kernel(x_ref, w1_ref, twc_ref, tws_ref, w2_ref, cc_ref, sc_ref, o_ref, xs_ref, bre_ref, bim_ref,
                      *, n2):
    n1 = FFT_N1
    xs_ref[...] = x_ref[0].astype(F32)
    twc = twc_ref[...]
    tws = tws_ref[...]
    for m in range(0, n2, FFT_A_PER_STEP):
        cols = [xs_ref[pl.ds(m + t, n1, stride=n2), :] for t in range(FFT_A_PER_STEP)]
        a = jnp.dot(w1_ref[...], jnp.concatenate(cols, axis=1).astype(BF16), preferred_element_type=F32)
        for t in range(FFT_A_PER_STEP):
            re = a[:n1, t * GROUP_DIM:(t + 1) * GROUP_DIM]
            im = a[n1:, t * GROUP_DIM:(t + 1) * GROUP_DIM]
            c = twc[:, m + t:m + t + 1]
            s = tws[:, m + t:m + t + 1]
            bre_ref[(m + t) * n1:(m + t + 1) * n1, :] = re * c + im * s
            bim_ref[(m + t) * n1:(m + t + 1) * n1, :] = im * c - re * s
    pb = FFT_B_PER_STEP
    for k in range(0, n1, pb):
        z = [jnp.concatenate([bre_ref[pl.ds(k + t, n2, stride=n1), :],
                              bim_ref[pl.ds(k + t, n2, stride=n1), :]], axis=0) for t in range(pb)]
        y = jnp.dot(w2_ref[...], jnp.concatenate(z, axis=1).astype(BF16),
                    preferred_element_type=F32).astype(BF16)
        yre = jnp.concatenate([y[:n2, t * GROUP_DIM:(t + 1) * GROUP_DIM] for t in range(pb)], axis=0)
        yim = jnp.concatenate([y[n2:, t * GROUP_DIM:(t + 1) * GROUP_DIM] for t in range(pb)], axis=0)
        o = (jnp.dot(yre, cc_ref[...], preferred_element_type=F32)
             + jnp.dot(yim, sc_ref[...], preferred_element_type=F32))
        for t in range(pb):
            o_ref[0, pl.ds(k + t, n2, stride=n1), :] = o[t * n2:(t + 1) * n2]


def _fourier_latent_fused(cfg, p_lat3, tables):
    b, s = cfg.batch, cfg.seq
    n1, n2 = FFT_N1, s // FFT_N1
    stage1, tw_cos, tw_sin, stage2, cc, sc = tables
    gblk = cfg.off_f // GROUP_DIM
    full = lambda bi, g: (0, 0)
    out = pl.pallas_call(
        functools.partial(_fft_fused_kernel, n2=n2),
        grid=(b, GROUPS),
        in_specs=[pl.BlockSpec((1, s, GROUP_DIM), lambda bi, g: (bi, 0, gblk + g)),
                  pl.BlockSpec((2 * n1, n1), full), pl.BlockSpec((n1, n2), full), pl.BlockSpec((n1, n2), full),
                  pl.BlockSpec((2 * n2, 2 * n2), full),
                  pl.BlockSpec((GROUP_DIM, GROUP_DIM), full), pl.BlockSpec((GROUP_DIM, GROUP_DIM), full)],
        out_specs=pl.BlockSpec((1, s, GROUP_DIM), lambda bi, g: (bi, 0, g)),
        out_shape=jax.ShapeDtypeStruct((b, s, BRANCH_DIM), F32),
        scratch_shapes=[pltpu.VMEM((s, GROUP_DIM), F32)] * 3,
        compiler_params=_params("parallel", "parallel"), name="fft_fused",
    )(p_lat3, stage1, tw_cos, tw_sin, stage2, cc, sc)
    return out.reshape(b * s, BRANCH_DIM)


def _dft_dense_kernel(x_ref, c_ref, s_ref, cc_ref, sc_ref, o_ref):
    x = x_ref[0]
    yr = jnp.dot(c_ref[...], x, preferred_element_type=F32).astype(BF16)
    yi = (-jnp.dot(s_ref[...], x, preferred_element_type=F32)).astype(BF16)
    for g in range(GROUPS):
        sl = slice(g * GROUP_DIM, (g + 1) * GROUP_DIM)
        o_ref[0, :, sl] = (jnp.dot(yr[:, sl], cc_ref[...], preferred_element_type=F32)
                           + jnp.dot(yi[:, sl], sc_ref[...], preferred_element_type=F32)).astype(o_ref.dtype)


def _fourier_ctx(cfg, p_ctx):
    b, n = cfg.batch, cfg.ctx
    cn, sn = _dft_cos_sin(n)
    cc, sc = _dft_cos_sin(GROUP_DIM, scale=(n * GROUP_DIM) ** -0.5)
    fblk = cfg.off_f // BRANCH_DIM
    full = lambda bi: (0, 0)
    out = pl.pallas_call(
        _dft_dense_kernel,
        grid=(b,),
        in_specs=[pl.BlockSpec((1, n, BRANCH_DIM), lambda bi: (bi, 0, fblk)),
                  pl.BlockSpec((n, n), full), pl.BlockSpec((n, n), full),
                  pl.BlockSpec((GROUP_DIM, GROUP_DIM), full), pl.BlockSpec((GROUP_DIM, GROUP_DIM), full)],
        out_specs=pl.BlockSpec((1, n, BRANCH_DIM), lambda bi: (bi, 0, 0)),
        out_shape=jax.ShapeDtypeStruct((b, n, BRANCH_DIM), BF16),
        compiler_params=_params("parallel"), name="dft_ctx",
    )(p_ctx.reshape(b, n, cfg.n_in), cn.astype(BF16), sn.astype(BF16), cc.astype(BF16), sc.astype(BF16))
    return out.reshape(b * n, BRANCH_DIM)


def _sgu_kernel(u_ref, v_ref, g_ref, w_ref, b_ref, o_ref, *, n_chunks):
    gv = g_ref[...]
    for c in range(n_chunks):
        rows = slice(c * CHUNK, (c + 1) * CHUNK)
        u = jax.nn.gelu(u_ref[rows, :].astype(F32))
        v = jax.nn.gelu(v_ref[rows, :].astype(F32))
        vc = v - jnp.mean(v, axis=-1, keepdims=True)
        vn = (vc * lax.rsqrt(jnp.mean(vc * vc, axis=-1, keepdims=True) + EPS) * gv).astype(BF16)
        for g in range(GROUPS):
            sl = slice(g * GROUP_DIM, (g + 1) * GROUP_DIM)
            sv = jnp.dot(w_ref[g], vn[:, sl], preferred_element_type=F32) + b_ref[:, g:g + 1]
            o_ref[rows, sl] = (u[:, sl] * sv).astype(o_ref.dtype)


def _sgu(cfg, p2, sgu_w, sgu_bt, sgu_g):
    t = p2.shape[0]
    tm = min(cfg.tm_sgu, t)
    ublk, vblk = cfg.off_u // BRANCH_DIM, cfg.off_v // BRANCH_DIM
    return pl.pallas_call(
        functools.partial(_sgu_kernel, n_chunks=tm // CHUNK),
        grid=(t // tm,),
        in_specs=[pl.BlockSpec((tm, BRANCH_DIM), lambda i: (i, ublk)),
                  pl.BlockSpec((tm, BRANCH_DIM), lambda i: (i, vblk)),
                  pl.BlockSpec((1, BRANCH_DIM), lambda i: (0, 0)),
                  pl.BlockSpec((GROUPS, CHUNK, CHUNK), lambda i: (0, 0, 0)),
                  pl.BlockSpec((CHUNK, GROUPS), lambda i: (0, 0))],
        out_specs=pl.BlockSpec((tm, BRANCH_DIM), lambda i: (i, 0)),
        out_shape=jax.ShapeDtypeStruct((t, BRANCH_DIM), BF16),
        compiler_params=_params("parallel"), name="spatial_gating",
    )(p2, p2, sgu_g, sgu_w, sgu_bt)


def _merge_kernel(pg0_ref, pg1_ref, pg2_ref, f_ref, s_ref, a_ref, x_ref, bg_ref,
                  wf_ref, ws_ref, wa_ref, wo_ref, g_ref, mod_ref, *rest, d):
    o_ref = rest[-2] if len(rest) == 3 else rest[0]

    def gate(pg_ref, k):
        return jax.nn.sigmoid(pg_ref[...].astype(F32) + bg_ref[:, k * d:(k + 1) * d])

    merged = gate(pg0_ref, 0) * jnp.dot(f_ref[...].astype(BF16), wf_ref[...], preferred_element_type=F32)
    merged += gate(pg1_ref, 1) * jnp.dot(s_ref[...], ws_ref[...], preferred_element_type=F32)
    merged += gate(pg2_ref, 2) * jnp.dot(a_ref[...], wa_ref[...], preferred_element_type=F32)
    y = jnp.dot(merged.astype(BF16), wo_ref[...], preferred_element_type=F32)
    m = mod_ref[0]
    x_new = x_ref[...] + m[2:3] * _rms(y, g_ref[...])
    o_ref[...] = x_new
    if len(rest) == 3:
        g2_ref, _, h_ref = rest
        h_ref[...] = (_rms(x_new, g2_ref[...]) * (1.0 + m[4:5]) + m[3:4]).astype(h_ref.dtype)


def _resident(shape):
    return pl.BlockSpec(shape, lambda *_: (0,) * len(shape), pipeline_mode=pl.Buffered(1))


def _merge(cfg, p2, four, sgu, attn, x, b_gate, wf, ws, wa, wo, g, mod, rows_per_mod, g_ffn=None):
    t, d = x.shape
    tm = cfg.tm_merge
    row = lambda i: (i, 0)
    in_specs = [pl.BlockSpec((tm, d), lambda i: (i, 0)),
                pl.BlockSpec((tm, d), lambda i: (i, 1)),
                pl.BlockSpec((tm, d), lambda i: (i, 2)),
                pl.BlockSpec((tm, BRANCH_DIM), row), pl.BlockSpec((tm, BRANCH_DIM), row),
                pl.BlockSpec((tm, QK_DIM), row), pl.BlockSpec((tm, d), row),
                _resident((1, 3 * d)), _resident((BRANCH_DIM, d)), _resident((BRANCH_DIM, d)),
                _resident((QK_DIM, d)), _resident((d, d)), _resident((1, d)),
                pl.BlockSpec((1, 6, d), lambda i: (i * tm // rows_per_mod, 0, 0))]
    args = [p2, p2, p2, four, sgu, attn, x, b_gate, wf, ws, wa, wo, g, mod]
    out_specs = pl.BlockSpec((tm, d), row)
    out_shape = jax.ShapeDtypeStruct((t, d), F32)
    if g_ffn is not None:
        in_specs.append(_resident((1, d)))
        args.append(g_ffn)
        out_specs = [out_specs, pl.BlockSpec((tm, d), row)]
        out_shape = [out_shape, jax.ShapeDtypeStruct((t, d), BF16)]
    return pl.pallas_call(
        functools.partial(_merge_kernel, d=d),
        grid=(t // tm,), in_specs=in_specs, out_specs=out_specs, out_shape=out_shape,
        compiler_params=_params("parallel"), name="merge",
    )(*args)


def _last_active(i, n_active_ref):
    return jnp.minimum(i, jnp.maximum(n_active_ref[0] - 1, 0))


def _ffn_up_kernel(eid_ref, nact_ref, a_ref, w1_ref, w3_ref, o_ref, w1b_ref, w3b_ref):
    i = pl.program_id(1)
    active = i < nact_ref[0]
    fresh = jnp.logical_or(i == 0, eid_ref[i] != eid_ref[jnp.maximum(i - 1, 0)])

    @pl.when(jnp.logical_and(active, fresh))
    def _():
        w1b_ref[...] = w1_ref[0].astype(BF16)
        w3b_ref[...] = w3_ref[0].astype(BF16)

    @pl.when(active)
    def _():
        a = a_ref[...]
        h1 = jnp.dot(a, w1b_ref[...], preferred_element_type=F32)
        h3 = jnp.dot(a, w3b_ref[...], preferred_element_type=F32)
        o_ref[...] = (jax.nn.silu(h1) * h3).astype(o_ref.dtype)

    @pl.when(jnp.logical_not(active))
    def _():
        o_ref[...] = jnp.zeros(o_ref.shape, o_ref.dtype)


def _ffn_up(cfg, a, w1, w3, tile_eid, n_active, tm):
    r, d = a.shape
    dff = w1.shape[2]
    tn = cfg.tn_up
    grid_spec = pltpu.PrefetchScalarGridSpec(
        num_scalar_prefetch=2,
        grid=(dff // tn, r // tm),
        in_specs=[pl.BlockSpec((tm, d), lambda j, i, eid, na: (_last_active(i, na), 0)),
                  pl.BlockSpec((1, d, tn), lambda j, i, eid, na: (eid[i], 0, j)),
                  pl.BlockSpec((1, d, tn), lambda j, i, eid, na: (eid[i], 0, j))],
        out_specs=pl.BlockSpec((tm, tn), lambda j, i, eid, na: (i, j)),
        scratch_shapes=[pltpu.VMEM((d, tn), BF16), pltpu.VMEM((d, tn), BF16)],
    )
    return pl.pallas_call(
        _ffn_up_kernel, grid_spec=grid_spec,
        out_shape=jax.ShapeDtypeStruct((r, dff), BF16),
        compiler_params=_params("arbitrary", "arbitrary"), name="swiglu_up",
    )(tile_eid, n_active, a, w1, w3)


def _ffn_down_kernel(eid_ref, nact_ref, *refs, residual):
    if residual:
        h_ref, w_ref, x_ref, g_ref, mod_ref, o_ref, acc_ref = refs
    else:
        h_ref, w_ref, o_ref, acc_ref = refs
    k = pl.program_id(1)

    @pl.when(k == 0)
    def _():
        acc_ref[...] = jnp.zeros(acc_ref.shape, F32)

    @pl.when(pl.program_id(0) < nact_ref[0])
    def _():
        acc_ref[...] += jnp.dot(h_ref[...], w_ref[0], preferred_element_type=F32)

    @pl.when(k == pl.num_programs(1) - 1)
    def _():
        if residual:
            o_ref[...] = x_ref[...] + mod_ref[0][5:6] * _rms(acc_ref[...], g_ref[...])
        else:
            o_ref[...] = acc_ref[...].astype(o_ref.dtype)


def _ffn_down(cfg, h, w2, tile_eid, n_active, tm, tk, x=None, g=None, mod=None, rows_per_mod=None):
    r, dff = h.shape
    d = w2.shape[2]
    residual = x is not None
    in_specs = [pl.BlockSpec((tm, tk), lambda i, k, eid, na: (_last_active(i, na), k)),
                pl.BlockSpec((1, tk, d), lambda i, k, eid, na: (eid[i], k, 0))]
    args = [h, w2]
    if residual:
        in_specs += [pl.BlockSpec((tm, d), lambda i, k, eid, na: (i, 0)),
                     pl.BlockSpec((1, d), lambda i, k, eid, na: (0, 0)),
                     pl.BlockSpec((1, 6, d), lambda i, k, eid, na: (i * tm // rows_per_mod, 0, 0))]
        args += [x, g, mod]
    grid_spec = pltpu.PrefetchScalarGridSpec(
        num_scalar_prefetch=2,
        grid=(r // tm, dff // tk),
        in_specs=in_specs,
        out_specs=pl.BlockSpec((tm, d), lambda i, k, eid, na: (i, 0)),
        scratch_shapes=[pltpu.VMEM((tm, d), F32)],
    )
    return pl.pallas_call(
        functools.partial(_ffn_down_kernel, residual=residual), grid_spec=grid_spec,
        out_shape=jax.ShapeDtypeStruct((r, d), F32 if residual else BF16),
        compiler_params=_params("parallel", "arbitrary"),
        name="swiglu_down_residual" if residual else "swiglu_down",
    )(tile_eid, n_active, *args)


def _ffn_down_cols_kernel(eid_ref, nact_ref, h_ref, w_ref, *refs):
    o_ref, wb_ref = refs[-2:]
    i = pl.program_id(1)
    active = i < nact_ref[0]
    fresh = jnp.logical_or(i == 0, eid_ref[i] != eid_ref[jnp.maximum(i - 1, 0)])

    @pl.when(jnp.logical_and(active, fresh))
    def _():
        wb_ref[...] = w_ref[0].astype(BF16)

    @pl.when(active)
    def _():
        o_ref[...] = jnp.dot(h_ref[...], wb_ref[...], preferred_element_type=F32).astype(o_ref.dtype)

    @pl.when(jnp.logical_not(active))
    def _():
        o_ref[...] = jnp.zeros(o_ref.shape, o_ref.dtype)


def _ffn_down_cols(cfg, h, w2, tile_eid, n_active, tm, y_prev=None, tile_off=0, total_rows=None):
    r, dff = h.shape
    d = w2.shape[2]
    tn = cfg.tn_down_cols
    in_specs = [pl.BlockSpec((tm, dff), lambda n, i, eid, na: (_last_active(i, na), 0)),
                pl.BlockSpec((1, dff, tn), lambda n, i, eid, na: (eid[i], 0, n))]
    args = [tile_eid, n_active, h, w2]
    aliases = {}
    if y_prev is not None:
        in_specs.append(pl.BlockSpec(memory_space=pl.ANY))
        args.append(y_prev)
        aliases = {len(args) - 1: 0}
    grid_spec = pltpu.PrefetchScalarGridSpec(
        num_scalar_prefetch=2,
        grid=(d // tn, r // tm),
        in_specs=in_specs,
        out_specs=pl.BlockSpec((tm, tn), lambda n, i, eid, na: (i + tile_off, n)),
        scratch_shapes=[pltpu.VMEM((dff, tn), BF16)],
    )
    return pl.pallas_call(
        _ffn_down_cols_kernel, grid_spec=grid_spec,
        out_shape=jax.ShapeDtypeStruct((total_rows or r, d), BF16),
        input_output_aliases=aliases,
        compiler_params=_params("arbitrary", "arbitrary"), name="swiglu_down_cols",
    )(*args)


def _combine_kernel(y1_ref, y2_ref, wt_ref, x_ref, g_ref, mod_ref, o_ref):
    wt = wt_ref[...]
    y = wt[:, 0:1] * y1_ref[...].astype(F32) + wt[:, 1:2] * y2_ref[...].astype(F32)
    o_ref[...] = x_ref[...] + mod_ref[0][5:6] * _rms(y, g_ref[...])


def _combine(cfg, y1, y2, wt, x, g, mod, rows_per_mod):
    t, d = x.shape
    tm = min(cfg.tm_norm, t)
    row = lambda i: (i, 0)
    return pl.pallas_call(
        _combine_kernel,
        grid=(t // tm,),
        in_specs=[pl.BlockSpec((tm, d), row), pl.BlockSpec((tm, d), row),
                  pl.BlockSpec((tm, TOP_K), row), pl.BlockSpec((tm, d), row),
                  pl.BlockSpec((1, d), lambda i: (0, 0)),
                  pl.BlockSpec((1, 6, d), lambda i: (i * tm // rows_per_mod, 0, 0))],
        out_specs=pl.BlockSpec((tm, d), row),
        out_shape=jax.ShapeDtypeStruct((t, d), F32),
        compiler_params=_params("parallel"), name="moe_combine",
    )(y1, y2, wt, x, g, mod)


def _dispatch_plan(route, counts, tm):
    t = route.shape[1]
    n_pairs = TOP_K * t
    n_rows = n_pairs + N_EXPERTS * tm
    e_flat = route[:TOP_K].reshape(n_pairs)
    rank = route[TOP_K:2 * TOP_K].reshape(n_pairs)
    padded = (counts + tm - 1) // tm * tm
    ends = jnp.cumsum(padded)
    starts = ends - padded
    onehot = e_flat[:, None] == jnp.arange(N_EXPERTS, dtype=jnp.int32)[None, :]
    dest = jnp.sum(jnp.where(onehot, starts[None, :], 0), axis=1) + rank
    tok = jnp.tile(jnp.arange(t, dtype=jnp.int32), TOP_K)
    row_tok = jnp.zeros((n_rows,), jnp.int32).at[dest].set(tok, mode="promise_in_bounds", unique_indices=True)
    tile_start = jnp.arange(n_rows // tm, dtype=jnp.int32) * tm
    tile_eid = jnp.minimum(jnp.sum((tile_start[:, None] >= ends[None, :]).astype(jnp.int32), axis=1),
                           N_EXPERTS - 1)
    n_active = (ends[-1] // tm).astype(jnp.int32).reshape(1)
    return row_tok, dest.reshape(TOP_K, t), tile_eid, n_active


def _take_rows(a, idx):
    return a.at[idx].get(mode="promise_in_bounds")


def _dense_plan(t, tm):
    return jnp.zeros((t // tm,), jnp.int32), jnp.full((1,), t // tm, jnp.int32)


def _permute_w_in(w):
    b3 = 3 * BRANCH_DIM
    q3 = 3 * QK_DIM
    return jnp.concatenate([w[:, b3 + q3:], w[:, b3:b3 + q3], w[:, :b3]], axis=1).astype(BF16)


def _forward(cfg, x, c, ctx, c_ctx, w_mod, b_mod, g_norm, w_in, b_gate, w_fourier_out, w_sgu_out,
             w_attn_out, w_o, sgu_w, sgu_b, sgu_g, diff_lambda, diff_subln_g,
             ffn_w1, ffn_w3, ffn_w2, router_w, moe_w1, moe_w3, moe_w2):
    b, s, d = x.shape
    n_ctx = ctx.shape[1]
    t_lat, t_ctx = b * s, b * n_ctx
    cos, sin = _rope_tables(s)
    fft_tables = _fft_tables(s)

    n_cond = -(-(b + 1) // SUBLANES) * SUBLANES
    c_rows = jnp.concatenate([c, c_ctx[None, :], jnp.zeros((n_cond - b - 1, d), F32)], axis=0)
    mod_all = _modulation(cfg, c_rows, w_mod, b_mod)

    xl = x.reshape(t_lat, d)
    xc = ctx.reshape(t_ctx, d)
    for l in range(cfg.depth):
        last = l == cfg.depth - 1
        lam_init = 0.8 - 0.6 * math.exp(-0.3 * l)
        mod_l = mod_all[l, :b].reshape(b, 6, d)
        mod_c = mod_all[l, b:b + 1].reshape(1, 6, d)
        g = g_norm[l].reshape(4, 1, d)
        w_in_l = _permute_w_in(w_in[l])
        wf, ws = w_fourier_out[l].astype(BF16), w_sgu_out[l].astype(BF16)
        wa, wo = w_attn_out[l].astype(BF16), w_o[l].astype(BF16)
        bg = b_gate[l].reshape(1, 3 * d)
        sw = sgu_w[l].astype(BF16)
        sbt = sgu_b[l].T
        sg = sgu_g[l].reshape(1, BRANCH_DIM)
        dl = diff_lambda[l]
        gsub = diff_subln_g[l].reshape(1, PAIR)
        pair_blk = lambda off: off // PAIR

        p_lat = _in_proj(cfg, xl, g[0], mod_l, s, w_in_l)
        p_lat3 = p_lat.reshape(b, s, cfg.n_in)
        qk_rot = _rope(cfg, p_lat3, cos, sin)
        if last:
            p_ctx3 = _in_proj(cfg, xc, g[0], mod_c, t_ctx,
                              w_in_l[:, cfg.off_k:cfg.off_f]).reshape(b, n_ctx, 2 * QK_DIM)
            kc_blk, vc_blk = 0, pair_blk(QK_DIM)
        else:
            p_ctx = _in_proj(cfg, xc, g[0], mod_c, t_ctx, w_in_l)
            p_ctx3 = p_ctx.reshape(b, n_ctx, cfg.n_in)
            kc_blk, vc_blk = pair_blk(cfg.off_k), pair_blk(cfg.off_va)
        al = _attention(cfg, qk_rot[0], 0, p_ctx3, kc_blk, p_ctx3, vc_blk, dl, gsub, lam_init, cfg.tq,
                        k_lat=qk_rot, v_arr=p_lat3, v_blk=pair_blk(cfg.off_va))
        four_l = _fourier_latent_fused(cfg, p_lat3, fft_tables)
        sgu_l = _sgu(cfg, p_lat, sw, sbt, sg)
        dense = l % 2 == 0
        g_ffn = g[2] if dense else None
        xl = _merge(cfg, p_lat, four_l, sgu_l, al.reshape(t_lat, QK_DIM), xl, bg, wf, ws, wa, wo, g[1], mod_l, s,
                    g_ffn=g_ffn)
        fl_l = fl_c = None
        if dense:
            xl, fl_l = xl
        if not last:
            ac = _attention(cfg, p_ctx3, pair_blk(cfg.off_q), p_ctx3, kc_blk, p_ctx3, vc_blk, dl, gsub,
                            lam_init, n_ctx)
            four_c = _fourier_ctx(cfg, p_ctx)
            sgu_c = _sgu(cfg, p_ctx, sw, sbt, sg)
            xc = _merge(cfg, p_ctx, four_c, sgu_c, ac.reshape(t_ctx, QK_DIM), xc, bg, wf, ws, wa, wo, g[1],
                        mod_c, t_ctx, g_ffn=g_ffn)
            if dense:
                xc, fl_c = xc

        i = l // 2
        streams = [(xl, mod_l, s, fl_l)] + ([] if last else [(xc, mod_c, t_ctx, fl_c)])
        outs = []
        if dense:
            w2 = ffn_w2[i][None].astype(BF16)
            for xs, mod, rpm, fl in streams:
                t = xs.shape[0]
                tm_up = min(cfg.tm_up, t)
                eid_up, n_up = _dense_plan(t, tm_up)
                h = _ffn_up(cfg, fl, ffn_w1, ffn_w3, eid_up + i, n_up, tm_up)
                tm_dn = min(cfg.tm_down_dense, t)
                outs.append(_ffn_down(cfg, h, w2, *_dense_plan(t, tm_dn), tm_dn, cfg.tk_down_dense, x=xs,
                                      g=g[3], mod=mod, rows_per_mod=rpm))
        else:
            w1 = moe_w1.reshape(-1, d, cfg.dff)
            w3 = moe_w3.reshape(-1, d, cfg.dff)
            w2 = moe_w2.reshape(-1, cfg.dff, d)
            rwt = router_w[i].T
            tm = cfg.tm_down
            routed = [_norm_mod_router(cfg, xs, g[2], mod, rpm, 3, 4, rwt) for xs, mod, rpm, _ in streams]
            seen = jnp.zeros((N_EXPERTS,), jnp.int32)
            routes = []
            for _, route, _, counts in routed:
                experts = route[:TOP_K]
                hit = experts[..., None] == jnp.arange(N_EXPERTS, dtype=jnp.int32)
                earlier = jnp.sum(jnp.where(hit, seen, 0), axis=-1)
                routes.append(jnp.concatenate([experts, route[TOP_K:2 * TOP_K] + earlier], axis=0))
                seen = seen + counts[:N_EXPERTS, 0].astype(jnp.int32)
            fl = jnp.concatenate([r[0] for r in routed], axis=0) if len(routed) > 1 else routed[0][0]
            row_tok, pos, tile_eid, n_active = _dispatch_plan(jnp.concatenate(routes, axis=1), seen, tm)
            n_tiles = tile_eid.shape[0]
            n_chunks = cfg.moe_chunks if n_tiles % cfg.moe_chunks == 0 else 1
            tpc = n_tiles // n_chunks
            tm_dn = min(cfg.tm_down_cols, tm)
            rep = tm // tm_dn
            eid = tile_eid + i * N_EXPERTS
            y = None
            for c in range(n_chunks):
                na_c = jnp.clip(n_active - c * tpc, 0, tpc)
                eid_c = eid[c * tpc:(c + 1) * tpc]
                a = _take_rows(fl, row_tok[c * tpc * tm:(c + 1) * tpc * tm])
                h = _ffn_up(cfg, a, w1, w3, eid_c, na_c, tm)
                y = _ffn_down_cols(cfg, h, w2, jnp.repeat(eid_c, rep), na_c * rep, tm_dn, y_prev=y,
                                   tile_off=c * tpc * rep, total_rows=n_tiles * tm)
            first = 0
            for (xs, mod, rpm, _), (_, _, top_w, _) in zip(streams, routed):
                p = pos[:, first:first + xs.shape[0]]
                first += xs.shape[0]
                outs.append(_combine(cfg, _take_rows(y, p[0]), _take_rows(y, p[1]),
                                     top_w[:TOP_K].T, xs, g[3], mod, rpm))
        xl = outs[0]
        if not last:
            xc = outs[1]
    return xl.reshape(b, s, d)


def kernel(x, c, ctx, c_ctx, w_mod, b_mod, g_norm, w_in, b_gate, w_fourier_out, w_sgu_out, w_attn_out, w_o,
           sgu_w, sgu_b, sgu_g, diff_lambda, diff_subln_g, ffn_w1, ffn_w3, ffn_w2, router_w,
           moe_w1, moe_w3, moe_w2):
    return _forward(PROD, x, c, ctx, c_ctx, w_mod, b_mod, g_norm, w_in, b_gate, w_fourier_out, w_sgu_out,
                    w_attn_out, w_o, sgu_w, sgu_b, sgu_g, diff_lambda, diff_subln_g,
                    ffn_w1, ffn_w3, ffn_w2, router_w, moe_w1, moe_w3, moe_w2)
```

```python
import functools
import math
from typing import NamedTuple

import jax
import jax.numpy as jnp
from jax import lax
from jax.experimental import pallas as pl
from jax.experimental.pallas import tpu as pltpu

F32 = jnp.float32
BF16 = jnp.bfloat16

EPS = 1e-6
GRID_W = 64
ROPE_BASE = 10000.0
HEAD_DIM = 64
HEADS = 8
PAIR = 2 * HEAD_DIM
QK_DIM = HEADS * PAIR
GROUP_DIM = 128
GROUPS = 4
BRANCH_DIM = GROUPS * GROUP_DIM
CHUNK = 128
FFT_N1 = 128
FFT_A_PER_STEP = 4
FFT_B_PER_STEP = 8
N_EXPERTS = 8
TOP_K = 2
SUBLANES = 8
LANES = 128
EXPERT_ROWS = SUBLANES
ATTN_SCALE = HEAD_DIM ** -0.5
Q_SCALE = ATTN_SCALE * math.log2(math.e)
VMEM_LIMIT = 56 * 1024 * 1024


class Cfg(NamedTuple):
    d: int
    batch: int
    seq: int
    ctx: int
    dff: int
    depth: int
    tm_norm: int
    tm_mm: int
    tn_mm: int
    tm_rope: int
    tq: int
    tk: int
    tm_sgu: int
    tm_merge: int
    tm_up: int
    tn_up: int
    tm_down: int
    tm_down_cols: int
    tn_down_cols: int
    moe_chunks: int
    tm_down_dense: int
    tk_down_dense: int
    tn_mod: int

    @property
    def n_in(self):
        return 3 * self.d + 3 * QK_DIM + 3 * BRANCH_DIM

    @property
    def off_q(self):
        return 3 * self.d

    @property
    def off_k(self):
        return self.off_q + QK_DIM

    @property
    def off_va(self):
        return self.off_k + QK_DIM

    @property
    def off_f(self):
        return self.off_va + QK_DIM

    @property
    def off_u(self):
        return self.off_f + BRANCH_DIM

    @property
    def off_v(self):
        return self.off_u + BRANCH_DIM


PROD = Cfg(d=2048, batch=4, seq=8192, ctx=256, dff=5632, depth=4,
           tm_norm=512, tm_mm=1024, tn_mm=1536, tm_rope=512, tq=1024, tk=256,
           tm_sgu=1024, tm_merge=256, tm_up=1024, tn_up=512, tm_down=1024, tm_down_cols=512, tn_down_cols=1024,
           moe_chunks=2, tm_down_dense=1024, tk_down_dense=512, tn_mod=1024)


def _params(*sem):
    return pltpu.CompilerParams(dimension_semantics=sem, vmem_limit_bytes=VMEM_LIMIT)


def _rms(y, g):
    return y * lax.rsqrt(jnp.mean(y * y, axis=-1, keepdims=True) + EPS) * g


def _mod_kernel(c_ref, w_ref, b_ref, o_ref):
    sc = jax.nn.silu(c_ref[...])
    o_ref[0] = jnp.dot(sc.astype(BF16), w_ref[0].astype(BF16), preferred_element_type=F32) + b_ref[0]


def _modulation(cfg, c_rows, w_mod, b_mod):
    depth, d, n6 = w_mod.shape
    rows = c_rows.shape[0]
    tn = cfg.tn_mod
    return pl.pallas_call(
        _mod_kernel,
        grid=(depth, n6 // tn),
        in_specs=[pl.BlockSpec((rows, d), lambda l, j: (0, 0)),
                  pl.BlockSpec((1, d, tn), lambda l, j: (l, 0, j)),
                  pl.BlockSpec((1, 1, tn), lambda l, j: (l, 0, j))],
        out_specs=pl.BlockSpec((1, rows, tn), lambda l, j: (l, 0, j)),
        out_shape=jax.ShapeDtypeStruct((depth, rows, n6), F32),
        compiler_params=_params("parallel", "parallel"),
        name="modulation",
    )(c_rows, w_mod, b_mod.reshape(depth, 1, n6))


def _norm_mod_router_kernel(x_ref, g_ref, mod_ref, rw_ref, tri_ref, o_ref, idx_ref, wt_ref, cnt_ref, *,
                            shift_idx, scale_idx):
    @pl.when(pl.program_id(0) == 0)
    def _():
        cnt_ref[...] = jnp.zeros(cnt_ref.shape, F32)

    y = _rms(x_ref[...], g_ref[...])
    m = mod_ref[0]
    h = y * (1.0 + m[scale_idx:scale_idx + 1]) + m[shift_idx:shift_idx + 1]
    o_ref[...] = h.astype(o_ref.dtype)
    logits = lax.dot_general(rw_ref[...], h, (((1,), (1,)), ((), ())),
                             precision=lax.Precision.HIGHEST, preferred_element_type=F32)
    row = lax.broadcasted_iota(jnp.int32, logits.shape, 0).astype(F32)
    m1 = jnp.max(logits, axis=0, keepdims=True)
    i1 = jnp.min(jnp.where(logits == m1, row, float(N_EXPERTS)), axis=0, keepdims=True)
    rest = jnp.where(row == i1, -jnp.inf, logits)
    m2 = jnp.max(rest, axis=0, keepdims=True)
    i2 = jnp.min(jnp.where(rest == m2, row, float(N_EXPERTS)), axis=0, keepdims=True)
    e2 = jnp.exp(m2 - m1)
    w1 = 1.0 / (1.0 + e2)
    w2 = e2 / (1.0 + e2)
    wt_ref[...] = jnp.where(row == 0.0, w1, jnp.where(row == 1.0, w2, 0.0))
    pick1 = jnp.where(row == i1, 1.0, 0.0)
    pick2 = jnp.where(row == i2, 1.0, 0.0)
    pre1 = jnp.dot(pick1.astype(BF16), tri_ref[...], preferred_element_type=F32)
    pre2 = jnp.dot(pick2.astype(BF16), tri_ref[...], preferred_element_type=F32)
    n1 = jnp.sum(pick1, axis=1, keepdims=True)
    n2 = jnp.sum(pick2, axis=1, keepdims=True)
    seen = cnt_ref[:, 0:1]
    r1 = jnp.sum(pick1 * (pre1 - 1.0 + seen), axis=0, keepdims=True)
    r2 = jnp.sum(pick2 * (pre2 - 1.0 + seen + n1), axis=0, keepdims=True)
    cnt_ref[...] = cnt_ref[...] + (n1 + n2)
    idx_ref[...] = jnp.where(row == 0.0, i1, jnp.where(row == 1.0, i2, jnp.where(row == 2.0, r1, jnp.where(
        row == 3.0, r2, 0.0)))).astype(jnp.int32)


def _norm_mod_router(cfg, x, g, mod, rows_per_mod, shift_idx, scale_idx, router_wt):
    t, d = x.shape
    tm = min(cfg.tm_norm, t)
    in_specs = [pl.BlockSpec((tm, d), lambda i: (i, 0)),
                pl.BlockSpec((1, d), lambda i: (0, 0)),
                pl.BlockSpec((1, 6, d), lambda i: (i * tm // rows_per_mod, 0, 0))]
    h_spec = pl.BlockSpec((tm, d), lambda i: (i, 0))
    h_shape = jax.ShapeDtypeStruct((t, d), BF16)
    r_spec = pl.BlockSpec((EXPERT_ROWS, tm), lambda i: (0, i))
    tri = jnp.triu(jnp.ones((tm, tm), BF16))
    return pl.pallas_call(
        functools.partial(_norm_mod_router_kernel, shift_idx=shift_idx, scale_idx=scale_idx),
        grid=(t // tm,),
        in_specs=in_specs + [pl.BlockSpec((N_EXPERTS, d), lambda i: (0, 0)),
                             pl.BlockSpec((tm, tm), lambda i: (0, 0))],
        out_specs=[h_spec, r_spec, r_spec, pl.BlockSpec((EXPERT_ROWS, LANES), lambda i: (0, 0))],
        out_shape=[h_shape, jax.ShapeDtypeStruct((EXPERT_ROWS, t), jnp.int32),
                   jax.ShapeDtypeStruct((EXPERT_ROWS, t), F32),
                   jax.ShapeDtypeStruct((EXPERT_ROWS, LANES), F32)],
        compiler_params=_params("arbitrary"), name="norm_mod_router",
    )(x, g, mod, router_wt, tri)


def _in_proj_kernel(x_ref, g_ref, mod_ref, w_ref, o_ref, h_ref):
    @pl.when(pl.program_id(1) == 0)
    def _():
        m = mod_ref[0]
        h_ref[...] = (_rms(x_ref[...], g_ref[...]) * (1.0 + m[1:2]) + m[0:1]).astype(h_ref.dtype)

    o_ref[...] = jnp.dot(h_ref[...], w_ref[...], preferred_element_type=F32).astype(o_ref.dtype)


def _in_proj(cfg, x, g, mod, rows_per_mod, w):
    t, d = x.shape
    n = w.shape[1]
    tm = min(cfg.tm_mm, t)
    tn = cfg.tn_mm if n % cfg.tn_mm == 0 else 512
    return pl.pallas_call(
        _in_proj_kernel,
        grid=(t // tm, n // tn),
        in_specs=[pl.BlockSpec((tm, d), lambda i, j: (i, 0)),
                  pl.BlockSpec((1, d), lambda i, j: (0, 0)),
                  pl.BlockSpec((1, 6, d), lambda i, j: (i * tm // rows_per_mod, 0, 0)),
                  pl.BlockSpec((d, tn), lambda i, j: (0, j))],
        out_specs=pl.BlockSpec((tm, tn), lambda i, j: (i, j)),
        out_shape=jax.ShapeDtypeStruct((t, n), BF16),
        scratch_shapes=[pltpu.VMEM((tm, d), BF16)],
        compiler_params=_params("parallel", "arbitrary"), name="in_proj",
    )(x, g, mod, w)


def _rope_tables(seq):
    rows = seq // GRID_W
    r = jnp.repeat(jnp.arange(rows, dtype=F32), GRID_W)
    col = jnp.tile(jnp.arange(GRID_W, dtype=F32), rows)
    n_freq = HEAD_DIM // 4
    inv = ROPE_BASE ** (-jnp.arange(n_freq, dtype=F32) / n_freq)
    ang = jnp.concatenate([r[:, None] * inv, col[:, None] * inv], axis=-1)
    cos, sin = jnp.cos(ang), jnp.sin(ang)
    return jnp.tile(cos, (1, 4)), jnp.tile(jnp.concatenate([-sin, sin], axis=-1), (1, 2))


def _rotate_pair_block(t, cos, sin):
    lane = lax.broadcasted_iota(jnp.int32, t.shape, 1)
    first_half = (lane % HEAD_DIM) < (HEAD_DIM // 2)
    partner = jnp.where(first_half, pltpu.roll(t, PAIR - HEAD_DIM // 2, 1), pltpu.roll(t, HEAD_DIM // 2, 1))
    return t * cos + partner * sin


def _rope_kernel(p_ref, cos_ref, sin_ref, o_ref):
    scale = jnp.where(pl.program_id(2) == 0, Q_SCALE, 1.0).astype(F32)
    cos = cos_ref[...] * scale
    sin = sin_ref[...] * scale
    for c in range(QK_DIM // PAIR):
        t = p_ref[0, :, c * PAIR:(c + 1) * PAIR].astype(F32)
        o_ref[0, 0, :, c * PAIR:(c + 1) * PAIR] = _rotate_pair_block(t, cos, sin).astype(o_ref.dtype)


def _rope(cfg, p3, cos, sin):
    b, s, _ = p3.shape
    tm = cfg.tm_rope
    qblk = cfg.off_q // QK_DIM
    return pl.pallas_call(
        _rope_kernel,
        grid=(b, s // tm, 2),
        in_specs=[pl.BlockSpec((1, tm, QK_DIM), lambda bi, i, j: (bi, i, qblk + j)),
                  pl.BlockSpec((tm, PAIR), lambda bi, i, j: (i, 0)),
                  pl.BlockSpec((tm, PAIR), lambda bi, i, j: (i, 0))],
        out_specs=pl.BlockSpec((1, 1, tm, QK_DIM), lambda bi, i, j: (j, bi, i, 0)),
        out_shape=jax.ShapeDtypeStruct((2, b, s, QK_DIM), BF16),
        compiler_params=_params("parallel", "parallel", "parallel"), name="rope",
    )(p3, cos, sin)


def _attn_kernel(*refs, tq, tk, n_lat):
    if n_lat:
        q_ref, kc_ref, vc_ref, k_ref, v_ref, dl_ref, g_ref, li_ref, o_ref, q2_ref, vx_ref = refs
    else:
        q_ref, kc_ref, vc_ref, dl_ref, g_ref, li_ref, o_ref, q2_ref, vx_ref = refs
    lc = kc_ref.shape[1]

    @pl.when(pl.program_id(2) == 0)
    def _():
        vx_ref[:lc, :PAIR] = vc_ref[0]
        vx_ref[:lc, PAIR:] = jnp.ones((lc, PAIR), BF16)
        if n_lat:
            vx_ref[lc:, :PAIR] = v_ref[0]
            vx_ref[lc:, PAIR:] = jnp.ones((n_lat * tk, PAIR), BF16)

    q = q_ref[0]
    if not n_lat:
        q = (q.astype(F32) * Q_SCALE).astype(BF16)
    lane = lax.broadcasted_iota(jnp.int32, q.shape, 1)
    zero = jnp.zeros_like(q)
    q2_ref[:tq] = jnp.where(lane < HEAD_DIM, q, zero)
    q2_ref[tq:] = jnp.where(lane >= HEAD_DIM, q, zero)
    q2 = q2_ref[...]

    chunks = [(kc_ref[0], 0, lc)]
    for j in range(n_lat):
        chunks.append((k_ref[0, 0, j * tk:(j + 1) * tk, :], lc + j * tk, tk))
    m = jnp.full((2 * tq, 1), -jnp.inf, F32)
    acc = jnp.zeros((2 * tq, 2 * PAIR), F32)
    for k, off, n in chunks:
        s = lax.dot_general(q2, k, (((1,), (1,)), ((), ())), preferred_element_type=F32)
        m_new = jnp.maximum(m, jnp.max(s, axis=-1, keepdims=True))
        p = jnp.exp2(s - m_new).astype(BF16)
        acc = jnp.exp2(m - m_new) * acc + jnp.dot(p, vx_ref[off:off + n, :], preferred_element_type=F32)
        m = m_new

    o12 = acc[:, :PAIR] / acc[:, PAIR:]
    dl = dl_ref[...]
    lam_init = li_ref[...]
    lam = (jnp.exp(jnp.sum(dl[0:1] * dl[1:2], axis=-1, keepdims=True))
           - jnp.exp(jnp.sum(dl[2:3] * dl[3:4], axis=-1, keepdims=True)) + lam_init)
    o = o12[:tq] - lam * o12[tq:]
    o_ref[0] = (_rms(o, g_ref[...]) * (1.0 - lam_init)).astype(o_ref.dtype)


def _attention(cfg, q_arr, q_blk, kc_arr, kc_blk, vc_arr, vc_blk, dl, g_sub, lam_init, tq,
               k_lat=None, v_arr=None, v_blk=0):
    b, lq = q_arr.shape[0], q_arr.shape[1]
    lc = kc_arr.shape[1]
    in_specs = [pl.BlockSpec((1, tq, PAIR), lambda bi, h, i: (bi, i, q_blk + h)),
                pl.BlockSpec((1, lc, PAIR), lambda bi, h, i: (bi, 0, kc_blk + h)),
                pl.BlockSpec((1, lc, PAIR), lambda bi, h, i: (bi, 0, vc_blk + h))]
    args = [q_arr, kc_arr, vc_arr]
    n_lat = 0
    if k_lat is not None:
        s = k_lat.shape[2]
        n_lat = s // cfg.tk
        in_specs += [pl.BlockSpec((1, 1, s, PAIR), lambda bi, h, i: (1, bi, 0, h)),
                     pl.BlockSpec((1, s, PAIR), lambda bi, h, i: (bi, 0, v_blk + h))]
        args += [k_lat, v_arr]
    in_specs += [pl.BlockSpec((4, HEAD_DIM), lambda bi, h, i: (0, 0)),
                 pl.BlockSpec((1, PAIR), lambda bi, h, i: (0, 0)),
                 pl.BlockSpec((1, PAIR), lambda bi, h, i: (0, 0))]
    args += [dl, g_sub, jnp.full((1, PAIR), lam_init, F32)]
    lk = lc + n_lat * cfg.tk
    return pl.pallas_call(
        functools.partial(_attn_kernel, tq=tq, tk=cfg.tk, n_lat=n_lat),
        grid=(b, HEADS, lq // tq),
        in_specs=in_specs,
        out_specs=pl.BlockSpec((1, tq, PAIR), lambda bi, h, i: (bi, i, h)),
        out_shape=jax.ShapeDtypeStruct((b, lq, QK_DIM), BF16),
        scratch_shapes=[pltpu.VMEM((2 * tq, PAIR), BF16), pltpu.VMEM((lk, 2 * PAIR), BF16)],
        compiler_params=_params("parallel", "parallel", "arbitrary"),
        name="diff_attention" if n_lat else "diff_attention_ctx",
    )(*args)


def _dft_cos_sin(n, scale=1.0):
    j = jnp.arange(n, dtype=jnp.int32)
    ang = ((j[:, None] * j[None, :]) % n).astype(F32) * (2.0 * math.pi / n)
    return jnp.cos(ang) * scale, jnp.sin(ang) * scale


def _fft_tables(seq):
    n1, n2 = FFT_N1, seq // FFT_N1
    c1, s1 = _dft_cos_sin(n1)
    stage1 = jnp.concatenate([c1, -s1], axis=0).astype(BF16)
    k1 = jnp.arange(n1, dtype=jnp.int32)
    m = jnp.arange(n2, dtype=jnp.int32)
    ang = (k1[:, None] * m[None, :]).astype(F32) * (2.0 * math.pi / seq)
    tw_cos, tw_sin = jnp.cos(ang), jnp.sin(ang)
    c2, s2 = _dft_cos_sin(n2)
    stage2 = jnp.concatenate([jnp.concatenate([c2, s2], axis=1),
                              jnp.concatenate([-s2, c2], axis=1)], axis=0).astype(BF16)
    cc, sc = _dft_cos_sin(GROUP_DIM, scale=(seq * GROUP_DIM) ** -0.5)
    return stage1, tw_cos, tw_sin, stage2, cc.astype(BF16), sc.astype(BF16)


def _fft_fused_kernel(x_ref, w1_ref, twc_ref, tws_ref, w2_ref, cc_ref, sc_ref, o_ref, xs_ref, bre_ref, bim_ref,
                      *, n2):
    n1 = FFT_N1
    xs_ref[...] = x_ref[0].astype(F32)
    twc = twc_ref[...]
    tws = tws_ref[...]
    for m in range(0, n2, FFT_A_PER_STEP):
        cols = [xs_ref[pl.ds(m + t, n1, stride=n2), :] for t in range(FFT_A_PER_STEP)]
        a = jnp.dot(w1_ref[...], jnp.concatenate(cols, axis=1).astype(BF16), preferred_element_type=F32)
        for t in range(FFT_A_PER_STEP):
            re = a[:n1, t * GROUP_DIM:(t + 1) * GROUP_DIM]
            im = a[n1:, t * GROUP_DIM:(t + 1) * GROUP_DIM]
            c = twc[:, m + t:m + t + 1]
            s = tws[:, m + t:m + t + 1]
            bre_ref[(m + t) * n1:(m + t + 1) * n1, :] = re * c + im * s
            bim_ref[(m + t) * n1:(m + t + 1) * n1, :] = im * c - re * s
    pb = FFT_B_PER_STEP
    for k in range(0, n1, pb):
        z = [jnp.concatenate([bre_ref[pl.ds(k + t, n2, stride=n1), :],
                              bim_ref[pl.ds(k + t, n2, stride=n1), :]], axis=0) for t in range(pb)]
        y = jnp.dot(w2_ref[...], jnp.concatenate(z, axis=1).astype(BF16),
                    preferred_element_type=F32).astype(BF16)
        yre = jnp.concatenate([y[:n2, t * GROUP_DIM:(t + 1) * GROUP_DIM] for t in range(pb)], axis=0)
        yim = jnp.concatenate([y[n2:, t * GROUP_DIM:(t + 1) * GROUP_DIM] for t in range(pb)], axis=0)
        o = (jnp.dot(yre, cc_ref[...], preferred_element_type=F32)
             + jnp.dot(yim, sc_ref[...], preferred_element_type=F32))
        for t in range(pb):
            o_ref[0, pl.ds(k + t, n2, stride=n1), :] = o[t * n2:(t + 1) * n2]


def _fourier_latent_fused(cfg, p_lat3, tables):
    b, s = cfg.batch, cfg.seq
    n1, n2 = FFT_N1, s // FFT_N1
    stage1, tw_cos, tw_sin, stage2, cc, sc = tables
    gblk = cfg.off_f // GROUP_DIM
    full = lambda bi, g: (0, 0)
    out = pl.pallas_call(
        functools.partial(_fft_fused_kernel, n2=n2),
        grid=(b, GROUPS),
        in_specs=[pl.BlockSpec((1, s, GROUP_DIM), lambda bi, g: (bi, 0, gblk + g)),
                  pl.BlockSpec((2 * n1, n1), full), pl.BlockSpec((n1, n2), full), pl.BlockSpec((n1, n2), full),
                  pl.BlockSpec((2 * n2, 2 * n2), full),
                  pl.BlockSpec((GROUP_DIM, GROUP_DIM), full), pl.BlockSpec((GROUP_DIM, GROUP_DIM), full)],
        out_specs=pl.BlockSpec((1, s, GROUP_DIM), lambda bi, g: (bi, 0, g)),
        out_shape=jax.ShapeDtypeStruct((b, s, BRANCH_DIM), F32),
        scratch_shapes=[pltpu.VMEM((s, GROUP_DIM), F32)] * 3,
        compiler_params=_params("parallel", "parallel"), name="fft_fused",
    )(p_lat3, stage1, tw_cos, tw_sin, stage2, cc, sc)
    return out.reshape(b * s, BRANCH_DIM)


def _dft_dense_kernel(x_ref, c_ref, s_ref, cc_ref, sc_ref, o_ref):
    x = x_ref[0]
    yr = jnp.dot(c_ref[...], x, preferred_element_type=F32).astype(BF16)
    yi = (-jnp.dot(s_ref[...], x, preferred_element_type=F32)).astype(BF16)
    for g in range(GROUPS):
        sl = slice(g * GROUP_DIM, (g + 1) * GROUP_DIM)
        o_ref[0, :, sl] = (jnp.dot(yr[:, sl], cc_ref[...], preferred_element_type=F32)
                           + jnp.dot(yi[:, sl], sc_ref[...], preferred_element_type=F32)).astype(o_ref.dtype)


def _fourier_ctx(cfg, p_ctx):
    b, n = cfg.batch, cfg.ctx
    cn, sn = _dft_cos_sin(n)
    cc, sc = _dft_cos_sin(GROUP_DIM, scale=(n * GROUP_DIM) ** -0.5)
    fblk = cfg.off_f // BRANCH_DIM
    full = lambda bi: (0, 0)
    out = pl.pallas_call(
        _dft_dense_kernel,
        grid=(b,),
        in_specs=[pl.BlockSpec((1, n, BRANCH_DIM), lambda bi: (bi, 0, fblk)),
                  pl.BlockSpec((n, n), full), pl.BlockSpec((n, n), full),
                  pl.BlockSpec((GROUP_DIM, GROUP_DIM), full), pl.BlockSpec((GROUP_DIM, GROUP_DIM), full)],
        out_specs=pl.BlockSpec((1, n, BRANCH_DIM), lambda bi: (bi, 0, 0)),
        out_shape=jax.ShapeDtypeStruct((b, n, BRANCH_DIM), BF16),
        compiler_params=_params("parallel"), name="dft_ctx",
    )(p_ctx.reshape(b, n, cfg.n_in), cn.astype(BF16), sn.astype(BF16), cc.astype(BF16), sc.astype(BF16))
    return out.reshape(b * n, BRANCH_DIM)


def _sgu_kernel(u_ref, v_ref, g_ref, w_ref, b_ref, o_ref, *, n_chunks):
    gv = g_ref[...]
    for c in range(n_chunks):
        rows = slice(c * CHUNK, (c + 1) * CHUNK)
        u = jax.nn.gelu(u_ref[rows, :].astype(F32))
        v = jax.nn.gelu(v_ref[rows, :].astype(F32))
        vc = v - jnp.mean(v, axis=-1, keepdims=True)
        vn = (vc * lax.rsqrt(jnp.mean(vc * vc, axis=-1, keepdims=True) + EPS) * gv).astype(BF16)
        for g in range(GROUPS):
            sl = slice(g * GROUP_DIM, (g + 1) * GROUP_DIM)
            sv = jnp.dot(w_ref[g], vn[:, sl], preferred_element_type=F32) + b_ref[:, g:g + 1]
            o_ref[rows, sl] = (u[:, sl] * sv).astype(o_ref.dtype)


def _sgu(cfg, p2, sgu_w, sgu_bt, sgu_g):
    t = p2.shape[0]
    tm = min(cfg.tm_sgu, t)
    ublk, vblk = cfg.off_u // BRANCH_DIM, cfg.off_v // BRANCH_DIM
    return pl.pallas_call(
        functools.partial(_sgu_kernel, n_chunks=tm // CHUNK),
        grid=(t // tm,),
        in_specs=[pl.BlockSpec((tm, BRANCH_DIM), lambda i: (i, ublk)),
                  pl.BlockSpec((tm, BRANCH_DIM), lambda i: (i, vblk)),
                  pl.BlockSpec((1, BRANCH_DIM), lambda i: (0, 0)),
                  pl.BlockSpec((GROUPS, CHUNK, CHUNK), lambda i: (0, 0, 0)),
                  pl.BlockSpec((CHUNK, GROUPS), lambda i: (0, 0))],
        out_specs=pl.BlockSpec((tm, BRANCH_DIM), lambda i: (i, 0)),
        out_shape=jax.ShapeDtypeStruct((t, BRANCH_DIM), BF16),
        compiler_params=_params("parallel"), name="spatial_gating",
    )(p2, p2, sgu_g, sgu_w, sgu_bt)


def _merge_kernel(pg0_ref, pg1_ref, pg2_ref, f_ref, s_ref, a_ref, x_ref, bg_ref,
                  wf_ref, ws_ref, wa_ref, wo_ref, g_ref, mod_ref, *rest, d):
    o_ref = rest[-2] if len(rest) == 3 else rest[0]

    def gate(pg_ref, k):
        return jax.nn.sigmoid(pg_ref[...].astype(F32) + bg_ref[:, k * d:(k + 1) * d])

    merged = gate(pg0_ref, 0) * jnp.dot(f_ref[...].astype(BF16), wf_ref[...], preferred_element_type=F32)
    merged += gate(pg1_ref, 1) * jnp.dot(s_ref[...], ws_ref[...], preferred_element_type=F32)
    merged += gate(pg2_ref, 2) * jnp.dot(a_ref[...], wa_ref[...], preferred_element_type=F32)
    y = jnp.dot(merged.astype(BF16), wo_ref[...], preferred_element_type=F32)
    m = mod_ref[0]
    x_new = x_ref[...] + m[2:3] * _rms(y, g_ref[...])
    o_ref[...] = x_new
    if len(rest) == 3:
        g2_ref, _, h_ref = rest
        h_ref[...] = (_rms(x_new, g2_ref[...]) * (1.0 + m[4:5]) + m[3:4]).astype(h_ref.dtype)


def _resident(shape):
    return pl.BlockSpec(shape, lambda *_: (0,) * len(shape), pipeline_mode=pl.Buffered(1))


def _merge(cfg, p2, four, sgu, attn, x, b_gate, wf, ws, wa, wo, g, mod, rows_per_mod, g_ffn=None):
    t, d = x.shape
    tm = cfg.tm_merge
    row = lambda i: (i, 0)
    in_specs = [pl.BlockSpec((tm, d), lambda i: (i, 0)),
                pl.BlockSpec((tm, d), lambda i: (i, 1)),
                pl.BlockSpec((tm, d), lambda i: (i, 2)),
                pl.BlockSpec((tm, BRANCH_DIM), row), pl.BlockSpec((tm, BRANCH_DIM), row),
                pl.BlockSpec((tm, QK_DIM), row), pl.BlockSpec((tm, d), row),
                _resident((1, 3 * d)), _resident((BRANCH_DIM, d)), _resident((BRANCH_DIM, d)),
                _resident((QK_DIM, d)), _resident((d, d)), _resident((1, d)),
                pl.BlockSpec((1, 6, d), lambda i: (i * tm // rows_per_mod, 0, 0))]
    args = [p2, p2, p2, four, sgu, attn, x, b_gate, wf, ws, wa, wo, g, mod]
    out_specs = pl.BlockSpec((tm, d), row)
    out_shape = jax.ShapeDtypeStruct((t, d), F32)
    if g_ffn is not None:
        in_specs.append(_resident((1, d)))
        args.append(g_ffn)
        out_specs = [out_specs, pl.BlockSpec((tm, d), row)]
        out_shape = [out_shape, jax.ShapeDtypeStruct((t, d), BF16)]
    return pl.pallas_call(
        functools.partial(_merge_kernel, d=d),
        grid=(t // tm,), in_specs=in_specs, out_specs=out_specs, out_shape=out_shape,
        compiler_params=_params("parallel"), name="merge",
    )(*args)


def _last_active(i, n_active_ref):
    return jnp.minimum(i, jnp.maximum(n_active_ref[0] - 1, 0))


def _ffn_up_kernel(eid_ref, nact_ref, a_ref, w1_ref, w3_ref, o_ref, w1b_ref, w3b_ref):
    i = pl.program_id(1)
    active = i < nact_ref[0]
    fresh = jnp.logical_or(i == 0, eid_ref[i] != eid_ref[jnp.maximum(i - 1, 0)])

    @pl.when(jnp.logical_and(active, fresh))
    def _():
        w1b_ref[...] = w1_ref[0].astype(BF16)
        w3b_ref[...] = w3_ref[0].astype(BF16)

    @pl.when(active)
    def _():
        a = a_ref[...]
        h1 = jnp.dot(a, w1b_ref[...], preferred_element_type=F32)
        h3 = jnp.dot(a, w3b_ref[...], preferred_element_type=F32)
        o_ref[...] = (jax.nn.silu(h1) * h3).astype(o_ref.dtype)

    @pl.when(jnp.logical_not(active))
    def _():
        o_ref[...] = jnp.zeros(o_ref.shape, o_ref.dtype)


def _ffn_up(cfg, a, w1, w3, tile_eid, n_active, tm):
    r, d = a.shape
    dff = w1.shape[2]
    tn = cfg.tn_up
    grid_spec = pltpu.PrefetchScalarGridSpec(
        num_scalar_prefetch=2,
        grid=(dff // tn, r // tm),
        in_specs=[pl.BlockSpec((tm, d), lambda j, i, eid, na: (_last_active(i, na), 0)),
                  pl.BlockSpec((1, d, tn), lambda j, i, eid, na: (eid[i], 0, j)),
                  pl.BlockSpec((1, d, tn), lambda j, i, eid, na: (eid[i], 0, j))],
        out_specs=pl.BlockSpec((tm, tn), lambda j, i, eid, na: (i, j)),
        scratch_shapes=[pltpu.VMEM((d, tn), BF16), pltpu.VMEM((d, tn), BF16)],
    )
    return pl.pallas_call(
        _ffn_up_kernel, grid_spec=grid_spec,
        out_shape=jax.ShapeDtypeStruct((r, dff), BF16),
        compiler_params=_params("arbitrary", "arbitrary"), name="swiglu_up",
    )(tile_eid, n_active, a, w1, w3)


def _ffn_down_kernel(eid_ref, nact_ref, *refs, residual):
    if residual:
        h_ref, w_ref, x_ref, g_ref, mod_ref, o_ref, acc_ref = refs
    else:
        h_ref, w_ref, o_ref, acc_ref = refs
    k = pl.program_id(1)

    @pl.when(k == 0)
    def _():
        acc_ref[...] = jnp.zeros(acc_ref.shape, F32)

    @pl.when(pl.program_id(0) < nact_ref[0])
    def _():
        acc_ref[...] += jnp.dot(h_ref[...], w_ref[0], preferred_element_type=F32)

    @pl.when(k == pl.num_programs(1) - 1)
    def _():
        if residual:
            o_ref[...] = x_ref[...] + mod_ref[0][5:6] * _rms(acc_ref[...], g_ref[...])
        else:
            o_ref[...] = acc_ref[...].astype(o_ref.dtype)


def _ffn_down(cfg, h, w2, tile_eid, n_active, tm, tk, x=None, g=None, mod=None, rows_per_mod=None):
    r, dff = h.shape
    d = w2.shape[2]
    residual = x is not None
    in_specs = [pl.BlockSpec((tm, tk), lambda i, k, eid, na: (_last_active(i, na), k)),
                pl.BlockSpec((1, tk, d), lambda i, k, eid, na: (eid[i], k, 0))]
    args = [h, w2]
    if residual:
        in_specs += [pl.BlockSpec((tm, d), lambda i, k, eid, na: (i, 0)),
                     pl.BlockSpec((1, d), lambda i, k, eid, na: (0, 0)),
                     pl.BlockSpec((1, 6, d), lambda i, k, eid, na: (i * tm // rows_per_mod, 0, 0))]
        args += [x, g, mod]
    grid_spec = pltpu.PrefetchScalarGridSpec(
        num_scalar_prefetch=2,
        grid=(r // tm, dff // tk),
        in_specs=in_specs,
        out_specs=pl.BlockSpec((tm, d), lambda i, k, eid, na: (i, 0)),
        scratch_shapes=[pltpu.VMEM((tm, d), F32)],
    )
    return pl.pallas_call(
        functools.partial(_ffn_down_kernel, residual=residual), grid_spec=grid_spec,
        out_shape=jax.ShapeDtypeStruct((r, d), F32 if residual else BF16),
        compiler_params=_params("parallel", "arbitrary"),
        name="swiglu_down_residual" if residual else "swiglu_down",
    )(tile_eid, n_active, *args)


def _ffn_down_cols_kernel(eid_ref, nact_ref, h_ref, w_ref, *refs):
    o_ref, wb_ref = refs[-2:]
    i = pl.program_id(1)
    active = i < nact_ref[0]
    fresh = jnp.logical_or(i == 0, eid_ref[i] != eid_ref[jnp.maximum(i - 1, 0)])

    @pl.when(jnp.logical_and(active, fresh))
    def _():
        wb_ref[...] = w_ref[0].astype(BF16)

    @pl.when(active)
    def _():
        o_ref[...] = jnp.dot(h_ref[...], wb_ref[...], preferred_element_type=F32).astype(o_ref.dtype)

    @pl.when(jnp.logical_not(active))
    def _():
        o_ref[...] = jnp.zeros(o_ref.shape, o_ref.dtype)


def _ffn_down_cols(cfg, h, w2, tile_eid, n_active, tm, y_prev=None, tile_off=0, total_rows=None):
    r, dff = h.shape
    d = w2.shape[2]
    tn = cfg.tn_down_cols
    in_specs = [pl.BlockSpec((tm, dff), lambda n, i, eid, na: (_last_active(i, na), 0)),
                pl.BlockSpec((1, dff, tn), lambda n, i, eid, na: (eid[i], 0, n), pipeline_mode=pl.Buffered(1))]
    args = [tile_eid, n_active, h, w2]
    aliases = {}
    if y_prev is not None:
        in_specs.append(pl.BlockSpec(memory_space=pl.ANY))
        args.append(y_prev)
        aliases = {len(args) - 1: 0}
    grid_spec = pltpu.PrefetchScalarGridSpec(
        num_scalar_prefetch=2,
        grid=(d // tn, r // tm),
        in_specs=in_specs,
        out_specs=pl.BlockSpec((tm, tn), lambda n, i, eid, na: (i + tile_off, n)),
        scratch_shapes=[pltpu.VMEM((dff, tn), BF16)],
    )
    return pl.pallas_call(
        _ffn_down_cols_kernel, grid_spec=grid_spec,
        out_shape=jax.ShapeDtypeStruct((total_rows or r, d), BF16),
        input_output_aliases=aliases,
        compiler_params=_params("arbitrary", "arbitrary"), name="swiglu_down_cols",
    )(*args)


def _combine_kernel(y1_ref, y2_ref, wt_ref, x_ref, g_ref, mod_ref, o_ref):
    wt = wt_ref[...]
    y = wt[:, 0:1] * y1_ref[...].astype(F32) + wt[:, 1:2] * y2_ref[...].astype(F32)
    o_ref[...] = x_ref[...] + mod_ref[0][5:6] * _rms(y, g_ref[...])


def _combine(cfg, y1, y2, wt, x, g, mod, rows_per_mod):
    t, d = x.shape
    tm = min(cfg.tm_norm, t)
    row = lambda i: (i, 0)
    return pl.pallas_call(
        _combine_kernel,
        grid=(t // tm,),
        in_specs=[pl.BlockSpec((tm, d), row), pl.BlockSpec((tm, d), row),
                  pl.BlockSpec((tm, TOP_K), row), pl.BlockSpec((tm, d), row),
                  pl.BlockSpec((1, d), lambda i: (0, 0)),
                  pl.BlockSpec((1, 6, d), lambda i: (i * tm // rows_per_mod, 0, 0))],
        out_specs=pl.BlockSpec((tm, d), row),
        out_shape=jax.ShapeDtypeStruct((t, d), F32),
        compiler_params=_params("parallel"), name="moe_combine",
    )(y1, y2, wt, x, g, mod)


def _dispatch_plan(route, counts, tm):
    t = route.shape[1]
    n_pairs = TOP_K * t
    n_rows = n_pairs + N_EXPERTS * tm
    e_flat = route[:TOP_K].reshape(n_pairs)
    rank = route[TOP_K:2 * TOP_K].reshape(n_pairs)
    padded = (counts + tm - 1) // tm * tm
    ends = jnp.cumsum(padded)
    starts = ends - padded
    onehot = e_flat[:, None] == jnp.arange(N_EXPERTS, dtype=jnp.int32)[None, :]
    dest = jnp.sum(jnp.where(onehot, starts[None, :], 0), axis=1) + rank
    tok = jnp.tile(jnp.arange(t, dtype=jnp.int32), TOP_K)
    row_tok = jnp.zeros((n_rows,), jnp.int32).at[dest].set(tok, mode="promise_in_bounds", unique_indices=True)
    tile_start = jnp.arange(n_rows // tm, dtype=jnp.int32) * tm
    tile_eid = jnp.minimum(jnp.sum((tile_start[:, None] >= ends[None, :]).astype(jnp.int32), axis=1),
                           N_EXPERTS - 1)
    n_active = (ends[-1] // tm).astype(jnp.int32).reshape(1)
    return row_tok, dest.reshape(TOP_K, t), tile_eid, n_active


def _take_rows(a, idx):
    return a.at[idx].get(mode="promise_in_bounds")


def _dense_plan(t, tm):
    return jnp.zeros((t // tm,), jnp.int32), jnp.full((1,), t // tm, jnp.int32)


def _permute_w_in(w):
    b3 = 3 * BRANCH_DIM
    q3 = 3 * QK_DIM
    return jnp.concatenate([w[:, b3 + q3:], w[:, b3:b3 + q3], w[:, :b3]], axis=1).astype(BF16)


def _forward(cfg, x, c, ctx, c_ctx, w_mod, b_mod, g_norm, w_in, b_gate, w_fourier_out, w_sgu_out,
             w_attn_out, w_o, sgu_w, sgu_b, sgu_g, diff_lambda, diff_subln_g,
             ffn_w1, ffn_w3, ffn_w2, router_w, moe_w1, moe_w3, moe_w2):
    b, s, d = x.shape
    n_ctx = ctx.shape[1]
    t_lat, t_ctx = b * s, b * n_ctx
    cos, sin = _rope_tables(s)
    fft_tables = _fft_tables(s)

    n_cond = -(-(b + 1) // SUBLANES) * SUBLANES
    c_rows = jnp.concatenate([c, c_ctx[None, :], jnp.zeros((n_cond - b - 1, d), F32)], axis=0)
    mod_all = _modulation(cfg, c_rows, w_mod, b_mod)

    xl = x.reshape(t_lat, d)
    xc = ctx.reshape(t_ctx, d)
    for l in range(cfg.depth):
        last = l == cfg.depth - 1
        lam_init = 0.8 - 0.6 * math.exp(-0.3 * l)
        mod_l = mod_all[l, :b].reshape(b, 6, d)
        mod_c = mod_all[l, b:b + 1].reshape(1, 6, d)
        g = g_norm[l].reshape(4, 1, d)
        w_in_l = _permute_w_in(w_in[l])
        wf, ws = w_fourier_out[l].astype(BF16), w_sgu_out[l].astype(BF16)
        wa, wo = w_attn_out[l].astype(BF16), w_o[l].astype(BF16)
        bg = b_gate[l].reshape(1, 3 * d)
        sw = sgu_w[l].astype(BF16)
        sbt = sgu_b[l].T
        sg = sgu_g[l].reshape(1, BRANCH_DIM)
        dl = diff_lambda[l]
        gsub = diff_subln_g[l].reshape(1, PAIR)
        pair_blk = lambda off: off // PAIR

        p_lat = _in_proj(cfg, xl, g[0], mod_l, s, w_in_l)
        p_lat3 = p_lat.reshape(b, s, cfg.n_in)
        qk_rot = _rope(cfg, p_lat3, cos, sin)
        if last:
            p_ctx3 = _in_proj(cfg, xc, g[0], mod_c, t_ctx,
                              w_in_l[:, cfg.off_k:cfg.off_f]).reshape(b, n_ctx, 2 * QK_DIM)
            kc_blk, vc_blk = 0, pair_blk(QK_DIM)
        else:
            p_ctx = _in_proj(cfg, xc, g[0], mod_c, t_ctx, w_in_l)
            p_ctx3 = p_ctx.reshape(b, n_ctx, cfg.n_in)
            kc_blk, vc_blk = pair_blk(cfg.off_k), pair_blk(cfg.off_va)
        al = _attention(cfg, qk_rot[0], 0, p_ctx3, kc_blk, p_ctx3, vc_blk, dl, gsub, lam_init, cfg.tq,
                        k_lat=qk_rot, v_arr=p_lat3, v_blk=pair_blk(cfg.off_va))
        four_l = _fourier_latent_fused(cfg, p_lat3, fft_tables)
        sgu_l = _sgu(cfg, p_lat, sw, sbt, sg)
        dense = l % 2 == 0
        g_ffn = g[2] if dense else None
        xl = _merge(cfg, p_lat, four_l, sgu_l, al.reshape(t_lat, QK_DIM), xl, bg, wf, ws, wa, wo, g[1], mod_l, s,
                    g_ffn=g_ffn)
        fl_l = fl_c = None
        if dense:
            xl, fl_l = xl
        if not last:
            ac = _attention(cfg, p_ctx3, pair_blk(cfg.off_q), p_ctx3, kc_blk, p_ctx3, vc_blk, dl, gsub,
                            lam_init, n_ctx)
            four_c = _fourier_ctx(cfg, p_ctx)
            sgu_c = _sgu(cfg, p_ctx, sw, sbt, sg)
            xc = _merge(cfg, p_ctx, four_c, sgu_c, ac.reshape(t_ctx, QK_DIM), xc, bg, wf, ws, wa, wo, g[1],
                        mod_c, t_ctx, g_ffn=g_ffn)
            if dense:
                xc, fl_c = xc

        i = l // 2
        streams = [(xl, mod_l, s, fl_l)] + ([] if last else [(xc, mod_c, t_ctx, fl_c)])
        outs = []
        if dense:
            w2 = ffn_w2[i][None].astype(BF16)
            for xs, mod, rpm, fl in streams:
                t = xs.shape[0]
                tm_up = min(cfg.tm_up, t)
                eid_up, n_up = _dense_plan(t, tm_up)
                h = _ffn_up(cfg, fl, ffn_w1, ffn_w3, eid_up + i, n_up, tm_up)
                tm_dn = min(cfg.tm_down_dense, t)
                outs.append(_ffn_down(cfg, h, w2, *_dense_plan(t, tm_dn), tm_dn, cfg.tk_down_dense, x=xs,
                                      g=g[3], mod=mod, rows_per_mod=rpm))
        else:
            w1 = moe_w1.reshape(-1, d, cfg.dff)
            w3 = moe_w3.reshape(-1, d, cfg.dff)
            w2 = moe_w2.reshape(-1, cfg.dff, d)
            rwt = router_w[i].T
            tm = cfg.tm_down
            routed = [_norm_mod_router(cfg, xs, g[2], mod, rpm, 3, 4, rwt) for xs, mod, rpm, _ in streams]
            seen = jnp.zeros((N_EXPERTS,), jnp.int32)
            routes = []
            for _, route, _, counts in routed:
                experts = route[:TOP_K]
                hit = experts[..., None] == jnp.arange(N_EXPERTS, dtype=jnp.int32)
                earlier = jnp.sum(jnp.where(hit, seen, 0), axis=-1)
                routes.append(jnp.concatenate([experts, route[TOP_K:2 * TOP_K] + earlier], axis=0))
                seen = seen + counts[:N_EXPERTS, 0].astype(jnp.int32)
            fl = jnp.concatenate([r[0] for r in routed], axis=0) if len(routed) > 1 else routed[0][0]
            row_tok, pos, tile_eid, n_active = _dispatch_plan(jnp.concatenate(routes, axis=1), seen, tm)
            n_tiles = tile_eid.shape[0]
            n_chunks = cfg.moe_chunks if n_tiles % cfg.moe_chunks == 0 else 1
            tpc = n_tiles // n_chunks
            tm_dn = min(cfg.tm_down_cols, tm)
            rep = tm // tm_dn
            eid = tile_eid + i * N_EXPERTS
            y = None
            for c in range(n_chunks):
                na_c = jnp.clip(n_active - c * tpc, 0, tpc)
                eid_c = eid[c * tpc:(c + 1) * tpc]
                a = _take_rows(fl, row_tok[c * tpc * tm:(c + 1) * tpc * tm])
                h = _ffn_up(cfg, a, w1, w3, eid_c, na_c, tm)
                y = _ffn_down_cols(cfg, h, w2, jnp.repeat(eid_c, rep), na_c * rep, tm_dn, y_prev=y,
                                   tile_off=c * tpc * rep, total_rows=n_tiles * tm)
            first = 0
            for (xs, mod, rpm, _), (_, _, top_w, _) in zip(streams, routed):
                p = pos[:, first:first + xs.shape[0]]
                first += xs.shape[0]
                outs.append(_combine(cfg, _take_rows(y, p[0]), _take_rows(y, p[1]),
                                     top_w[:TOP_K].T, xs, g[3], mod, rpm))
        xl = outs[0]
        if not last:
            xc = outs[1]
    return xl.reshape(b, s, d)


def kernel(x, c, ctx, c_ctx, w_mod, b_mod, g_norm, w_in, b_gate, w_fourier_out, w_sgu_out, w_attn_out, w_o,
           sgu_w, sgu_b, sgu_g, diff_lambda, diff_subln_g, ffn_w1, ffn_w3, ffn_w2, router_w,
           moe_w1, moe_w3, moe_w2):
    return _forward(PROD, x, c, ctx, c_ctx, w_mod, b_mod, g_norm, w_in, b_gate, w_fourier_out, w_sgu_out,
                    w_attn_out, w_o, sgu_w, sgu_b, sgu_g, diff_lambda, diff_subln_g,
                    ffn_w1, ffn_w3, ffn_w2, router_w, moe_w1, moe_w3, moe_w2)
```
